```python
import math
import jax, jax.numpy as jnp
from jax import lax
import numpy as np

D_MODEL = 1024
BATCH = 8
SEQ = 8192
DEPTH = 4
DEC_BATCH = 8
DEC_SEQ = 2048
PAST_LEN = 128

N_META = 16
GRID_W = 64
N_MIXERS = 2
N_HYENA_LAYERS = (DEPTH + N_MIXERS - 1) // N_MIXERS
N_ATTN_LAYERS = DEPTH // N_MIXERS
HEAD_DIM = 128
N_HEADS = D_MODEL // HEAD_DIM
N_KV_HEADS = N_HEADS // 4
GROUP = N_HEADS // N_KV_HEADS
Q_BLOCK = 128
ROPE_THETA = 10000.0
AXIS_ROT = HEAD_DIM // 2
FILTER_EMB_DIM = 33
FILTER_BANDS = (FILTER_EMB_DIM - 1) // 2
FILTER_HIDDEN = 64
DECAY_TARGET = 1e-2
FAST_DECAY_PCT = 0.3
SLOW_DECAY_PCT = 1.5
DECAY_SHIFT = 0.05
FILTER_NORM_EPS = 1e-6
D_FF = 2816
DEEPNORM_ALPHA = (2 * DEPTH) ** 0.25
DEEPNORM_BETA = (8 * DEPTH) ** -0.25
LN_EPS = 1e-5
RMS_EPS = 1e-6

kernel_name = "hyena_gqa_axial_rope_deepnorm_meta_encoder"


def layer_norm(x, g, b):
    xf = x.astype(jnp.float32)
    mu = jnp.mean(xf, axis=-1, keepdims=True)
    xc = xf - mu
    var = jnp.mean(xc * xc, axis=-1, keepdims=True)
    y = xc * lax.rsqrt(var + LN_EPS) * g.astype(jnp.float32) + b.astype(jnp.float32)
    return y.astype(x.dtype)


def rms_norm_f32(x, g):
    xf = x.astype(jnp.float32)
    return xf * lax.rsqrt(jnp.mean(xf * xf, axis=-1, keepdims=True) + RMS_EPS) * g.astype(jnp.float32)


def dwconv3(x, w, b):
    xp = jnp.pad(x, ((0, 0), (1, 1), (0, 0)))
    return xp[:, :-2] * w[0] + xp[:, 1:-1] * w[1] + xp[:, 2:] * w[2] + b


def hyena_filter_spectrum(L, w1, b1, w2, b2, w3, b3, freq):
    f32 = jnp.float32
    pos = jnp.arange(L, dtype=f32)
    t = (pos / (L - 1))[:, None]
    w = (2.0 * math.pi / L) * pos[:, None]
    bands = jnp.linspace(1e-4, FILTER_BANDS - 1, FILTER_BANDS, dtype=f32)
    z = jnp.concatenate([t, jnp.cos(w * bands), -jnp.sin(w * bands)], axis=-1)
    fr = freq.astype(f32)
    h = jnp.sin(fr * (z @ w1.astype(f32) + b1.astype(f32)))
    h = jnp.sin(fr * (h @ w2.astype(f32) + b2.astype(f32)))
    h = h @ w3.astype(f32) + b3.astype(f32)
    max_decay = math.log(1.0 / DECAY_TARGET) / FAST_DECAY_PCT
    min_decay = math.log(1.0 / DECAY_TARGET) / SLOW_DECAY_PCT
    deltas = jnp.linspace(min_decay, max_decay, D_MODEL, dtype=f32)
    window = jnp.exp(-t * deltas) + DECAY_SHIFT
    h = h * jnp.concatenate([window, window], axis=-1)
    h = h / (jnp.sum(jnp.abs(h), axis=0, keepdims=True) + FILTER_NORM_EPS)
    hf, hb = h[:, :D_MODEL], h[:, D_MODEL:]
    k = jnp.concatenate([hf, jnp.zeros((1, D_MODEL), f32), hb[1:][::-1]], axis=0)
    return jnp.fft.rfft(k, axis=0)


def hyena_mixer(x, w_in, conv_w, conv_b, fw1, fb1, fw2, fb2, fw3, fb3, ffreq, skip, w_out):
    B, L, _ = x.shape
    u = dwconv3(x @ w_in, conv_w, conv_b)
    x0, x1, v = jnp.split(u, 3, axis=-1)
    v = (v * x1).astype(jnp.float32)
    kf = hyena_filter_spectrum(L, fw1, fb1, fw2, fb2, fw3, fb3, ffreq)
    y = jnp.fft.irfft(jnp.fft.rfft(v, n=2 * L, axis=1) * kf[None], n=2 * L, axis=1)[:, :L]
    y = y + v * skip.astype(jnp.float32)
    y = y.astype(x.dtype) * x0
    return y @ w_out


def axial_rope_tables(n_tokens):
    rows = n_tokens // GRID_W
    f32 = jnp.float32
    row = jnp.concatenate([jnp.full((N_META,), -1.0, f32),
                           jnp.repeat(jnp.arange(rows, dtype=f32), GRID_W)])
    col = jnp.concatenate([jnp.arange(N_META, dtype=f32),
                           jnp.tile(jnp.arange(GRID_W, dtype=f32), rows)])
    inv_freq = ROPE_THETA ** (-jnp.arange(0, AXIS_ROT, 2, dtype=f32) / AXIS_ROT)
    ang = jnp.concatenate([row[:, None] * inv_freq, col[:, None] * inv_freq], axis=-1)
    return jnp.cos(ang), jnp.sin(ang)


def apply_rope(x, cos, sin):
    half = HEAD_DIM // 2
    x1, x2 = x[..., :half], x[..., half:]
    c = cos[None, :, None, :]
    s = sin[None, :, None, :]
    return jnp.concatenate([x1 * c - x2 * s, x2 * c + x1 * s], axis=-1)


def attend(q, k, v):
    s = jnp.einsum('bqhgd,bkhd->bhgqk', q, k).astype(jnp.float32) * (HEAD_DIM ** -0.5)
    p = jax.nn.softmax(s, axis=-1).astype(v.dtype)
    return jnp.einsum('bhgqk,bkhd->bqhgd', p, v)


def attention_mixer(x, w_qkv, q_gain, k_gain, w_out, cos, sin):
    B, L, _ = x.shape
    nq = N_HEADS * HEAD_DIM
    nk = N_KV_HEADS * HEAD_DIM
    qkv = x @ w_qkv
    q = qkv[..., :nq].reshape(B, L, N_HEADS, HEAD_DIM)
    k = qkv[..., nq:nq + nk].reshape(B, L, N_KV_HEADS, HEAD_DIM)
    v = qkv[..., nq + nk:].reshape(B, L, N_KV_HEADS, HEAD_DIM)
    q = apply_rope(rms_norm_f32(q, q_gain), cos, sin).astype(x.dtype)
    k = apply_rope(rms_norm_f32(k, k_gain), cos, sin).astype(x.dtype)
    q = q.reshape(B, L, N_KV_HEADS, GROUP, HEAD_DIM)
    o_meta = attend(q[:, :N_META], k, v)
    n = L - N_META
    nblk = n // Q_BLOCK
    qb = q[:, N_META:].reshape(B, nblk, Q_BLOCK, N_KV_HEADS, GROUP, HEAD_DIM).transpose(1, 0, 2, 3, 4, 5)
    ob = lax.map(lambda qi: attend(qi, k, v), qb)
    o_real = ob.transpose(1, 0, 2, 3, 4, 5).reshape(B, n, N_KV_HEADS, GROUP, HEAD_DIM)
    o = jnp.concatenate([o_meta, o_real], axis=1).reshape(B, L, nq)
    return o @ w_out


def conv_ffn(x, w_in, conv_w, conv_b, w_out):
    h = x @ w_in
    g, a = jnp.split(h, 2, axis=-1)
    g = dwconv3(g, conv_w, conv_b)
    return (jax.nn.gelu(g, approximate=False) * a) @ w_out


def run_trunk(x, meta_tokens, hy_w_in, hy_conv_w, hy_conv_b, hy_filt_w1, hy_filt_b1, hy_filt_w2,
              hy_filt_b2, hy_filt_w3, hy_filt_b3, hy_filt_freq, hy_skip, hy_w_out, at_w_qkv,
              at_q_gain, at_k_gain, at_w_out, ln1_g, ln1_b, ln2_g, ln2_b, ffn_w_in, ffn_conv_w,
              ffn_conv_b, ffn_w_out):
    B, n, _ = x.shape
    meta = jnp.broadcast_to(meta_tokens.astype(x.dtype)[None], (B, N_META, D_MODEL))
    h = jnp.concatenate([meta, x], axis=1)
    cos, sin = axial_rope_tables(n)
    for i in range(DEPTH):
        j = i // N_MIXERS
        if i % N_MIXERS == 0:
            m = hyena_mixer(h, hy_w_in[j], hy_conv_w[j], hy_conv_b[j], hy_filt_w1[j], hy_filt_b1[j],
                            hy_filt_w2[j], hy_filt_b2[j], hy_filt_w3[j], hy_filt_b3[j],
                            hy_filt_freq[j], hy_skip[j], hy_w_out[j])
        else:
            m = attention_mixer(h, at_w_qkv[j], at_q_gain[j], at_k_gain[j], at_w_out[j], cos, sin)
        h = layer_norm(DEEPNORM_ALPHA * h + m, ln1_g[i], ln1_b[i])
        f = conv_ffn(h, ffn_w_in[i], ffn_conv_w[i], ffn_conv_b[i], ffn_w_out[i])
        h = layer_norm(DEEPNORM_ALPHA * h + f, ln2_g[i], ln2_b[i])
    return h[:, N_META:]


def setup_inputs(seed: int = 0) -> dict:
    key = jax.random.key(seed)
    ks = iter(jax.random.split(key, 40))
    f32 = jnp.float32
    D = D_MODEL
    NH, NA = N_HYENA_LAYERS, N_ATTN_LAYERS
    qkv_w = (N_HEADS + 2 * N_KV_HEADS) * HEAD_DIM

    def nrm(shape, scale):
        return jax.random.normal(next(ks), shape, f32) * scale

    return {
        "x_prompt": nrm((BATCH, SEQ, D), 1.0),
        "x_sample": nrm((DEC_BATCH, DEC_SEQ, D), 1.0),
        "meta_tokens": nrm((N_META, D), 1.0),
        "hy_w_in": nrm((NH, D, 3 * D), D ** -0.5),
        "hy_conv_w": nrm((NH, 3, 3 * D), 3 ** -0.5),
        "hy_conv_b": nrm((NH, 3 * D), 0.02),
        "hy_filt_w1": nrm((NH, FILTER_EMB_DIM, FILTER_HIDDEN), FILTER_EMB_DIM ** -0.5),
        "hy_filt_b1": nrm((NH, FILTER_HIDDEN), 0.02),
        "hy_filt_w2": nrm((NH, FILTER_HIDDEN, FILTER_HIDDEN), FILTER_HIDDEN ** -0.5),
        "hy_filt_b2": nrm((NH, FILTER_HIDDEN), 0.02),
        "hy_filt_w3": nrm((NH, FILTER_HIDDEN, 2 * D), FILTER_HIDDEN ** -0.5),
        "hy_filt_b3": nrm((NH, 2 * D), 0.02),
        "hy_filt_freq": 1.0 + nrm((NH, FILTER_HIDDEN), 0.05),
        "hy_skip": nrm((NH, D), 0.5),
        "hy_w_out": nrm((NH, D, D), DEEPNORM_BETA * D ** -0.5),
        "at_w_qkv": nrm((NA, D, qkv_w), D ** -0.5),
        "at_q_gain": 1.0 + nrm((NA, HEAD_DIM), 0.05),
        "at_k_gain": 1.0 + nrm((NA, HEAD_DIM), 0.05),
        "at_w_out": nrm((NA, N_HEADS * HEAD_DIM, D), DEEPNORM_BETA * (N_HEADS * HEAD_DIM) ** -0.5),
        "ln1_g": 1.0 + nrm((DEPTH, D), 0.05),
        "ln1_b": nrm((DEPTH, D), 0.02),
        "ln2_g": 1.0 + nrm((DEPTH, D), 0.05),
        "ln2_b": nrm((DEPTH, D), 0.02),
        "ffn_w_in": nrm((DEPTH, D, 2 * D_FF), D ** -0.5),
        "ffn_conv_w": nrm((DEPTH, 3, D_FF), 3 ** -0.5),
        "ffn_conv_b": nrm((DEPTH, D_FF), 0.02),
        "ffn_w_out": nrm((DEPTH, D_FF, D), DEEPNORM_BETA * D_FF ** -0.5),
    }


def reference(x_prompt, x_sample, meta_tokens, hy_w_in, hy_conv_w, hy_conv_b, hy_filt_w1, hy_filt_b1,
              hy_filt_w2, hy_filt_b2, hy_filt_w3, hy_filt_b3, hy_filt_freq, hy_skip, hy_w_out,
              at_w_qkv, at_q_gain, at_k_gain, at_w_out, ln1_g, ln1_b, ln2_g, ln2_b, ffn_w_in,
              ffn_conv_w, ffn_conv_b, ffn_w_out):
    weights = (meta_tokens, hy_w_in, hy_conv_w, hy_conv_b, hy_filt_w1, hy_filt_b1, hy_filt_w2,
               hy_filt_b2, hy_filt_w3, hy_filt_b3, hy_filt_freq, hy_skip, hy_w_out, at_w_qkv,
               at_q_gain, at_k_gain, at_w_out, ln1_g, ln1_b, ln2_g, ln2_b, ffn_w_in, ffn_conv_w,
               ffn_conv_b, ffn_w_out)
    y_prompt = run_trunk(x_prompt, *weights)
    y_sample = run_trunk(x_sample, *weights)
    return (y_prompt, y_sample)
```

```python
import functools
import math

import jax
import jax.numpy as jnp
from jax import lax
from jax.experimental import pallas as pl
from jax.experimental.pallas import tpu as pltpu

f32 = jnp.float32
bf16 = jnp.bfloat16

D_MODEL = 1024
DEPTH = 4
N_META = 16
GRID_W = 64
HEAD_DIM = 128
N_HEADS = 8
N_KV_HEADS = 2
GROUP = N_HEADS // N_KV_HEADS
ROPE_THETA = 10000.0
FILTER_EMB_DIM = 33
FILTER_EMB_PAD = 40
FILTER_BANDS = 16
FILTER_HIDDEN = 64
DECAY_TARGET = 1e-2
FAST_DECAY_PCT = 0.3
SLOW_DECAY_PCT = 1.5
DECAY_SHIFT = 0.05
FILTER_NORM_EPS = 1e-6
D_FF = 2816
DEEPNORM_ALPHA = (2 * DEPTH) ** 0.25
LN_EPS = 1e-5
RMS_EPS = 1e-6
NEG_BIG = -1e30

FFT_N2 = 128
V7X_VMEM_BYTES = 64 * 1024 * 1024
VMEM_LIMIT = 52 * 1024 * 1024


class Cfg:
    def __init__(self, n):
        self.L = n + N_META
        if n == 8192:
            self.Lp, self.TL, self.TM, self.N1, self.NIN = 8320, 640, 1280, 136, 80
        elif n == 2048:
            self.Lp, self.TL, self.TM, self.N1, self.NIN = 2304, 768, 1024, 40, 32
        else:
            up = lambda a, m: -(-a // m) * m
            self.TL = self.TM = 128
            self.Lp = up(self.L, 128)
            self.N1 = up(-(-(2 * self.L - 1) // FFT_N2), 8)
            self.NIN = up(-(-self.Lp // FFT_N2), 16)
        self.NR = self.NIN * FFT_N2
        self.N = self.N1 * FFT_N2
        assert self.N >= 2 * self.L - 1 and self.NR >= self.Lp and self.Lp % self.TL == 0


def _cparams(sem, vmem=None):
    return pltpu.CompilerParams(dimension_semantics=sem, vmem_limit_bytes=vmem)


def _mm_kernel(x_ref, w_ref, o_ref):
    o_ref[...] = jnp.dot(x_ref[...], w_ref[...], preferred_element_type=f32).astype(o_ref.dtype)


def matmul(x, w, out_dtype, tm, tn):
    R, K = x.shape
    N = w.shape[1]
    assert R % tm == 0 and N % tn == 0
    return pl.pallas_call(
        _mm_kernel,
        grid=(N // tn, R // tm),
        in_specs=[pl.BlockSpec((tm, K), lambda j, i: (i, 0)), pl.BlockSpec((K, tn), lambda j, i: (0, j))],
        out_specs=pl.BlockSpec((tm, tn), lambda j, i: (i, j)),
        out_shape=jax.ShapeDtypeStruct((R, N), out_dtype),
        compiler_params=_cparams(("arbitrary", "arbitrary"), VMEM_LIMIT),
        name="mm",
    )(x, w)


def _mm_ln_kernel(x_ref, w_ref, h_ref, g_ref, b_ref, o_ref, obf_ref):
    m = jnp.dot(x_ref[...], w_ref[...], preferred_element_type=f32)
    y = DEEPNORM_ALPHA * h_ref[...] + m
    mu = jnp.mean(y, axis=-1, keepdims=True)
    yc = y - mu
    var = jnp.mean(yc * yc, axis=-1, keepdims=True)
    out = yc * lax.rsqrt(var + LN_EPS) * g_ref[...] + b_ref[...]
    o_ref[...] = out
    obf_ref[...] = out.astype(bf16)


def matmul_residual_ln(x, w, h, g, b, tm):
    R, K = x.shape
    N = w.shape[1]
    row = lambda i: (i, 0)
    fixed = lambda i: (0, 0)
    return pl.pallas_call(
        _mm_ln_kernel,
        grid=(R // tm,),
        in_specs=[pl.BlockSpec((tm, K), row), pl.BlockSpec((K, N), fixed), pl.BlockSpec((tm, N), row),
                  pl.BlockSpec((1, N), fixed), pl.BlockSpec((1, N), fixed)],
        out_specs=[pl.BlockSpec((tm, N), row), pl.BlockSpec((tm, N), row)],
        out_shape=[jax.ShapeDtypeStruct((R, N), f32), jax.ShapeDtypeStruct((R, N), bf16)],
        compiler_params=_cparams(("arbitrary",), VMEM_LIMIT),
        name="mm_ln",
    )(x, w, h, g.reshape(1, N), b.reshape(1, N))


def _left_mm_kernel(a_ref, x_ref, o_ref, *, cw):
    a = a_ref[...]

    def body(c, carry):
        sl = pl.ds(pl.multiple_of(c * cw, cw), cw)
        o_ref[:, sl] = jnp.dot(a, x_ref[:, sl], preferred_element_type=f32).astype(o_ref.dtype)
        return carry

    lax.fori_loop(0, x_ref.shape[1] // cw, body, 0)


def left_matmul(a, x, out_dtype, tn=8192, cw=512):
    M, K = a.shape
    B, _, C = x.shape
    assert C % tn == 0 and tn % cw == 0
    return pl.pallas_call(
        functools.partial(_left_mm_kernel, cw=cw),
        grid=(B, C // tn),
        in_specs=[pl.BlockSpec((M, K), lambda b, j: (0, 0)), pl.BlockSpec((None, K, tn), lambda b, j: (b, 0, j))],
        out_specs=pl.BlockSpec((None, M, tn), lambda b, j: (b, 0, j)),
        out_shape=jax.ShapeDtypeStruct((B, M, C), out_dtype),
        compiler_params=_cparams(("arbitrary", "arbitrary"), VMEM_LIMIT),
        name="left_mm",
    )(a, x)


def _conv3_rows(ref, r, ch, Lp, L, w, b):
    c = ref[pl.ds(r, ch), :].astype(f32)
    rows = r + lax.broadcasted_iota(jnp.int32, (ch, 1), 0)
    if r == 0:
        p = jnp.where(rows >= 1, pltpu.roll(c, 1, axis=0), 0.0)
    else:
        p = ref[pl.ds(r - 1, ch), :].astype(f32)
    if r + ch == Lp:
        n = pltpu.roll(c, ch - 1, axis=0)
    else:
        n = ref[pl.ds(r + 1, ch), :].astype(f32)
    n = jnp.where(rows + 1 < L, n, 0.0)
    return p * w[0:1, :] + c * w[1:2, :] + n * w[2:3, :] + b, rows


def _hyena_gate_kernel(x1_ref, v_ref, w1_ref, b1_ref, wv_ref, bv_ref, o_ref, *, L, Lp, ch):
    w1, b1, wv, bv = w1_ref[...], b1_ref[...], wv_ref[...], bv_ref[...]
    for r in range(0, Lp, ch):
        cx, rows = _conv3_rows(x1_ref, r, ch, Lp, L, w1, b1)
        cv, _ = _conv3_rows(v_ref, r, ch, Lp, L, wv, bv)
        o_ref[pl.ds(r, ch), :] = jnp.where(rows < L, cx * cv, 0.0).astype(o_ref.dtype)
    NR = o_ref.shape[0]
    if NR > Lp:
        o_ref[pl.ds(Lp, NR - Lp), :] = jnp.zeros((NR - Lp, o_ref.shape[1]), o_ref.dtype)


def hyena_gate(u, conv_w, conv_b, cfg, tc=256):
    B = u.shape[0]
    D = D_MODEL
    nb = D // tc
    sec = lambda s: (lambda b, c: (b, 0, s * nb + c))
    wsec = lambda s: (lambda b, c: (0, s * nb + c))
    return pl.pallas_call(
        functools.partial(_hyena_gate_kernel, L=cfg.L, Lp=cfg.Lp, ch=cfg.TL),
        grid=(B, nb),
        in_specs=[pl.BlockSpec((None, cfg.Lp, tc), sec(1)), pl.BlockSpec((None, cfg.Lp, tc), sec(2)),
                  pl.BlockSpec((3, tc), wsec(1)), pl.BlockSpec((1, tc), wsec(1)),
                  pl.BlockSpec((3, tc), wsec(2)), pl.BlockSpec((1, tc), wsec(2))],
        out_specs=pl.BlockSpec((None, cfg.NR, tc), lambda b, c: (b, 0, c)),
        out_shape=jax.ShapeDtypeStruct((B, cfg.NR, D), bf16),
        compiler_params=_cparams(("arbitrary", "arbitrary"), VMEM_LIMIT),
        name="hyena_gate",
    )(u, u, conv_w, conv_b.reshape(1, -1), conv_w, conv_b.reshape(1, -1))


def _hyena_post_kernel(y_ref, vg_ref, x0_ref, w0_ref, b0_ref, skip_ref, o_ref, *, L, Lp, ch):
    w0, b0, skip = w0_ref[...], b0_ref[...], skip_ref[...]
    for r in range(0, Lp, ch):
        cx, _ = _conv3_rows(x0_ref, r, ch, Lp, L, w0, b0)
        y = y_ref[pl.ds(r, ch), :].astype(f32) + vg_ref[pl.ds(r, ch), :].astype(f32) * skip
        o_ref[pl.ds(r, ch), :] = (y * cx).astype(o_ref.dtype)


def hyena_post(yc, vg, u, conv_w, conv_b, skip, cfg, tc=256):
    B = u.shape[0]
    D = D_MODEL
    col = lambda b, c: (b, 0, c)
    return pl.pallas_call(
        functools.partial(_hyena_post_kernel, L=cfg.L, Lp=cfg.Lp, ch=cfg.TL),
        grid=(B, D // tc),
        in_specs=[pl.BlockSpec((None, cfg.Lp, tc), col), pl.BlockSpec((None, cfg.Lp, tc), col),
                  pl.BlockSpec((None, cfg.Lp, tc), col),
                  pl.BlockSpec((3, tc), lambda b, c: (0, c)), pl.BlockSpec((1, tc), lambda b, c: (0, c)),
                  pl.BlockSpec((1, tc), lambda b, c: (0, c))],
        out_specs=pl.BlockSpec((None, cfg.Lp, tc), col),
        out_shape=jax.ShapeDtypeStruct((B, cfg.Lp, D), bf16),
        compiler_params=_cparams(("arbitrary", "arbitrary"), VMEM_LIMIT),
        name="hyena_post",
    )(yc, vg, u, conv_w, conv_b.reshape(1, -1), skip.reshape(1, -1))


def _ffn_gate_kernel(g_ref, a_ref, w_ref, b_ref, o_ref, *, L, Lp, ch):
    w, b = w_ref[...], b_ref[...]
    for r in range(0, Lp, ch):
        g, _ = _conv3_rows(g_ref, r, ch, Lp, L, w, b)
        gelu = 0.5 * g * (1.0 + lax.erf(g * (2.0 ** -0.5)))
        o_ref[pl.ds(r, ch), :] = (gelu * a_ref[pl.ds(r, ch), :].astype(f32)).astype(o_ref.dtype)


def ffn_gate(hid, conv_w, conv_b, cfg, tc=256):
    B = hid.shape[0]
    nb = D_FF // tc
    return pl.pallas_call(
        functools.partial(_ffn_gate_kernel, L=cfg.L, Lp=cfg.Lp, ch=cfg.TL),
        grid=(B, nb),
        in_specs=[pl.BlockSpec((None, cfg.Lp, tc), lambda b, c: (b, 0, c)),
                  pl.BlockSpec((None, cfg.Lp, tc), lambda b, c: (b, 0, nb + c)),
                  pl.BlockSpec((3, tc), lambda b, c: (0, c)), pl.BlockSpec((1, tc), lambda b, c: (0, c))],
        out_specs=pl.BlockSpec((None, cfg.Lp, tc), lambda b, c: (b, 0, c)),
        out_shape=jax.ShapeDtypeStruct((B, cfg.Lp, D_FF), bf16),
        compiler_params=_cparams(("arbitrary", "arbitrary"), VMEM_LIMIT),
        name="ffn_gate",
    )(hid, hid, conv_w, conv_b.reshape(1, -1))


def _dft_tables(cfg):
    N1, NIN, N2, N = cfg.N1, cfg.NIN, FFT_N2, cfg.N
    k1 = jnp.arange(N1, dtype=jnp.int32)
    n1 = jnp.arange(NIN, dtype=jnp.int32)
    ang = (2.0 * math.pi / N1) * ((k1[:, None] * n1[None, :]) % N1).astype(f32)
    ca, sa = jnp.cos(ang), jnp.sin(ang)
    fa = jnp.concatenate([ca, -sa], axis=0)
    fa_inv = jnp.concatenate([ca.T, -sa.T], axis=1) * (1.0 / N)
    k2 = jnp.arange(N2, dtype=jnp.int32)[:, None]
    n2 = jnp.arange(N2, dtype=jnp.int32)[None, :]
    ph = ((n2 * k2 * N1)[None] + n2[None] * k1[:, None, None]) % N
    ang = (2.0 * math.pi / N) * ph.astype(f32)
    ar, ai = jnp.cos(ang), -jnp.sin(ang)
    g_fwd = jnp.concatenate([jnp.concatenate([ar, -ai], axis=2), jnp.concatenate([ai, ar], axis=2)], axis=1)
    art, ait = jnp.swapaxes(ar, 1, 2), -jnp.swapaxes(ai, 1, 2)
    g_inv = jnp.concatenate([jnp.concatenate([art, -ait], axis=2), jnp.concatenate([ait, art], axis=2)], axis=1)
    return fa.astype(bf16), fa_inv.astype(bf16), g_fwd.astype(bf16), g_inv.astype(bf16)


def _fft_b_fused_kernel(z_ref, h_ref, gf_ref, gi_ref, o_ref):
    zc = z_ref[...].reshape(2 * FFT_N2, z_ref.shape[-1])
    y = jnp.dot(gf_ref[...], zc, preferred_element_type=f32)
    yr, yi = y[:FFT_N2], y[FFT_N2:]
    hr, hi = h_ref[0], h_ref[1]
    pc = jnp.concatenate([yr * hr - yi * hi, yr * hi + yi * hr], axis=0).astype(bf16)
    q = jnp.dot(gi_ref[...], pc, preferred_element_type=f32)
    o_ref[0] = q[:FFT_N2].astype(o_ref.dtype)
    o_ref[1] = q[FFT_N2:].astype(o_ref.dtype)


def fft_b_fused(z5, hspec, g_fwd, g_inv):
    B, _, N1, N2, D = z5.shape
    zspec = pl.BlockSpec((None, 2, None, N2, D), lambda k, b: (b, 0, k, 0, 0))
    gspec = pl.BlockSpec((None, 2 * N2, 2 * N2), lambda k, b: (k, 0, 0))
    return pl.pallas_call(
        _fft_b_fused_kernel,
        grid=(N1, B),
        in_specs=[zspec, pl.BlockSpec((2, None, N2, D), lambda k, b: (0, k, 0, 0)), gspec, gspec],
        out_specs=zspec,
        out_shape=jax.ShapeDtypeStruct(z5.shape, bf16),
        compiler_params=_cparams(("arbitrary", "arbitrary"), VMEM_LIMIT),
        name="fft_b_fused",
    )(z5, hspec, g_fwd, g_inv)


def _fft_b_filter_kernel(z_ref, csum_ref, gf_ref, o_ref):
    D = D_MODEL
    zc = z_ref[...].reshape(2 * FFT_N2, 2 * D)
    y = jnp.dot(gf_ref[...], zc, preferred_element_type=f32)
    scale = 1.0 / (csum_ref[...] + FILTER_NORM_EPS)
    y = y * scale
    yr, yi = y[:FFT_N2], y[FFT_N2:]
    o_ref[0] = yr[:, :D] + yr[:, D:]
    o_ref[1] = yi[:, :D] - yi[:, D:]


def fft_b_filter(z5, csum, g_fwd):
    _, N1, N2, D2 = z5.shape
    return pl.pallas_call(
        _fft_b_filter_kernel,
        grid=(N1,),
        in_specs=[pl.BlockSpec((2, None, N2, D2), lambda k: (0, k, 0, 0)), pl.BlockSpec((1, D2), lambda k: (0, 0)),
                  pl.BlockSpec((None, 2 * N2, 2 * N2), lambda k: (k, 0, 0))],
        out_specs=pl.BlockSpec((2, None, N2, D2 // 2), lambda k: (0, k, 0, 0)),
        out_shape=jax.ShapeDtypeStruct((2, N1, N2, D2 // 2), f32),
        compiler_params=_cparams(("arbitrary",), VMEM_LIMIT),
        name="fft_b_filter",
    )(z5, csum, g_fwd)


def _filter_mlp_kernel(z_ref, w1_ref, b1_ref, w2_ref, b2_ref, w3_ref, b3_ref, fr_ref, dl_ref, o_ref, cs_ref, *, L, tf):
    i = pl.program_id(0)
    hp = lax.Precision.HIGHEST
    z = z_ref[...]
    fr = fr_ref[...]
    h = jnp.sin(fr * (jnp.dot(z, w1_ref[...], preferred_element_type=f32, precision=hp) + b1_ref[...]))
    h = jnp.sin(fr * (jnp.dot(h, w2_ref[...], preferred_element_type=f32, precision=hp) + b2_ref[...]))
    h = jnp.dot(h, w3_ref[...], preferred_element_type=f32, precision=hp) + b3_ref[...]
    win = jnp.exp(-z[:, 0:1] * dl_ref[...]) + DECAY_SHIFT
    h = h * jnp.concatenate([win, win], axis=1)
    rows = i * tf + lax.broadcasted_iota(jnp.int32, (tf, 1), 0)
    h = jnp.where(rows < L, h, 0.0)

    @pl.when(i == 0)
    def _():
        cs_ref[...] = jnp.zeros_like(cs_ref)

    cs_ref[...] += jnp.sum(jnp.abs(h), axis=0, keepdims=True)
    lanes = lax.broadcasted_iota(jnp.int32, (1, h.shape[1]), 1)
    o_ref[...] = jnp.where((rows == 0) & (lanes >= D_MODEL), 0.0, h).astype(o_ref.dtype)


def filter_mlp(z, w1, b1, w2, b2, w3, b3, freq, deltas, cfg, tf=512):
    NR = cfg.NR
    D2 = 2 * D_MODEL
    fixed = lambda i: (0, 0)
    full = lambda a: pl.BlockSpec(a.shape, fixed)
    args = (w1, b1.reshape(1, -1), w2, b2.reshape(1, -1), w3, b3.reshape(1, -1), freq.reshape(1, -1), deltas.reshape(1, -1))
    return pl.pallas_call(
        functools.partial(_filter_mlp_kernel, L=cfg.L, tf=tf),
        grid=(NR // tf,),
        in_specs=[pl.BlockSpec((tf, FILTER_EMB_PAD), lambda i: (i, 0))] + [full(a) for a in args],
        out_specs=[pl.BlockSpec((tf, D2), lambda i: (i, 0)), pl.BlockSpec((1, D2), fixed)],
        out_shape=[jax.ShapeDtypeStruct((NR, D2), bf16), jax.ShapeDtypeStruct((1, D2), f32)],
        compiler_params=_cparams(("arbitrary",), VMEM_LIMIT),
        name="filter_mlp",
    )(z, *args)


def _filter_features(cfg):
    L = cfg.L
    pos = jnp.arange(cfg.NR, dtype=f32)
    t = (pos / (L - 1))[:, None]
    w = (2.0 * math.pi / L) * pos[:, None]
    bands = jnp.linspace(1e-4, FILTER_BANDS - 1, FILTER_BANDS, dtype=f32)
    z = jnp.concatenate([t, jnp.cos(w * bands), -jnp.sin(w * bands)], axis=-1)
    return jnp.pad(z, ((0, 0), (0, FILTER_EMB_PAD - FILTER_EMB_DIM)))


def hyena_filter_spectrum(cfg, tables, w1, b1, w2, b2, w3, b3, freq):
    fa, _, g_fwd, _ = tables
    z = _filter_features(cfg)
    w1p = jnp.pad(w1, ((0, FILTER_EMB_PAD - FILTER_EMB_DIM), (0, 0)))
    max_decay = math.log(1.0 / DECAY_TARGET) / FAST_DECAY_PCT
    min_decay = math.log(1.0 / DECAY_TARGET) / SLOW_DECAY_PCT
    deltas = jnp.linspace(min_decay, max_decay, D_MODEL, dtype=f32)
    xf, csum = filter_mlp(z, w1p, b1, w2, b2, w3, b3, freq, deltas, cfg)
    D2 = 2 * D_MODEL
    z2 = left_matmul(fa, xf.reshape(1, cfg.NIN, FFT_N2 * D2), bf16)
    z5 = z2.reshape(2, cfg.N1, FFT_N2, D2)
    return fft_b_filter(z5, csum, g_fwd)


def long_conv(vg, hspec, tables, cfg):
    fa, fa_inv, g_fwd, g_inv = tables
    B = vg.shape[0]
    D = D_MODEL
    z2 = left_matmul(fa, vg.reshape(B, cfg.NIN, FFT_N2 * D), bf16)
    q5 = fft_b_fused(z2.reshape(B, 2, cfg.N1, FFT_N2, D), hspec, g_fwd, g_inv)
    y2 = left_matmul(fa_inv, q5.reshape(B, 2 * cfg.N1, FFT_N2 * D), bf16)
    return y2.reshape(B, cfg.NR, D)


def _qkv_prep_kernel(x_ref, cc_ref, ss_ref, qg_ref, kg_ref, q_ref, k_ref, v_ref, *, L, tl):
    rows = pl.program_id(1) * tl + lax.broadcasted_iota(jnp.int32, (tl, 1), 0)
    valid = rows < L
    cc, ss = cc_ref[...], ss_ref[...]

    def norm_rope(x, gain, scale):
        x = x.astype(f32)
        xn = x * lax.rsqrt(jnp.mean(x * x, axis=-1, keepdims=True) + RMS_EPS) * gain
        y = xn * cc + pltpu.roll(xn, HEAD_DIM // 2, axis=1) * ss
        return jnp.where(valid, y * scale, 0.0).astype(bf16)

    nq = N_HEADS * HEAD_DIM
    for hh in range(N_HEADS):
        sl = slice(hh * HEAD_DIM, (hh + 1) * HEAD_DIM)
        q_ref[:, sl] = norm_rope(x_ref[:, sl], qg_ref[...], HEAD_DIM ** -0.5)
    for hh in range(N_KV_HEADS):
        sl = slice(hh * HEAD_DIM, (hh + 1) * HEAD_DIM)
        k_ref[:, sl] = norm_rope(x_ref[:, nq + hh * HEAD_DIM: nq + (hh + 1) * HEAD_DIM], kg_ref[...], 1.0)
    nk = N_KV_HEADS * HEAD_DIM
    v_ref[...] = jnp.where(valid, x_ref[:, nq + nk:], jnp.zeros((), x_ref.dtype))


def qkv_prep(qkv, cc, ss, q_gain, k_gain, cfg):
    B = qkv.shape[0]
    tl = cfg.TL
    nq, nk = N_HEADS * HEAD_DIM, N_KV_HEADS * HEAD_DIM
    row = lambda b, i: (b, i, 0)
    tab = lambda b, i: (i, 0)
    fixed = lambda b, i: (0, 0)
    return pl.pallas_call(
        functools.partial(_qkv_prep_kernel, L=cfg.L, tl=tl),
        grid=(B, cfg.Lp // tl),
        in_specs=[pl.BlockSpec((None, tl, nq + 2 * nk), row), pl.BlockSpec((tl, HEAD_DIM), tab),
                  pl.BlockSpec((tl, HEAD_DIM), tab), pl.BlockSpec((1, HEAD_DIM), fixed), pl.BlockSpec((1, HEAD_DIM), fixed)],
        out_specs=[pl.BlockSpec((None, tl, nq), row), pl.BlockSpec((None, tl, nk), row), pl.BlockSpec((None, tl, nk), row)],
        out_shape=[jax.ShapeDtypeStruct((B, cfg.Lp, nq), bf16), jax.ShapeDtypeStruct((B, cfg.Lp, nk), bf16),
                   jax.ShapeDtypeStruct((B, cfg.Lp, nk), bf16)],
        compiler_params=_cparams(("arbitrary", "arbitrary"), VMEM_LIMIT),
        name="qkv_prep",
    )(qkv, cc, ss, q_gain.reshape(1, -1), k_gain.reshape(1, -1))


def _flash_kernel(q_ref, k_ref, v_ref, o_ref, m_scr, l_scr, acc_scr, *, L, tk):
    kj = pl.program_id(3)

    @pl.when(kj == 0)
    def _():
        m_scr[...] = jnp.full_like(m_scr, NEG_BIG)
        l_scr[...] = jnp.zeros_like(l_scr)
        acc_scr[...] = jnp.zeros_like(acc_scr)

    k = k_ref[...]
    v = v_ref[...]
    cols = kj * tk + lax.broadcasted_iota(jnp.int32, (1, tk), 1)
    kmask = cols < L
    for g in range(GROUP):
        q = q_ref[:, g * HEAD_DIM:(g + 1) * HEAD_DIM]
        s = lax.dot_general(q, k, (((1,), (1,)), ((), ())), preferred_element_type=f32)
        s = jnp.where(kmask, s, NEG_BIG)
        m_old = m_scr[g]
        m_new = jnp.maximum(m_old, jnp.max(s, axis=-1, keepdims=True))
        p = jnp.exp(s - m_new)
        alpha = jnp.exp(m_old - m_new)
        l_scr[g] = alpha * l_scr[g] + jnp.sum(p, axis=-1, keepdims=True)
        acc_scr[g] = alpha * acc_scr[g] + jnp.dot(p.astype(bf16), v, preferred_element_type=f32)
        m_scr[g] = m_new

    @pl.when(kj == pl.num_programs(3) - 1)
    def _():
        for g in range(GROUP):
            o_ref[:, g * HEAD_DIM:(g + 1) * HEAD_DIM] = (acc_scr[g] / l_scr[g]).astype(o_ref.dtype)


def flash_attention(q, k, v, cfg):
    B = q.shape[0]
    tq = tk = cfg.TL
    gw = GROUP * HEAD_DIM
    return pl.pallas_call(
        functools.partial(_flash_kernel, L=cfg.L, tk=tk),
        grid=(B, N_KV_HEADS, cfg.Lp // tq, cfg.Lp // tk),
        in_specs=[pl.BlockSpec((None, tq, gw), lambda b, h, i, j: (b, i, h)),
                  pl.BlockSpec((None, tk, HEAD_DIM), lambda b, h, i, j: (b, j, h)),
                  pl.BlockSpec((None, tk, HEAD_DIM), lambda b, h, i, j: (b, j, h))],
        out_specs=pl.BlockSpec((None, tq, gw), lambda b, h, i, j: (b, i, h)),
        out_shape=jax.ShapeDtypeStruct(q.shape, bf16),
        scratch_shapes=[pltpu.VMEM((GROUP, tq, 1), f32), pltpu.VMEM((GROUP, tq, 1), f32),
                        pltpu.VMEM((GROUP, tq, HEAD_DIM), f32)],
        compiler_params=_cparams(("arbitrary", "arbitrary", "arbitrary", "arbitrary"), VMEM_LIMIT),
        name="flash",
    )(q, k, v)


def _rope_tables(cfg):
    n = cfg.L - N_META
    rows = n // GRID_W
    row = jnp.concatenate([jnp.full((N_META,), -1.0, f32), jnp.repeat(jnp.arange(rows, dtype=f32), GRID_W)])
    col = jnp.concatenate([jnp.arange(N_META, dtype=f32), jnp.tile(jnp.arange(GRID_W, dtype=f32), rows)])
    axis_rot = HEAD_DIM // 2
    inv_freq = ROPE_THETA ** (-jnp.arange(0, axis_rot, 2, dtype=f32) / axis_rot)
    ang = jnp.concatenate([row[:, None] * inv_freq, col[:, None] * inv_freq], axis=-1)
    ang = jnp.pad(ang, ((0, cfg.Lp - cfg.L), (0, 0)))
    c, s = jnp.cos(ang), jnp.sin(ang)
    return jnp.concatenate([c, c], axis=-1), jnp.concatenate([-s, s], axis=-1)


def _run_trunk(x, cfg, meta_tokens, hy, at, ln, ffn):
    B, n, D = x.shape
    Lp, tm = cfg.Lp, cfg.TM
    R = B * Lp
    meta = jnp.broadcast_to(meta_tokens[None], (B, N_META, D))
    h = jnp.concatenate([meta, x, jnp.zeros((B, Lp - cfg.L, D), x.dtype)], axis=1).reshape(R, D)
    hb = h.astype(bf16)
    cc, ss = _rope_tables(cfg)
    tables = _dft_tables(cfg)
    for i in range(DEPTH):
        j = i // 2
        if i % 2 == 0:
            hspec = hyena_filter_spectrum(cfg, tables, hy["f_w1"][j], hy["f_b1"][j], hy["f_w2"][j], hy["f_b2"][j],
                                          hy["f_w3"][j], hy["f_b3"][j], hy["f_freq"][j])
            u = matmul(hb, hy["w_in"][j], bf16, tm, 1024).reshape(B, Lp, 3 * D)
            vg = hyena_gate(u, hy["conv_w"][j], hy["conv_b"][j], cfg)
            yc = long_conv(vg, hspec, tables, cfg)
            y = hyena_post(yc, vg, u, hy["conv_w"][j], hy["conv_b"][j], hy["skip"][j], cfg)
            h, hb = matmul_residual_ln(y.reshape(R, D), hy["w_out"][j], h, ln["g1"][i], ln["b1"][i], cfg.TL)
        else:
            qkv = matmul(hb, at["w_qkv"][j], bf16, tm, 512).reshape(B, Lp, -1)
            q, k, v = qkv_prep(qkv, cc, ss, at["q_gain"][j], at["k_gain"][j], cfg)
            o = flash_attention(q, k, v, cfg)
            h, hb = matmul_residual_ln(o.reshape(R, D), at["w_out"][j], h, ln["g1"][i], ln["b1"][i], cfg.TL)
        hid = matmul(hb, ffn["w_in"][i], bf16, tm, 512).reshape(B, Lp, 2 * D_FF)
        gated = ffn_gate(hid, ffn["conv_w"][i], ffn["conv_b"][i], cfg)
        h, hb = matmul_residual_ln(gated.reshape(R, D_FF), ffn["w_out"][i], h, ln["g2"][i], ln["b2"][i], cfg.TL)
    return h.reshape(B, Lp, D)[:, N_META:cfg.L]


def kernel(x_prompt, x_sample, meta_tokens, hy_w_in, hy_conv_w, hy_conv_b, hy_filt_w1, hy_filt_b1, hy_filt_w2, hy_filt_b2, hy_filt_w3, hy_filt_b3, hy_filt_freq, hy_skip, hy_w_out, at_w_qkv, at_q_gain, at_k_gain, at_w_out, ln1_g, ln1_b, ln2_g, ln2_b, ffn_w_in, ffn_conv_w, ffn_conv_b, ffn_w_out):
    hy = dict(w_in=hy_w_in.astype(bf16), conv_w=hy_conv_w, conv_b=hy_conv_b, f_w1=hy_filt_w1, f_b1=hy_filt_b1,
              f_w2=hy_filt_w2, f_b2=hy_filt_b2, f_w3=hy_filt_w3, f_b3=hy_filt_b3, f_freq=hy_filt_freq, skip=hy_skip,
              w_out=hy_w_out.astype(bf16))
    at = dict(w_qkv=at_w_qkv.astype(bf16), q_gain=at_q_gain, k_gain=at_k_gain, w_out=at_w_out.astype(bf16))
    ln = dict(g1=ln1_g, b1=ln1_b, g2=ln2_g, b2=ln2_b)
    ffn = dict(w_in=ffn_w_in.astype(bf16), conv_w=ffn_conv_w, conv_b=ffn_conv_b, w_out=ffn_w_out.astype(bf16))
    y_prompt = _run_trunk(x_prompt, Cfg(x_prompt.shape[1]), meta_tokens, hy, at, ln, ffn)
    y_sample = _run_trunk(x_sample, Cfg(x_sample.shape[1]), meta_tokens, hy, at, ln, ffn)
    return (y_prompt, y_sample)
```

```python
import functools
import math

import jax
import jax.numpy as jnp
from jax import lax
from jax.experimental import pallas as pl
from jax.experimental.pallas import tpu as pltpu

f32 = jnp.float32
bf16 = jnp.bfloat16

D_MODEL = 1024
DEPTH = 4
N_META = 16
GRID_W = 64
HEAD_DIM = 128
N_HEADS = 8
N_KV_HEADS = 2
GROUP = N_HEADS // N_KV_HEADS
ROPE_THETA = 10000.0
FILTER_EMB_DIM = 33
FILTER_EMB_PAD = 40
FILTER_BANDS = 16
FILTER_HIDDEN = 64
DECAY_TARGET = 1e-2
FAST_DECAY_PCT = 0.3
SLOW_DECAY_PCT = 1.5
DECAY_SHIFT = 0.05
FILTER_NORM_EPS = 1e-6
D_FF = 2816
DEEPNORM_ALPHA = (2 * DEPTH) ** 0.25
LN_EPS = 1e-5
RMS_EPS = 1e-6
NEG_BIG = -1e30
LOG2E = 1.4426950408889634

FFT_N2 = 128
V7X_VMEM_BYTES = 64 * 1024 * 1024
VMEM_LIMIT = 52 * 1024 * 1024


class Cfg:
    def __init__(self, n):
        self.L = n + N_META
        if n == 8192:
            self.Lp, self.TL, self.TM, self.TQ, self.N1, self.NIN = 8320, 640, 1280, 128, 136, 80
        elif n == 2048:
            self.Lp, self.TL, self.TM, self.TQ, self.N1, self.NIN = 2304, 768, 1024, 256, 40, 32
        else:
            up = lambda a, m: -(-a // m) * m
            self.TL = self.TM = self.TQ = 128
            self.Lp = up(self.L, 128)
            self.N1 = up(-(-(2 * self.L - 1) // FFT_N2), 8)
            self.NIN = up(-(-self.Lp // FFT_N2), 16)
        self.NR = self.NIN * FFT_N2
        self.N1H = -(-(self.N1 // 2 + 1) // 8) * 8
        self.N = self.N1 * FFT_N2
        assert self.N >= 2 * self.L - 1 and self.NR >= self.Lp and self.Lp % self.TL == 0


def _cparams(sem, vmem=None):
    return pltpu.CompilerParams(dimension_semantics=sem, vmem_limit_bytes=vmem)


def _mm_kernel(x_ref, w_ref, o_ref):
    o_ref[...] = jnp.dot(x_ref[...], w_ref[...], preferred_element_type=f32).astype(o_ref.dtype)


def matmul(x, w, out_dtype, tm, tn):
    R, K = x.shape
    N = w.shape[1]
    assert R % tm == 0 and N % tn == 0
    return pl.pallas_call(
        _mm_kernel,
        grid=(N // tn, R // tm),
        in_specs=[pl.BlockSpec((tm, K), lambda j, i: (i, 0)), pl.BlockSpec((K, tn), lambda j, i: (0, j))],
        out_specs=pl.BlockSpec((tm, tn), lambda j, i: (i, j)),
        out_shape=jax.ShapeDtypeStruct((R, N), out_dtype),
        compiler_params=_cparams(("arbitrary", "arbitrary"), VMEM_LIMIT),
        name="mm",
    )(x, w)


def _mm_ln_kernel(x_ref, w_ref, h_ref, g_ref, b_ref, o_ref, obf_ref):
    m = jnp.dot(x_ref[...], w_ref[...], preferred_element_type=f32)
    y = DEEPNORM_ALPHA * h_ref[...] + m
    mu = jnp.mean(y, axis=-1, keepdims=True)
    yc = y - mu
    var = jnp.mean(yc * yc, axis=-1, keepdims=True)
    out = yc * lax.rsqrt(var + LN_EPS) * g_ref[...] + b_ref[...]
    o_ref[...] = out
    obf_ref[...] = out.astype(bf16)


def matmul_residual_ln(x, w, h, g, b, tm):
    R, K = x.shape
    N = w.shape[1]
    row = lambda i: (i, 0)
    fixed = lambda i: (0, 0)
    return pl.pallas_call(
        _mm_ln_kernel,
        grid=(R // tm,),
        in_specs=[pl.BlockSpec((tm, K), row), pl.BlockSpec((K, N), fixed), pl.BlockSpec((tm, N), row),
                  pl.BlockSpec((1, N), fixed), pl.BlockSpec((1, N), fixed)],
        out_specs=[pl.BlockSpec((tm, N), row), pl.BlockSpec((tm, N), row)],
        out_shape=[jax.ShapeDtypeStruct((R, N), f32), jax.ShapeDtypeStruct((R, N), bf16)],
        compiler_params=_cparams(("arbitrary",), VMEM_LIMIT),
        name="mm_ln",
    )(x, w, h, g.reshape(1, N), b.reshape(1, N))


def _left_mm_kernel(a_ref, x_ref, o_ref, *, cw):
    a = a_ref[...]

    def body(c, carry):
        sl = pl.ds(pl.multiple_of(c * cw, cw), cw)
        o_ref[:, sl] = jnp.dot(a, x_ref[:, sl], preferred_element_type=f32).astype(o_ref.dtype)
        return carry

    lax.fori_loop(0, x_ref.shape[1] // cw, body, 0)


def left_matmul(a, x, out_dtype, tn=8192, cw=512):
    M, K = a.shape
    B, _, C = x.shape
    assert C % tn == 0 and tn % cw == 0
    return pl.pallas_call(
        functools.partial(_left_mm_kernel, cw=cw),
        grid=(B, C // tn),
        in_specs=[pl.BlockSpec((M, K), lambda b, j: (0, 0)), pl.BlockSpec((None, K, tn), lambda b, j: (b, 0, j))],
        out_specs=pl.BlockSpec((None, M, tn), lambda b, j: (b, 0, j)),
        out_shape=jax.ShapeDtypeStruct((B, M, C), out_dtype),
        compiler_params=_cparams(("arbitrary", "arbitrary"), VMEM_LIMIT),
        name="left_mm",
    )(a, x)


def _conv3_rows(ref, r, ch, Lp, L, w, b):
    c = ref[pl.ds(r, ch), :].astype(f32)
    rows = r + lax.broadcasted_iota(jnp.int32, (ch, 1), 0)
    if r == 0:
        p = jnp.where(rows >= 1, pltpu.roll(c, 1, axis=0), 0.0)
    else:
        p = ref[pl.ds(r - 1, ch), :].astype(f32)
    if r + ch == Lp:
        n = pltpu.roll(c, ch - 1, axis=0)
    else:
        n = ref[pl.ds(r + 1, ch), :].astype(f32)
    n = jnp.where(rows + 1 < L, n, 0.0)
    return p * w[0:1, :] + c * w[1:2, :] + n * w[2:3, :] + b, rows


def _hyena_gate_kernel(x1_ref, v_ref, w1_ref, b1_ref, wv_ref, bv_ref, o_ref, *, L, Lp, ch):
    w1, b1, wv, bv = w1_ref[...], b1_ref[...], wv_ref[...], bv_ref[...]
    for r in range(0, Lp, ch):
        cx, rows = _conv3_rows(x1_ref, r, ch, Lp, L, w1, b1)
        cv, _ = _conv3_rows(v_ref, r, ch, Lp, L, wv, bv)
        o_ref[pl.ds(r, ch), :] = jnp.where(rows < L, cx * cv, 0.0).astype(o_ref.dtype)
    NR = o_ref.shape[0]
    if NR > Lp:
        o_ref[pl.ds(Lp, NR - Lp), :] = jnp.zeros((NR - Lp, o_ref.shape[1]), o_ref.dtype)


def hyena_gate(u, conv_w, conv_b, cfg, tc=256):
    B = u.shape[0]
    D = D_MODEL
    nb = D // tc
    sec = lambda s: (lambda b, c: (b, 0, s * nb + c))
    wsec = lambda s: (lambda b, c: (0, s * nb + c))
    return pl.pallas_call(
        functools.partial(_hyena_gate_kernel, L=cfg.L, Lp=cfg.Lp, ch=cfg.TL),
        grid=(B, nb),
        in_specs=[pl.BlockSpec((None, cfg.Lp, tc), sec(1)), pl.BlockSpec((None, cfg.Lp, tc), sec(2)),
                  pl.BlockSpec((3, tc), wsec(1)), pl.BlockSpec((1, tc), wsec(1)),
                  pl.BlockSpec((3, tc), wsec(2)), pl.BlockSpec((1, tc), wsec(2))],
        out_specs=pl.BlockSpec((None, cfg.NR, tc), lambda b, c: (b, 0, c)),
        out_shape=jax.ShapeDtypeStruct((B, cfg.NR, D), bf16),
        compiler_params=_cparams(("arbitrary", "arbitrary"), VMEM_LIMIT),
        name="hyena_gate",
    )(u, u, conv_w, conv_b.reshape(1, -1), conv_w, conv_b.reshape(1, -1))


def _hyena_post_kernel(y_ref, vg_ref, x0_ref, w0_ref, b0_ref, skip_ref, o_ref, *, L, Lp, ch):
    w0, b0, skip = w0_ref[...], b0_ref[...], skip_ref[...]
    for r in range(0, Lp, ch):
        cx, _ = _conv3_rows(x0_ref, r, ch, Lp, L, w0, b0)
        y = y_ref[pl.ds(r, ch), :].astype(f32) + vg_ref[pl.ds(r, ch), :].astype(f32) * skip
        o_ref[pl.ds(r, ch), :] = (y * cx).astype(o_ref.dtype)


def hyena_post(yc, vg, u, conv_w, conv_b, skip, cfg, tc=256):
    B = u.shape[0]
    D = D_MODEL
    col = lambda b, c: (b, 0, c)
    return pl.pallas_call(
        functools.partial(_hyena_post_kernel, L=cfg.L, Lp=cfg.Lp, ch=cfg.TL),
        grid=(B, D // tc),
        in_specs=[pl.BlockSpec((None, cfg.Lp, tc), col), pl.BlockSpec((None, cfg.Lp, tc), col),
                  pl.BlockSpec((None, cfg.Lp, tc), col),
                  pl.BlockSpec((3, tc), lambda b, c: (0, c)), pl.BlockSpec((1, tc), lambda b, c: (0, c)),
                  pl.BlockSpec((1, tc), lambda b, c: (0, c))],
        out_specs=pl.BlockSpec((None, cfg.Lp, tc), col),
        out_shape=jax.ShapeDtypeStruct((B, cfg.Lp, D), bf16),
        compiler_params=_cparams(("arbitrary", "arbitrary"), VMEM_LIMIT),
        name="hyena_post",
    )(yc, vg, u, conv_w, conv_b.reshape(1, -1), skip.reshape(1, -1))


def _ffn_gate_kernel(g_ref, a_ref, w_ref, b_ref, o_ref, *, L, Lp, ch):
    w, b = w_ref[...], b_ref[...]
    for r in range(0, Lp, ch):
        g, _ = _conv3_rows(g_ref, r, ch, Lp, L, w, b)
        gelu = 0.5 * g * (1.0 + lax.erf(g * (2.0 ** -0.5)))
        o_ref[pl.ds(r, ch), :] = (gelu * a_ref[pl.ds(r, ch), :].astype(f32)).astype(o_ref.dtype)


def ffn_gate(hid, conv_w, conv_b, cfg, tc=256):
    B = hid.shape[0]
    nb = D_FF // tc
    return pl.pallas_call(
        functools.partial(_ffn_gate_kernel, L=cfg.L, Lp=cfg.Lp, ch=cfg.TL),
        grid=(B, nb),
        in_specs=[pl.BlockSpec((None, cfg.Lp, tc), lambda b, c: (b, 0, c)),
                  pl.BlockSpec((None, cfg.Lp, tc), lambda b, c: (b, 0, nb + c)),
                  pl.BlockSpec((3, tc), lambda b, c: (0, c)), pl.BlockSpec((1, tc), lambda b, c: (0, c))],
        out_specs=pl.BlockSpec((None, cfg.Lp, tc), lambda b, c: (b, 0, c)),
        out_shape=jax.ShapeDtypeStruct((B, cfg.Lp, D_FF), bf16),
        compiler_params=_cparams(("arbitrary", "arbitrary"), VMEM_LIMIT),
        name="ffn_gate",
    )(hid, hid, conv_w, conv_b.reshape(1, -1))


def _dft_tables(cfg):
    N1, NIN, N2, N = cfg.N1, cfg.NIN, FFT_N2, cfg.N
    assert N1 % 2 == 0
    k1 = jnp.arange(cfg.N1H, dtype=jnp.int32)
    wk = jnp.where((k1 == 0) | (k1 == N1 // 2), 1.0, jnp.where(k1 < N1 // 2, 2.0, 0.0)).astype(f32)
    n1 = jnp.arange(NIN, dtype=jnp.int32)
    ang = (2.0 * math.pi / N1) * ((k1[:, None] * n1[None, :]) % N1).astype(f32)
    ca, sa = jnp.cos(ang), jnp.sin(ang)
    fa = jnp.concatenate([ca, -sa], axis=0)
    fa_inv = jnp.concatenate([ca.T * wk, -sa.T * wk], axis=1) * (1.0 / N)
    k2 = jnp.arange(N2, dtype=jnp.int32)[:, None]
    n2 = jnp.arange(N2, dtype=jnp.int32)[None, :]
    ph = ((n2 * k2 * N1)[None] + n2[None] * k1[:, None, None]) % N
    ang = (2.0 * math.pi / N) * ph.astype(f32)
    ar, ai = jnp.cos(ang), -jnp.sin(ang)
    g_fwd = jnp.concatenate([jnp.concatenate([ar, -ai], axis=2), jnp.concatenate([ai, ar], axis=2)], axis=1)
    art, ait = jnp.swapaxes(ar, 1, 2), -jnp.swapaxes(ai, 1, 2)
    g_inv = jnp.concatenate([jnp.concatenate([art, -ait], axis=2), jnp.concatenate([ait, art], axis=2)], axis=1)
    return fa.astype(bf16), fa_inv.astype(bf16), g_fwd.astype(bf16), g_inv.astype(bf16)


def _fft_b_fused_kernel(z_ref, h_ref, gf_ref, gi_ref, o_ref):
    zc = z_ref[...].reshape(2 * FFT_N2, z_ref.shape[-1])
    y = jnp.dot(gf_ref[...], zc, preferred_element_type=f32)
    yr, yi = y[:FFT_N2], y[FFT_N2:]
    hr, hi = h_ref[0], h_ref[1]
    pc = jnp.concatenate([yr * hr - yi * hi, yr * hi + yi * hr], axis=0).astype(bf16)
    q = jnp.dot(gi_ref[...], pc, preferred_element_type=f32)
    o_ref[0] = q[:FFT_N2].astype(o_ref.dtype)
    o_ref[1] = q[FFT_N2:].astype(o_ref.dtype)


def fft_b_fused(z5, hspec, g_fwd, g_inv):
    B, _, N1, N2, D = z5.shape
    zspec = pl.BlockSpec((None, 2, None, N2, D), lambda k, b: (b, 0, k, 0, 0))
    gspec = pl.BlockSpec((None, 2 * N2, 2 * N2), lambda k, b: (k, 0, 0))
    return pl.pallas_call(
        _fft_b_fused_kernel,
        grid=(N1, B),
        in_specs=[zspec, pl.BlockSpec((2, None, N2, D), lambda k, b: (0, k, 0, 0)), gspec, gspec],
        out_specs=zspec,
        out_shape=jax.ShapeDtypeStruct(z5.shape, bf16),
        compiler_params=_cparams(("arbitrary", "arbitrary"), VMEM_LIMIT),
        name="fft_b_fused",
    )(z5, hspec, g_fwd, g_inv)


def _fft_b_filter_kernel(z_ref, csum_ref, gf_ref, o_ref):
    D = D_MODEL
    zc = z_ref[...].reshape(2 * FFT_N2, 2 * D)
    y = jnp.dot(gf_ref[...], zc, preferred_element_type=f32)
    scale = 1.0 / (csum_ref[...] + FILTER_NORM_EPS)
    y = y * scale
    yr, yi = y[:FFT_N2], y[FFT_N2:]
    o_ref[0] = yr[:, :D] + yr[:, D:]
    o_ref[1] = yi[:, :D] - yi[:, D:]


def fft_b_filter(z5, csum, g_fwd):
    _, N1, N2, D2 = z5.shape
    return pl.pallas_call(
        _fft_b_filter_kernel,
        grid=(N1,),
        in_specs=[pl.BlockSpec((2, None, N2, D2), lambda k: (0, k, 0, 0)), pl.BlockSpec((1, D2), lambda k: (0, 0)),
                  pl.BlockSpec((None, 2 * N2, 2 * N2), lambda k: (k, 0, 0))],
        out_specs=pl.BlockSpec((2, None, N2, D2 // 2), lambda k: (0, k, 0, 0)),
        out_shape=jax.ShapeDtypeStruct((2, N1, N2, D2 // 2), f32),
        compiler_params=_cparams(("arbitrary",), VMEM_LIMIT),
        name="fft_b_filter",
    )(z5, csum, g_fwd)


def _filter_mlp_kernel(z_ref, w1_ref, b1_ref, w2_ref, b2_ref, w3_ref, b3_ref, fr_ref, dl_ref, o_ref, cs_ref, *, L, tf):
    i = pl.program_id(0)
    hp = lax.Precision.HIGHEST
    z = z_ref[...]
    fr = fr_ref[...]
    h = jnp.sin(fr * (jnp.dot(z, w1_ref[...], preferred_element_type=f32, precision=hp) + b1_ref[...]))
    h = jnp.sin(fr * (jnp.dot(h, w2_ref[...], preferred_element_type=f32, precision=hp) + b2_ref[...]))
    h = jnp.dot(h, w3_ref[...], preferred_element_type=f32, precision=hp) + b3_ref[...]
    win = jnp.exp(-z[:, 0:1] * dl_ref[...]) + DECAY_SHIFT
    h = h * jnp.concatenate([win, win], axis=1)
    rows = i * tf + lax.broadcasted_iota(jnp.int32, (tf, 1), 0)
    h = jnp.where(rows < L, h, 0.0)

    @pl.when(i == 0)
    def _():
        cs_ref[...] = jnp.zeros_like(cs_ref)

    cs_ref[...] += jnp.sum(jnp.abs(h), axis=0, keepdims=True)
    lanes = lax.broadcasted_iota(jnp.int32, (1, h.shape[1]), 1)
    o_ref[...] = jnp.where((rows == 0) & (lanes >= D_MODEL), 0.0, h).astype(o_ref.dtype)


def filter_mlp(z, w1, b1, w2, b2, w3, b3, freq, deltas, cfg, tf=512):
    NR = cfg.NR
    D2 = 2 * D_MODEL
    fixed = lambda i: (0, 0)
    full = lambda a: pl.BlockSpec(a.shape, fixed)
    args = (w1, b1.reshape(1, -1), w2, b2.reshape(1, -1), w3, b3.reshape(1, -1), freq.reshape(1, -1), deltas.reshape(1, -1))
    return pl.pallas_call(
        functools.partial(_filter_mlp_kernel, L=cfg.L, tf=tf),
        grid=(NR // tf,),
        in_specs=[pl.BlockSpec((tf, FILTER_EMB_PAD), lambda i: (i, 0))] + [full(a) for a in args],
        out_specs=[pl.BlockSpec((tf, D2), lambda i: (i, 0)), pl.BlockSpec((1, D2), fixed)],
        out_shape=[jax.ShapeDtypeStruct((NR, D2), bf16), jax.ShapeDtypeStruct((1, D2), f32)],
        compiler_params=_cparams(("arbitrary",), VMEM_LIMIT),
        name="filter_mlp",
    )(z, *args)


def _filter_features(cfg):
    L = cfg.L
    pos = jnp.arange(cfg.NR, dtype=f32)
    t = (pos / (L - 1))[:, None]
    w = (2.0 * math.pi / L) * pos[:, None]
    bands = jnp.linspace(1e-4, FILTER_BANDS - 1, FILTER_BANDS, dtype=f32)
    z = jnp.concatenate([t, jnp.cos(w * bands), -jnp.sin(w * bands)], axis=-1)
    return jnp.pad(z, ((0, 0), (0, FILTER_EMB_PAD - FILTER_EMB_DIM)))


def hyena_filter_spectrum(cfg, tables, w1, b1, w2, b2, w3, b3, freq):
    fa, _, g_fwd, _ = tables
    z = _filter_features(cfg)
    w1p = jnp.pad(w1, ((0, FILTER_EMB_PAD - FILTER_EMB_DIM), (0, 0)))
    max_decay = math.log(1.0 / DECAY_TARGET) / FAST_DECAY_PCT
    min_decay = math.log(1.0 / DECAY_TARGET) / SLOW_DECAY_PCT
    deltas = jnp.linspace(min_decay, max_decay, D_MODEL, dtype=f32)
    xf, csum = filter_mlp(z, w1p, b1, w2, b2, w3, b3, freq, deltas, cfg)
    D2 = 2 * D_MODEL
    z2 = left_matmul(fa, xf.reshape(1, cfg.NIN, FFT_N2 * D2), bf16)
    z5 = z2.reshape(2, cfg.N1H, FFT_N2, D2)
    return fft_b_filter(z5, csum, g_fwd)


def long_conv(vg, hspec, tables, cfg):
    fa, fa_inv, g_fwd, g_inv = tables
    B = vg.shape[0]
    D = D_MODEL
    z2 = left_matmul(fa, vg.reshape(B, cfg.NIN, FFT_N2 * D), bf16)
    q5 = fft_b_fused(z2.reshape(B, 2, cfg.N1H, FFT_N2, D), hspec, g_fwd, g_inv)
    y2 = left_matmul(fa_inv, q5.reshape(B, 2 * cfg.N1H, FFT_N2 * D), bf16)
    return y2.reshape(B, cfg.NR, D)


def _qkv_prep_kernel(x_ref, cc_ref, ss_ref, qg_ref, kg_ref, q_ref, k_ref, v_ref, *, L, tl):
    rows = pl.program_id(1) * tl + lax.broadcasted_iota(jnp.int32, (tl, 1), 0)
    valid = rows < L
    cc, ss = cc_ref[...], ss_ref[...]

    def norm_rope(x, gain, scale):
        x = x.astype(f32)
        xn = x * lax.rsqrt(jnp.mean(x * x, axis=-1, keepdims=True) + RMS_EPS) * gain
        y = xn * cc + pltpu.roll(xn, HEAD_DIM // 2, axis=1) * ss
        return jnp.where(valid, y * scale, 0.0).astype(bf16)

    nq = N_HEADS * HEAD_DIM
    for hh in range(N_HEADS):
        sl = slice(hh * HEAD_DIM, (hh + 1) * HEAD_DIM)
        q_ref[:, sl] = norm_rope(x_ref[:, sl], qg_ref[...], LOG2E * HEAD_DIM ** -0.5)
    for hh in range(N_KV_HEADS):
        sl = slice(hh * HEAD_DIM, (hh + 1) * HEAD_DIM)
        k_ref[:, sl] = norm_rope(x_ref[:, nq + hh * HEAD_DIM: nq + (hh + 1) * HEAD_DIM], kg_ref[...], 1.0)
    nk = N_KV_HEADS * HEAD_DIM
    v_ref[...] = jnp.where(valid, x_ref[:, nq + nk:], jnp.zeros((), x_ref.dtype))


def qkv_prep(qkv, cc, ss, q_gain, k_gain, cfg):
    B = qkv.shape[0]
    tl = cfg.TL
    nq, nk = N_HEADS * HEAD_DIM, N_KV_HEADS * HEAD_DIM
    row = lambda b, i: (b, i, 0)
    tab = lambda b, i: (i, 0)
    fixed = lambda b, i: (0, 0)
    return pl.pallas_call(
        functools.partial(_qkv_prep_kernel, L=cfg.L, tl=tl),
        grid=(B, cfg.Lp // tl),
        in_specs=[pl.BlockSpec((None, tl, nq + 2 * nk), row), pl.BlockSpec((tl, HEAD_DIM), tab),
                  pl.BlockSpec((tl, HEAD_DIM), tab), pl.BlockSpec((1, HEAD_DIM), fixed), pl.BlockSpec((1, HEAD_DIM), fixed)],
        out_specs=[pl.BlockSpec((None, tl, nq), row), pl.BlockSpec((None, tl, nk), row), pl.BlockSpec((None, tl, nk), row)],
        out_shape=[jax.ShapeDtypeStruct((B, cfg.Lp, nq), bf16), jax.ShapeDtypeStruct((B, cfg.Lp, nk), bf16),
                   jax.ShapeDtypeStruct((B, cfg.Lp, nk), bf16)],
        compiler_params=_cparams(("arbitrary", "arbitrary"), VMEM_LIMIT),
        name="qkv_prep",
    )(qkv, cc, ss, q_gain.reshape(1, -1), k_gain.reshape(1, -1))


def _flash_kernel(q_ref, k_ref, v_ref, o_ref, s0_scr, s1_scr, p0_scr, p1_scr, *, L):
    s_bufs, p_bufs = (s0_scr, s1_scr), (p0_scr, p1_scr)
    lp = k_ref.shape[0]
    c0 = (L // 128) * 128
    for g in range(GROUP):
        s_scr, p_scr = s_bufs[g % 2], p_bufs[g % 2]
        q = q_ref[:, g * HEAD_DIM:(g + 1) * HEAD_DIM]
        s = lax.dot_general(q, k_ref[...], (((1,), (1,)), ((), ())), preferred_element_type=f32)
        if c0 < lp:
            cols = c0 + lax.broadcasted_iota(jnp.int32, (1, lp - c0), 1)
            s_scr[:, :c0] = s[:, :c0]
            s_scr[:, c0:] = jnp.where(cols < L, s[:, c0:], NEG_BIG)
        else:
            s_scr[...] = s
        s = s_scr[...]
        p = jnp.exp2(s - jnp.max(s, axis=-1, keepdims=True))
        l = jnp.sum(p, axis=-1, keepdims=True)
        p_scr[...] = p.astype(bf16)
        o = jnp.dot(p_scr[...], v_ref[...], preferred_element_type=f32)
        o_ref[:, g * HEAD_DIM:(g + 1) * HEAD_DIM] = (o / l).astype(o_ref.dtype)


def flash_attention(q, k, v, cfg):
    B = q.shape[0]
    tq, lp = cfg.TQ, cfg.Lp
    gw = GROUP * HEAD_DIM
    return pl.pallas_call(
        functools.partial(_flash_kernel, L=cfg.L),
        grid=(B, N_KV_HEADS, lp // tq),
        in_specs=[pl.BlockSpec((None, tq, gw), lambda b, h, i: (b, i, h)),
                  pl.BlockSpec((None, lp, HEAD_DIM), lambda b, h, i: (b, 0, h)),
                  pl.BlockSpec((None, lp, HEAD_DIM), lambda b, h, i: (b, 0, h))],
        out_specs=pl.BlockSpec((None, tq, gw), lambda b, h, i: (b, i, h)),
        out_shape=jax.ShapeDtypeStruct(q.shape, bf16),
        scratch_shapes=[pltpu.VMEM((tq, lp), f32), pltpu.VMEM((tq, lp), f32),
                        pltpu.VMEM((tq, lp), bf16), pltpu.VMEM((tq, lp), bf16)],
        compiler_params=_cparams(("arbitrary", "arbitrary", "arbitrary"), VMEM_LIMIT),
        name="flash",
    )(q, k, v)


def _rope_tables(cfg):
    n = cfg.L - N_META
    rows = n // GRID_W
    row = jnp.concatenate([jnp.full((N_META,), -1.0, f32), jnp.repeat(jnp.arange(rows, dtype=f32), GRID_W)])
    col = jnp.concatenate([jnp.arange(N_META, dtype=f32), jnp.tile(jnp.arange(GRID_W, dtype=f32), rows)])
    axis_rot = HEAD_DIM // 2
    inv_freq = ROPE_THETA ** (-jnp.arange(0, axis_rot, 2, dtype=f32) / axis_rot)
    ang = jnp.concatenate([row[:, None] * inv_freq, col[:, None] * inv_freq], axis=-1)
    ang = jnp.pad(ang, ((0, cfg.Lp - cfg.L), (0, 0)))
    c, s = jnp.cos(ang), jnp.sin(ang)
    return jnp.concatenate([c, c], axis=-1), jnp.concatenate([-s, s], axis=-1)


def _run_trunk(x, cfg, meta_tokens, hy, at, ln, ffn):
    B, n, D = x.shape
    Lp, tm = cfg.Lp, cfg.TM
    R = B * Lp
    meta = jnp.broadcast_to(meta_tokens[None], (B, N_META, D))
    h = jnp.concatenate([meta, x, jnp.zeros((B, Lp - cfg.L, D), x.dtype)], axis=1).reshape(R, D)
    hb = h.astype(bf16)
    cc, ss = _rope_tables(cfg)
    tables = _dft_tables(cfg)
    for i in range(DEPTH):
        j = i // 2
        if i % 2 == 0:
            hspec = hyena_filter_spectrum(cfg, tables, hy["f_w1"][j], hy["f_b1"][j], hy["f_w2"][j], hy["f_b2"][j],
                                          hy["f_w3"][j], hy["f_b3"][j], hy["f_freq"][j])
            u = matmul(hb, hy["w_in"][j], bf16, tm, 1024).reshape(B, Lp, 3 * D)
            vg = hyena_gate(u, hy["conv_w"][j], hy["conv_b"][j], cfg)
            yc = long_conv(vg, hspec, tables, cfg)
            y = hyena_post(yc, vg, u, hy["conv_w"][j], hy["conv_b"][j], hy["skip"][j], cfg)
            h, hb = matmul_residual_ln(y.reshape(R, D), hy["w_out"][j], h, ln["g1"][i], ln["b1"][i], cfg.TL)
        else:
            qkv = matmul(hb, at["w_qkv"][j], bf16, tm, 512).reshape(B, Lp, -1)
            q, k, v = qkv_prep(qkv, cc, ss, at["q_gain"][j], at["k_gain"][j], cfg)
            o = flash_attention(q, k, v, cfg)
            h, hb = matmul_residual_ln(o.reshape(R, D), at["w_out"][j], h, ln["g1"][i], ln["b1"][i], cfg.TL)
        hid = matmul(hb, ffn["w_in"][i], bf16, tm, 512).reshape(B, Lp, 2 * D_FF)
        gated = ffn_gate(hid, ffn["conv_w"][i], ffn["conv_b"][i], cfg)
        h, hb = matmul_residual_ln(gated.reshape(R, D_FF), ffn["w_out"][i], h, ln["g2"][i], ln["b2"][i], cfg.TL)
    return h.reshape(B, Lp, D)[:, N_META:cfg.L]


def kernel(x_prompt, x_sample, meta_tokens, hy_w_in, hy_conv_w, hy_conv_b, hy_filt_w1, hy_filt_b1, hy_filt_w2, hy_filt_b2, hy_filt_w3, hy_filt_b3, hy_filt_freq, hy_skip, hy_w_out, at_w_qkv, at_q_gain, at_k_gain, at_w_out, ln1_g, ln1_b, ln2_g, ln2_b, ffn_w_in, ffn_conv_w, ffn_conv_b, ffn_w_out):
    hy = dict(w_in=hy_w_in.astype(bf16), conv_w=hy_conv_w, conv_b=hy_conv_b, f_w1=hy_filt_w1, f_b1=hy_filt_b1,
              f_w2=hy_filt_w2, f_b2=hy_filt_b2, f_w3=hy_filt_w3, f_b3=hy_filt_b3, f_freq=hy_filt_freq, skip=hy_skip,
              w_out=hy_w_out.astype(bf16))
    at = dict(w_qkv=at_w_qkv.astype(bf16), q_gain=at_q_gain, k_gain=at_k_gain, w_out=at_w_out.astype(bf16))
    ln = dict(g1=ln1_g, b1=ln1_b, g2=ln2_g, b2=ln2_b)
    ffn = dict(w_in=ffn_w_in.astype(bf16), conv_w=ffn_conv_w, conv_b=ffn_conv_b, w_out=ffn_w_out.astype(bf16))
    y_prompt = _run_trunk(x_prompt, Cfg(x_prompt.shape[1]), meta_tokens, hy, at, ln, ffn)
    y_sample = _run_trunk(x_sample, Cfg(x_sample.shape[1]), meta_tokens, hy, at, ln, ffn)
    return (y_prompt, y_sample)
```

```python
import functools
import math

import jax
import jax.numpy as jnp
from jax import lax
from jax.experimental import pallas as pl
from jax.experimental.pallas import tpu as pltpu

f32 = jnp.float32
bf16 = jnp.bfloat16

D_MODEL = 1024
DEPTH = 4
N_META = 16
GRID_W = 64
HEAD_DIM = 128
N_HEADS = 8
N_KV_HEADS = 2
GROUP = N_HEADS // N_KV_HEADS
ROPE_THETA = 10000.0
FILTER_EMB_DIM = 33
FILTER_EMB_PAD = 40
FILTER_BANDS = 16
FILTER_HIDDEN = 64
DECAY_TARGET = 1e-2
FAST_DECAY_PCT = 0.3
SLOW_DECAY_PCT = 1.5
DECAY_SHIFT = 0.05
FILTER_NORM_EPS = 1e-6
D_FF = 2816
DEEPNORM_ALPHA = (2 * DEPTH) ** 0.25
LN_EPS = 1e-5
RMS_EPS = 1e-6
NEG_BIG = -1e30
LOG2E = 1.4426950408889634

FFT_N2 = 128
V7X_VMEM_BYTES = 64 * 1024 * 1024
VMEM_LIMIT = 52 * 1024 * 1024


class Cfg:
    def __init__(self, n):
        self.L = n + N_META
        if n == 8192:
            self.Lp, self.TL, self.TM, self.TQ, self.N1, self.NIN = 8320, 640, 1280, 128, 136, 80
        elif n == 2048:
            self.Lp, self.TL, self.TM, self.TQ, self.N1, self.NIN = 2304, 768, 1024, 256, 40, 32
        else:
            up = lambda a, m: -(-a // m) * m
            self.TL = self.TM = self.TQ = 128
            self.Lp = up(self.L, 128)
            self.N1 = up(-(-(2 * self.L - 1) // FFT_N2), 8)
            self.NIN = up(-(-self.Lp // FFT_N2), 16)
        self.NR = self.NIN * FFT_N2
        self.N1H = -(-(self.N1 // 2 + 1) // 8) * 8
        self.N = self.N1 * FFT_N2
        assert self.N >= 2 * self.L - 1 and self.NR >= self.Lp
        assert self.Lp % self.TL == 0 and self.Lp % self.TQ == 0


def _cparams(sem, vmem=None):
    return pltpu.CompilerParams(dimension_semantics=sem, vmem_limit_bytes=vmem)


def _mm_kernel(x_ref, w_ref, o_ref):
    o_ref[...] = jnp.dot(x_ref[...], w_ref[...], preferred_element_type=f32).astype(o_ref.dtype)


def matmul(x, w, out_dtype, tm, tn):
    R, K = x.shape
    N = w.shape[1]
    assert R % tm == 0 and N % tn == 0
    return pl.pallas_call(
        _mm_kernel,
        grid=(N // tn, R // tm),
        in_specs=[pl.BlockSpec((tm, K), lambda j, i: (i, 0)), pl.BlockSpec((K, tn), lambda j, i: (0, j))],
        out_specs=pl.BlockSpec((tm, tn), lambda j, i: (i, j)),
        out_shape=jax.ShapeDtypeStruct((R, N), out_dtype),
        compiler_params=_cparams(("arbitrary", "arbitrary"), VMEM_LIMIT),
        name="mm",
    )(x, w)


def _mm_ln_kernel(x_ref, w_ref, h_ref, g_ref, b_ref, o_ref, obf_ref):
    m = jnp.dot(x_ref[...], w_ref[...], preferred_element_type=f32)
    y = DEEPNORM_ALPHA * h_ref[...] + m
    mu = jnp.mean(y, axis=-1, keepdims=True)
    yc = y - mu
    var = jnp.mean(yc * yc, axis=-1, keepdims=True)
    out = yc * lax.rsqrt(var + LN_EPS) * g_ref[...] + b_ref[...]
    o_ref[...] = out
    obf_ref[...] = out.astype(bf16)


def matmul_residual_ln(x, w, h, g, b, tm):
    R, K = x.shape
    N = w.shape[1]
    row = lambda i: (i, 0)
    fixed = lambda i: (0, 0)
    return pl.pallas_call(
        _mm_ln_kernel,
        grid=(R // tm,),
        in_specs=[pl.BlockSpec((tm, K), row), pl.BlockSpec((K, N), fixed), pl.BlockSpec((tm, N), row),
                  pl.BlockSpec((1, N), fixed), pl.BlockSpec((1, N), fixed)],
        out_specs=[pl.BlockSpec((tm, N), row), pl.BlockSpec((tm, N), row)],
        out_shape=[jax.ShapeDtypeStruct((R, N), f32), jax.ShapeDtypeStruct((R, N), bf16)],
        compiler_params=_cparams(("arbitrary",), VMEM_LIMIT),
        name="mm_ln",
    )(x, w, h, g.reshape(1, N), b.reshape(1, N))


def _left_mm_kernel(a_ref, x_ref, o_ref, *, cw):
    a = a_ref[...]

    def body(c, carry):
        sl = pl.ds(pl.multiple_of(c * cw, cw), cw)
        o_ref[:, sl] = jnp.dot(a, x_ref[:, sl], preferred_element_type=f32).astype(o_ref.dtype)
        return carry

    lax.fori_loop(0, x_ref.shape[1] // cw, body, 0)


def left_matmul(a, x, out_dtype, tn=8192, cw=512):
    M, K = a.shape
    B, _, C = x.shape
    assert C % tn == 0 and tn % cw == 0
    return pl.pallas_call(
        functools.partial(_left_mm_kernel, cw=cw),
        grid=(B, C // tn),
        in_specs=[pl.BlockSpec((M, K), lambda b, j: (0, 0)), pl.BlockSpec((None, K, tn), lambda b, j: (b, 0, j))],
        out_specs=pl.BlockSpec((None, M, tn), lambda b, j: (b, 0, j)),
        out_shape=jax.ShapeDtypeStruct((B, M, C), out_dtype),
        compiler_params=_cparams(("arbitrary", "arbitrary"), VMEM_LIMIT),
        name="left_mm",
    )(a, x)


def _conv3_rows(ref, r, ch, Lp, L, w, b):
    c = ref[pl.ds(r, ch), :].astype(f32)
    rows = r + lax.broadcasted_iota(jnp.int32, (ch, 1), 0)
    if r == 0:
        p = jnp.where(rows >= 1, pltpu.roll(c, 1, axis=0), 0.0)
    else:
        p = ref[pl.ds(r - 1, ch), :].astype(f32)
    if r + ch == Lp:
        n = pltpu.roll(c, ch - 1, axis=0)
    else:
        n = ref[pl.ds(r + 1, ch), :].astype(f32)
    n = jnp.where(rows + 1 < L, n, 0.0)
    return p * w[0:1, :] + c * w[1:2, :] + n * w[2:3, :] + b, rows


def _hyena_gate_kernel(x1_ref, v_ref, w1_ref, b1_ref, wv_ref, bv_ref, o_ref, *, L, Lp, ch):
    w1, b1, wv, bv = w1_ref[...], b1_ref[...], wv_ref[...], bv_ref[...]
    for r in range(0, Lp, ch):
        cx, rows = _conv3_rows(x1_ref, r, ch, Lp, L, w1, b1)
        cv, _ = _conv3_rows(v_ref, r, ch, Lp, L, wv, bv)
        o_ref[pl.ds(r, ch), :] = jnp.where(rows < L, cx * cv, 0.0).astype(o_ref.dtype)
    NR = o_ref.shape[0]
    if NR > Lp:
        o_ref[pl.ds(Lp, NR - Lp), :] = jnp.zeros((NR - Lp, o_ref.shape[1]), o_ref.dtype)


def hyena_gate(u, conv_w, conv_b, cfg, tc=256):
    B = u.shape[0]
    D = D_MODEL
    nb = D // tc
    sec = lambda s: (lambda b, c: (b, 0, s * nb + c))
    wsec = lambda s: (lambda b, c: (0, s * nb + c))
    return pl.pallas_call(
        functools.partial(_hyena_gate_kernel, L=cfg.L, Lp=cfg.Lp, ch=cfg.TL),
        grid=(B, nb),
        in_specs=[pl.BlockSpec((None, cfg.Lp, tc), sec(1)), pl.BlockSpec((None, cfg.Lp, tc), sec(2)),
                  pl.BlockSpec((3, tc), wsec(1)), pl.BlockSpec((1, tc), wsec(1)),
                  pl.BlockSpec((3, tc), wsec(2)), pl.BlockSpec((1, tc), wsec(2))],
        out_specs=pl.BlockSpec((None, cfg.NR, tc), lambda b, c: (b, 0, c)),
        out_shape=jax.ShapeDtypeStruct((B, cfg.NR, D), bf16),
        compiler_params=_cparams(("arbitrary", "arbitrary"), VMEM_LIMIT),
        name="hyena_gate",
    )(u, u, conv_w, conv_b.reshape(1, -1), conv_w, conv_b.reshape(1, -1))


def _hyena_post_kernel(y_ref, vg_ref, x0_ref, w0_ref, b0_ref, skip_ref, o_ref, *, L, Lp, ch):
    w0, b0, skip = w0_ref[...], b0_ref[...], skip_ref[...]
    for r in range(0, Lp, ch):
        cx, _ = _conv3_rows(x0_ref, r, ch, Lp, L, w0, b0)
        y = y_ref[pl.ds(r, ch), :].astype(f32) + vg_ref[pl.ds(r, ch), :].astype(f32) * skip
        o_ref[pl.ds(r, ch), :] = (y * cx).astype(o_ref.dtype)


def hyena_post(yc, vg, u, conv_w, conv_b, skip, cfg, tc=256):
    B = u.shape[0]
    D = D_MODEL
    col = lambda b, c: (b, 0, c)
    return pl.pallas_call(
        functools.partial(_hyena_post_kernel, L=cfg.L, Lp=cfg.Lp, ch=cfg.TL),
        grid=(B, D // tc),
        in_specs=[pl.BlockSpec((None, cfg.Lp, tc), col), pl.BlockSpec((None, cfg.Lp, tc), col),
                  pl.BlockSpec((None, cfg.Lp, tc), col),
                  pl.BlockSpec((3, tc), lambda b, c: (0, c)), pl.BlockSpec((1, tc), lambda b, c: (0, c)),
                  pl.BlockSpec((1, tc), lambda b, c: (0, c))],
        out_specs=pl.BlockSpec((None, cfg.Lp, tc), col),
        out_shape=jax.ShapeDtypeStruct((B, cfg.Lp, D), bf16),
        compiler_params=_cparams(("arbitrary", "arbitrary"), VMEM_LIMIT),
        name="hyena_post",
    )(yc, vg, u, conv_w, conv_b.reshape(1, -1), skip.reshape(1, -1))


HALO = 16


def _ffn_in_gate_kernel(x_ref, xp_ref, xn_ref, wg_ref, wa_ref, cw_ref, cb_ref, o_ref, x_scr, g_scr, *, L, tm, tpb):
    i, c = pl.program_id(0), pl.program_id(1)

    @pl.when(c == 0)
    def _():
        x_scr[0:HALO, :] = xp_ref[...]
        x_scr[HALO:HALO + tm, :] = x_ref[...]
        x_scr[HALO + tm:, :] = xn_ref[...]

    g_scr[...] = jnp.dot(x_scr[...], wg_ref[...], preferred_element_type=f32)
    a = jnp.dot(x_ref[...], wa_ref[...], preferred_element_type=f32)
    t = (i % tpb) * tm + lax.broadcasted_iota(jnp.int32, (tm, 1), 0)
    cw = cw_ref[...]
    prev = jnp.where(t >= 1, g_scr[HALO - 1:HALO - 1 + tm, :], 0.0)
    nxt = jnp.where(t + 1 < L, g_scr[HALO + 1:HALO + 1 + tm, :], 0.0)
    g = prev * cw[0:1, :] + g_scr[HALO:HALO + tm, :] * cw[1:2, :] + nxt * cw[2:3, :] + cb_ref[...]
    gelu = 0.5 * g * (1.0 + lax.erf(g * (2.0 ** -0.5)))
    o_ref[...] = (gelu * a).astype(o_ref.dtype)


def ffn_in_gate(x, w_in, conv_w, conv_b, cfg, nchunk=2):
    R, K = x.shape
    tm = cfg.TL
    tc = D_FF // nchunk
    tpb = cfg.Lp // tm
    nh = tm // HALO
    last = R // HALO - 1
    assert D_FF % nchunk == 0 and tc % 128 == 0 and tm % HALO == 0
    return pl.pallas_call(
        functools.partial(_ffn_in_gate_kernel, L=cfg.L, tm=tm, tpb=tpb),
        grid=(R // tm, nchunk),
        in_specs=[pl.BlockSpec((tm, K), lambda i, c: (i, 0)),
                  pl.BlockSpec((HALO, K), lambda i, c: (jnp.maximum(i * nh - 1, 0), 0)),
                  pl.BlockSpec((HALO, K), lambda i, c: (jnp.minimum((i + 1) * nh, last), 0)),
                  pl.BlockSpec((K, tc), lambda i, c: (0, c)), pl.BlockSpec((K, tc), lambda i, c: (0, nchunk + c)),
                  pl.BlockSpec((3, tc), lambda i, c: (0, c)), pl.BlockSpec((1, tc), lambda i, c: (0, c))],
        out_specs=pl.BlockSpec((tm, tc), lambda i, c: (i, c)),
        out_shape=jax.ShapeDtypeStruct((R, D_FF), bf16),
        scratch_shapes=[pltpu.VMEM((tm + 2 * HALO, K), bf16), pltpu.VMEM((tm + 2 * HALO, tc), f32)],
        compiler_params=_cparams(("arbitrary", "arbitrary"), VMEM_LIMIT),
        name="ffn_in_gate",
    )(x, x, x, w_in, w_in, conv_w, conv_b.reshape(1, -1))


def _dft_tables(cfg):
    N1, NIN, N2, N = cfg.N1, cfg.NIN, FFT_N2, cfg.N
    assert N1 % 2 == 0
    k1 = jnp.arange(cfg.N1H, dtype=jnp.int32)
    wk = jnp.where((k1 == 0) | (k1 == N1 // 2), 1.0, jnp.where(k1 < N1 // 2, 2.0, 0.0)).astype(f32)
    n1 = jnp.arange(NIN, dtype=jnp.int32)
    ang = (2.0 * math.pi / N1) * ((k1[:, None] * n1[None, :]) % N1).astype(f32)
    ca, sa = jnp.cos(ang), jnp.sin(ang)
    fa = jnp.concatenate([ca, -sa], axis=0)
    fa_inv = jnp.concatenate([ca.T * wk, -sa.T * wk], axis=1) * (1.0 / N)
    k2 = jnp.arange(N2, dtype=jnp.int32)[:, None]
    n2 = jnp.arange(N2, dtype=jnp.int32)[None, :]
    ph = ((n2 * k2 * N1)[None] + n2[None] * k1[:, None, None]) % N
    ang = (2.0 * math.pi / N) * ph.astype(f32)
    ar, ai = jnp.cos(ang), -jnp.sin(ang)
    g_fwd = jnp.concatenate([jnp.concatenate([ar, -ai], axis=2), jnp.concatenate([ai, ar], axis=2)], axis=1)
    art, ait = jnp.swapaxes(ar, 1, 2), -jnp.swapaxes(ai, 1, 2)
    g_inv = jnp.concatenate([jnp.concatenate([art, -ait], axis=2), jnp.concatenate([ait, art], axis=2)], axis=1)
    return fa.astype(bf16), fa_inv.astype(bf16), g_fwd.astype(bf16), g_inv.astype(bf16)


def _fft_b_fused_kernel(z_ref, h_ref, gf_ref, gi_ref, o_ref):
    zc = z_ref[...].reshape(2 * FFT_N2, z_ref.shape[-1])
    y = jnp.dot(gf_ref[...], zc, preferred_element_type=f32)
    yr, yi = y[:FFT_N2], y[FFT_N2:]
    hr, hi = h_ref[0], h_ref[1]
    pc = jnp.concatenate([yr * hr - yi * hi, yr * hi + yi * hr], axis=0).astype(bf16)
    q = jnp.dot(gi_ref[...], pc, preferred_element_type=f32)
    o_ref[0] = q[:FFT_N2].astype(o_ref.dtype)
    o_ref[1] = q[FFT_N2:].astype(o_ref.dtype)


def fft_b_fused(z5, hspec, g_fwd, g_inv):
    B, _, N1, N2, D = z5.shape
    zspec = pl.BlockSpec((None, 2, None, N2, D), lambda k, b: (b, 0, k, 0, 0))
    gspec = pl.BlockSpec((None, 2 * N2, 2 * N2), lambda k, b: (k, 0, 0))
    return pl.pallas_call(
        _fft_b_fused_kernel,
        grid=(N1, B),
        in_specs=[zspec, pl.BlockSpec((2, None, N2, D), lambda k, b: (0, k, 0, 0)), gspec, gspec],
        out_specs=zspec,
        out_shape=jax.ShapeDtypeStruct(z5.shape, bf16),
        compiler_params=_cparams(("arbitrary", "arbitrary"), VMEM_LIMIT),
        name="fft_b_fused",
    )(z5, hspec, g_fwd, g_inv)


def _fft_b_filter_kernel(z_ref, csum_ref, gf_ref, o_ref):
    D = D_MODEL
    zc = z_ref[...].reshape(2 * FFT_N2, 2 * D)
    y = jnp.dot(gf_ref[...], zc, preferred_element_type=f32)
    scale = 1.0 / (csum_ref[...] + FILTER_NORM_EPS)
    y = y * scale
    yr, yi = y[:FFT_N2], y[FFT_N2:]
    o_ref[0] = yr[:, :D] + yr[:, D:]
    o_ref[1] = yi[:, :D] - yi[:, D:]


def fft_b_filter(z5, csum, g_fwd):
    _, N1, N2, D2 = z5.shape
    return pl.pallas_call(
        _fft_b_filter_kernel,
        grid=(N1,),
        in_specs=[pl.BlockSpec((2, None, N2, D2), lambda k: (0, k, 0, 0)), pl.BlockSpec((1, D2), lambda k: (0, 0)),
                  pl.BlockSpec((None, 2 * N2, 2 * N2), lambda k: (k, 0, 0))],
        out_specs=pl.BlockSpec((2, None, N2, D2 // 2), lambda k: (0, k, 0, 0)),
        out_shape=jax.ShapeDtypeStruct((2, N1, N2, D2 // 2), f32),
        compiler_params=_cparams(("arbitrary",), VMEM_LIMIT),
        name="fft_b_filter",
    )(z5, csum, g_fwd)


def _filter_mlp_kernel(z_ref, w1_ref, b1_ref, w2_ref, b2_ref, w3_ref, b3_ref, fr_ref, dl_ref, o_ref, cs_ref, *, L, tf):
    i = pl.program_id(0)
    hp = lax.Precision.HIGHEST
    z = z_ref[...]
    fr = fr_ref[...]
    h = jnp.sin(fr * (jnp.dot(z, w1_ref[...], preferred_element_type=f32, precision=hp) + b1_ref[...]))
    h = jnp.sin(fr * (jnp.dot(h, w2_ref[...], preferred_element_type=f32, precision=hp) + b2_ref[...]))
    h = jnp.dot(h, w3_ref[...], preferred_element_type=f32, precision=hp) + b3_ref[...]
    win = jnp.exp(-z[:, 0:1] * dl_ref[...]) + DECAY_SHIFT
    h = h * jnp.concatenate([win, win], axis=1)
    rows = i * tf + lax.broadcasted_iota(jnp.int32, (tf, 1), 0)
    h = jnp.where(rows < L, h, 0.0)

    @pl.when(i == 0)
    def _():
        cs_ref[...] = jnp.zeros_like(cs_ref)

    cs_ref[...] += jnp.sum(jnp.abs(h), axis=0, keepdims=True)
    lanes = lax.broadcasted_iota(jnp.int32, (1, h.shape[1]), 1)
    o_ref[...] = jnp.where((rows == 0) & (lanes >= D_MODEL), 0.0, h).astype(o_ref.dtype)


def filter_mlp(z, w1, b1, w2, b2, w3, b3, freq, deltas, cfg, tf=512):
    NR = cfg.NR
    D2 = 2 * D_MODEL
    fixed = lambda i: (0, 0)
    full = lambda a: pl.BlockSpec(a.shape, fixed)
    args = (w1, b1.reshape(1, -1), w2, b2.reshape(1, -1), w3, b3.reshape(1, -1), freq.reshape(1, -1), deltas.reshape(1, -1))
    return pl.pallas_call(
        functools.partial(_filter_mlp_kernel, L=cfg.L, tf=tf),
        grid=(NR // tf,),
        in_specs=[pl.BlockSpec((tf, FILTER_EMB_PAD), lambda i: (i, 0))] + [full(a) for a in args],
        out_specs=[pl.BlockSpec((tf, D2), lambda i: (i, 0)), pl.BlockSpec((1, D2), fixed)],
        out_shape=[jax.ShapeDtypeStruct((NR, D2), bf16), jax.ShapeDtypeStruct((1, D2), f32)],
        compiler_params=_cparams(("arbitrary",), VMEM_LIMIT),
        name="filter_mlp",
    )(z, *args)


def _filter_features(cfg):
    L = cfg.L
    pos = jnp.arange(cfg.NR, dtype=f32)
    t = (pos / (L - 1))[:, None]
    w = (2.0 * math.pi / L) * pos[:, None]
    bands = jnp.linspace(1e-4, FILTER_BANDS - 1, FILTER_BANDS, dtype=f32)
    z = jnp.concatenate([t, jnp.cos(w * bands), -jnp.sin(w * bands)], axis=-1)
    return jnp.pad(z, ((0, 0), (0, FILTER_EMB_PAD - FILTER_EMB_DIM)))


def hyena_filter_spectrum(cfg, tables, w1, b1, w2, b2, w3, b3, freq):
    fa, _, g_fwd, _ = tables
    z = _filter_features(cfg)
    w1p = jnp.pad(w1, ((0, FILTER_EMB_PAD - FILTER_EMB_DIM), (0, 0)))
    max_decay = math.log(1.0 / DECAY_TARGET) / FAST_DECAY_PCT
    min_decay = math.log(1.0 / DECAY_TARGET) / SLOW_DECAY_PCT
    deltas = jnp.linspace(min_decay, max_decay, D_MODEL, dtype=f32)
    xf, csum = filter_mlp(z, w1p, b1, w2, b2, w3, b3, freq, deltas, cfg)
    D2 = 2 * D_MODEL
    z2 = left_matmul(fa, xf.reshape(1, cfg.NIN, FFT_N2 * D2), bf16)
    z5 = z2.reshape(2, cfg.N1H, FFT_N2, D2)
    return fft_b_filter(z5, csum, g_fwd)


def long_conv(vg, hspec, tables, cfg):
    fa, fa_inv, g_fwd, g_inv = tables
    B = vg.shape[0]
    D = D_MODEL
    z2 = left_matmul(fa, vg.reshape(B, cfg.NIN, FFT_N2 * D), bf16)
    q5 = fft_b_fused(z2.reshape(B, 2, cfg.N1H, FFT_N2, D), hspec, g_fwd, g_inv)
    y2 = left_matmul(fa_inv, q5.reshape(B, 2 * cfg.N1H, FFT_N2 * D), bf16)
    return y2.reshape(B, cfg.NR, D)


def _qkv_prep_kernel(x_ref, cc_ref, ss_ref, qg_ref, kg_ref, q_ref, k_ref, v_ref, *, L, tl):
    rows = pl.program_id(1) * tl + lax.broadcasted_iota(jnp.int32, (tl, 1), 0)
    valid = rows < L
    cc, ss = cc_ref[...], ss_ref[...]

    def norm_rope(x, gain, scale):
        x = x.astype(f32)
        xn = x * lax.rsqrt(jnp.mean(x * x, axis=-1, keepdims=True) + RMS_EPS) * gain
        y = xn * cc + pltpu.roll(xn, HEAD_DIM // 2, axis=1) * ss
        return jnp.where(valid, y * scale, 0.0).astype(bf16)

    nq = N_HEADS * HEAD_DIM
    for hh in range(N_HEADS):
        sl = slice(hh * HEAD_DIM, (hh + 1) * HEAD_DIM)
        q_ref[:, sl] = norm_rope(x_ref[:, sl], qg_ref[...], LOG2E * HEAD_DIM ** -0.5)
    for hh in range(N_KV_HEADS):
        sl = slice(hh * HEAD_DIM, (hh + 1) * HEAD_DIM)
        k_ref[:, sl] = norm_rope(x_ref[:, nq + hh * HEAD_DIM: nq + (hh + 1) * HEAD_DIM], kg_ref[...], 1.0)
    nk = N_KV_HEADS * HEAD_DIM
    v_ref[...] = jnp.where(valid, x_ref[:, nq + nk:], jnp.zeros((), x_ref.dtype))


def qkv_prep(qkv, cc, ss, q_gain, k_gain, cfg):
    B = qkv.shape[0]
    tl = cfg.TL
    nq, nk = N_HEADS * HEAD_DIM, N_KV_HEADS * HEAD_DIM
    row = lambda b, i: (b, i, 0)
    tab = lambda b, i: (i, 0)
    fixed = lambda b, i: (0, 0)
    return pl.pallas_call(
        functools.partial(_qkv_prep_kernel, L=cfg.L, tl=tl),
        grid=(B, cfg.Lp // tl),
        in_specs=[pl.BlockSpec((None, tl, nq + 2 * nk), row), pl.BlockSpec((tl, HEAD_DIM), tab),
                  pl.BlockSpec((tl, HEAD_DIM), tab), pl.BlockSpec((1, HEAD_DIM), fixed), pl.BlockSpec((1, HEAD_DIM), fixed)],
        out_specs=[pl.BlockSpec((None, tl, nq), row), pl.BlockSpec((None, tl, nk), row), pl.BlockSpec((None, tl, nk), row)],
        out_shape=[jax.ShapeDtypeStruct((B, cfg.Lp, nq), bf16), jax.ShapeDtypeStruct((B, cfg.Lp, nk), bf16),
                   jax.ShapeDtypeStruct((B, cfg.Lp, nk), bf16)],
        compiler_params=_cparams(("arbitrary", "arbitrary"), VMEM_LIMIT),
        name="qkv_prep",
    )(qkv, cc, ss, q_gain.reshape(1, -1), k_gain.reshape(1, -1))


def _flash_kernel(q_ref, k_ref, v_ref, o_ref, s0_scr, s1_scr, p0_scr, p1_scr, *, L):
    s_bufs, p_bufs = (s0_scr, s1_scr), (p0_scr, p1_scr)
    lp = k_ref.shape[0]
    c0 = (L // 128) * 128
    for g in range(GROUP):
        s_scr, p_scr = s_bufs[g % 2], p_bufs[g % 2]
        q = q_ref[:, g * HEAD_DIM:(g + 1) * HEAD_DIM]
        s = lax.dot_general(q, k_ref[...], (((1,), (1,)), ((), ())), preferred_element_type=f32)
        if c0 < lp:
            cols = c0 + lax.broadcasted_iota(jnp.int32, (1, lp - c0), 1)
            s_scr[:, :c0] = s[:, :c0]
            s_scr[:, c0:] = jnp.where(cols < L, s[:, c0:], NEG_BIG)
        else:
            s_scr[...] = s
        s = s_scr[...]
        p = jnp.exp2(s - jnp.max(s, axis=-1, keepdims=True))
        l = jnp.sum(p, axis=-1, keepdims=True)
        p_scr[...] = p.astype(bf16)
        o = jnp.dot(p_scr[...], v_ref[...], preferred_element_type=f32)
        o_ref[:, g * HEAD_DIM:(g + 1) * HEAD_DIM] = (o / l).astype(o_ref.dtype)


def flash_attention(q, k, v, cfg):
    B = q.shape[0]
    tq, lp = cfg.TQ, cfg.Lp
    gw = GROUP * HEAD_DIM
    return pl.pallas_call(
        functools.partial(_flash_kernel, L=cfg.L),
        grid=(B, N_KV_HEADS, lp // tq),
        in_specs=[pl.BlockSpec((None, tq, gw), lambda b, h, i: (b, i, h)),
                  pl.BlockSpec((None, lp, HEAD_DIM), lambda b, h, i: (b, 0, h)),
                  pl.BlockSpec((None, lp, HEAD_DIM), lambda b, h, i: (b, 0, h))],
        out_specs=pl.BlockSpec((None, tq, gw), lambda b, h, i: (b, i, h)),
        out_shape=jax.ShapeDtypeStruct(q.shape, bf16),
        scratch_shapes=[pltpu.VMEM((tq, lp), f32), pltpu.VMEM((tq, lp), f32),
                        pltpu.VMEM((tq, lp), bf16), pltpu.VMEM((tq, lp), bf16)],
        compiler_params=_cparams(("arbitrary", "arbitrary", "arbitrary"), VMEM_LIMIT),
        name="flash",
    )(q, k, v)


def _rope_tables(cfg):
    n = cfg.L - N_META
    rows = n // GRID_W
    row = jnp.concatenate([jnp.full((N_META,), -1.0, f32), jnp.repeat(jnp.arange(rows, dtype=f32), GRID_W)])
    col = jnp.concatenate([jnp.arange(N_META, dtype=f32), jnp.tile(jnp.arange(GRID_W, dtype=f32), rows)])
    axis_rot = HEAD_DIM // 2
    inv_freq = ROPE_THETA ** (-jnp.arange(0, axis_rot, 2, dtype=f32) / axis_rot)
    ang = jnp.concatenate([row[:, None] * inv_freq, col[:, None] * inv_freq], axis=-1)
    ang = jnp.pad(ang, ((0, cfg.Lp - cfg.L), (0, 0)))
    c, s = jnp.cos(ang), jnp.sin(ang)
    return jnp.concatenate([c, c], axis=-1), jnp.concatenate([-s, s], axis=-1)


def _run_trunk(x, cfg, meta_tokens, hy, at, ln, ffn):
    B, n, D = x.shape
    Lp, tm = cfg.Lp, cfg.TM
    R = B * Lp
    meta = jnp.broadcast_to(meta_tokens[None], (B, N_META, D))
    h = jnp.concatenate([meta, x, jnp.zeros((B, Lp - cfg.L, D), x.dtype)], axis=1).reshape(R, D)
    hb = h.astype(bf16)
    cc, ss = _rope_tables(cfg)
    tables = _dft_tables(cfg)
    for i in range(DEPTH):
        j = i // 2
        if i % 2 == 0:
            hspec = hyena_filter_spectrum(cfg, tables, hy["f_w1"][j], hy["f_b1"][j], hy["f_w2"][j], hy["f_b2"][j],
                                          hy["f_w3"][j], hy["f_b3"][j], hy["f_freq"][j])
            u = matmul(hb, hy["w_in"][j], bf16, tm, 1024).reshape(B, Lp, 3 * D)
            vg = hyena_gate(u, hy["conv_w"][j], hy["conv_b"][j], cfg)
            yc = long_conv(vg, hspec, tables, cfg)
            y = hyena_post(yc, vg, u, hy["conv_w"][j], hy["conv_b"][j], hy["skip"][j], cfg)
            h, hb = matmul_residual_ln(y.reshape(R, D), hy["w_out"][j], h, ln["g1"][i], ln["b1"][i], cfg.TL)
        else:
            qkv = matmul(hb, at["w_qkv"][j], bf16, tm, 512).reshape(B, Lp, -1)
            q, k, v = qkv_prep(qkv, cc, ss, at["q_gain"][j], at["k_gain"][j], cfg)
            o = flash_attention(q, k, v, cfg)
            h, hb = matmul_residual_ln(o.reshape(R, D), at["w_out"][j], h, ln["g1"][i], ln["b1"][i], cfg.TL)
        gated = ffn_in_gate(hb, ffn["w_in"][i], ffn["conv_w"][i], ffn["conv_b"][i], cfg)
        h, hb = matmul_residual_ln(gated, ffn["w_out"][i], h, ln["g2"][i], ln["b2"][i], cfg.TL)
    return h.reshape(B, Lp, D)[:, N_META:cfg.L]


def kernel(x_prompt, x_sample, meta_tokens, hy_w_in, hy_conv_w, hy_conv_b, hy_filt_w1, hy_filt_b1, hy_filt_w2, hy_filt_b2, hy_filt_w3, hy_filt_b3, hy_filt_freq, hy_skip, hy_w_out, at_w_qkv, at_q_gain, at_k_gain, at_w_out, ln1_g, ln1_b, ln2_g, ln2_b, ffn_w_in, ffn_conv_w, ffn_conv_b, ffn_w_out):
    hy = dict(w_in=hy_w_in.astype(bf16), conv_w=hy_conv_w, conv_b=hy_conv_b, f_w1=hy_filt_w1, f_b1=hy_filt_b1,
              f_w2=hy_filt_w2, f_b2=hy_filt_b2, f_w3=hy_filt_w3, f_b3=hy_filt_b3, f_freq=hy_filt_freq, skip=hy_skip,
              w_out=hy_w_out.astype(bf16))
    at = dict(w_qkv=at_w_qkv.astype(bf16), q_gain=at_q_gain, k_gain=at_k_gain, w_out=at_w_out.astype(bf16))
    ln = dict(g1=ln1_g, b1=ln1_b, g2=ln2_g, b2=ln2_b)
    ffn = dict(w_in=ffn_w_in.astype(bf16), conv_w=ffn_conv_w, conv_b=ffn_conv_b, w_out=ffn_w_out.astype(bf16))
    y_prompt = _run_trunk(x_prompt, Cfg(x_prompt.shape[1]), meta_tokens, hy, at, ln, ffn)
    y_sample = _run_trunk(x_sample, Cfg(x_sample.shape[1]), meta_tokens, hy, at, ln, ffn)
    return (y_prompt, y_sample)
```

```python
import functools
import math

import jax
import jax.numpy as jnp
from jax import lax
from jax.experimental import pallas as pl
from jax.experimental.pallas import tpu as pltpu

f32 = jnp.float32
bf16 = jnp.bfloat16

D_MODEL = 1024
DEPTH = 4
N_META = 16
GRID_W = 64
HEAD_DIM = 128
N_HEADS = 8
N_KV_HEADS = 2
GROUP = N_HEADS // N_KV_HEADS
ROPE_THETA = 10000.0
FILTER_EMB_DIM = 33
FILTER_EMB_PAD = 40
FILTER_BANDS = 16
FILTER_HIDDEN = 64
DECAY_TARGET = 1e-2
FAST_DECAY_PCT = 0.3
SLOW_DECAY_PCT = 1.5
DECAY_SHIFT = 0.05
FILTER_NORM_EPS = 1e-6
D_FF = 2816
DEEPNORM_ALPHA = (2 * DEPTH) ** 0.25
LN_EPS = 1e-5
RMS_EPS = 1e-6
NEG_BIG = -1e30
LOG2E = 1.4426950408889634

FFT_N2 = 128
V7X_VMEM_BYTES = 64 * 1024 * 1024
VMEM_LIMIT = 52 * 1024 * 1024


class Cfg:
    def __init__(self, n):
        self.L = n + N_META
        if n == 8192:
            self.Lp, self.TL, self.TM, self.TQ, self.N1, self.NIN = 8320, 640, 1280, 128, 136, 80
        elif n == 2048:
            self.Lp, self.TL, self.TM, self.TQ, self.N1, self.NIN = 2304, 768, 1024, 256, 40, 32
        else:
            up = lambda a, m: -(-a // m) * m
            self.TL = self.TM = self.TQ = 128
            self.Lp = up(self.L, 128)
            self.N1 = up(-(-(2 * self.L - 1) // FFT_N2), 8)
            self.NIN = up(-(-self.Lp // FFT_N2), 16)
        self.NR = self.NIN * FFT_N2
        self.N1H = -(-(self.N1 // 2 + 1) // 8) * 8
        self.N = self.N1 * FFT_N2
        assert self.N >= 2 * self.L - 1 and self.NR >= self.Lp
        assert self.Lp % self.TL == 0 and self.Lp % self.TQ == 0


def _cparams(sem, vmem=None):
    return pltpu.CompilerParams(dimension_semantics=sem, vmem_limit_bytes=vmem)


def _mm_kernel(x_ref, w_ref, o_ref):
    o_ref[...] = jnp.dot(x_ref[...], w_ref[...], preferred_element_type=f32).astype(o_ref.dtype)


def matmul(x, w, out_dtype, tm, tn):
    R, K = x.shape
    N = w.shape[1]
    assert R % tm == 0 and N % tn == 0
    return pl.pallas_call(
        _mm_kernel,
        grid=(N // tn, R // tm),
        in_specs=[pl.BlockSpec((tm, K), lambda j, i: (i, 0)), pl.BlockSpec((K, tn), lambda j, i: (0, j))],
        out_specs=pl.BlockSpec((tm, tn), lambda j, i: (i, j)),
        out_shape=jax.ShapeDtypeStruct((R, N), out_dtype),
        compiler_params=_cparams(("arbitrary", "arbitrary"), VMEM_LIMIT),
        name="mm",
    )(x, w)


def _mm_ln_kernel(x_ref, w_ref, h_ref, g_ref, b_ref, o_ref, obf_ref):
    m = jnp.dot(x_ref[...], w_ref[...], preferred_element_type=f32)
    y = DEEPNORM_ALPHA * h_ref[...] + m
    mu = jnp.mean(y, axis=-1, keepdims=True)
    yc = y - mu
    var = jnp.mean(yc * yc, axis=-1, keepdims=True)
    out = yc * lax.rsqrt(var + LN_EPS) * g_ref[...] + b_ref[...]
    o_ref[...] = out
    obf_ref[...] = out.astype(bf16)


def matmul_residual_ln(x, w, h, g, b, tm):
    R, K = x.shape
    N = w.shape[1]
    row = lambda i: (i, 0)
    fixed = lambda i: (0, 0)
    return pl.pallas_call(
        _mm_ln_kernel,
        grid=(R // tm,),
        in_specs=[pl.BlockSpec((tm, K), row), pl.BlockSpec((K, N), fixed), pl.BlockSpec((tm, N), row),
                  pl.BlockSpec((1, N), fixed), pl.BlockSpec((1, N), fixed)],
        out_specs=[pl.BlockSpec((tm, N), row), pl.BlockSpec((tm, N), row)],
        out_shape=[jax.ShapeDtypeStruct((R, N), f32), jax.ShapeDtypeStruct((R, N), bf16)],
        compiler_params=_cparams(("arbitrary",), VMEM_LIMIT),
        name="mm_ln",
    )(x, w, h, g.reshape(1, N), b.reshape(1, N))


def _left_mm_kernel(a_ref, x_ref, o_ref, *, cw):
    a = a_ref[...]
    for c in range(0, x_ref.shape[1], cw):
        o_ref[:, c:c + cw] = jnp.dot(a, x_ref[:, c:c + cw], preferred_element_type=f32).astype(o_ref.dtype)


def left_matmul(a, x, out_dtype, tn=32768, cw=512):
    M, K = a.shape
    B, _, C = x.shape
    assert C % tn == 0 and tn % cw == 0
    return pl.pallas_call(
        functools.partial(_left_mm_kernel, cw=cw),
        grid=(B, C // tn),
        in_specs=[pl.BlockSpec((M, K), lambda b, j: (0, 0)), pl.BlockSpec((None, K, tn), lambda b, j: (b, 0, j))],
        out_specs=pl.BlockSpec((None, M, tn), lambda b, j: (b, 0, j)),
        out_shape=jax.ShapeDtypeStruct((B, M, C), out_dtype),
        compiler_params=_cparams(("arbitrary", "arbitrary"), VMEM_LIMIT),
        name="left_mm",
    )(a, x)


def _conv3_rows(ref, r, ch, Lp, L, w, b):
    c = ref[pl.ds(r, ch), :].astype(f32)
    rows = r + lax.broadcasted_iota(jnp.int32, (ch, 1), 0)
    if r == 0:
        p = jnp.where(rows >= 1, pltpu.roll(c, 1, axis=0), 0.0)
    else:
        p = ref[pl.ds(r - 1, ch), :].astype(f32)
    if r + ch == Lp:
        n = pltpu.roll(c, ch - 1, axis=0)
    else:
        n = ref[pl.ds(r + 1, ch), :].astype(f32)
    n = jnp.where(rows + 1 < L, n, 0.0)
    return p * w[0:1, :] + c * w[1:2, :] + n * w[2:3, :] + b, rows


def _hyena_gate_kernel(x1_ref, v_ref, w1_ref, b1_ref, wv_ref, bv_ref, o_ref, *, L, Lp, ch):
    w1, b1, wv, bv = w1_ref[...], b1_ref[...], wv_ref[...], bv_ref[...]
    for r in range(0, Lp, ch):
        cx, rows = _conv3_rows(x1_ref, r, ch, Lp, L, w1, b1)
        cv, _ = _conv3_rows(v_ref, r, ch, Lp, L, wv, bv)
        o_ref[pl.ds(r, ch), :] = jnp.where(rows < L, cx * cv, 0.0).astype(o_ref.dtype)
    NR = o_ref.shape[0]
    if NR > Lp:
        o_ref[pl.ds(Lp, NR - Lp), :] = jnp.zeros((NR - Lp, o_ref.shape[1]), o_ref.dtype)


def hyena_gate(u, conv_w, conv_b, cfg, tc=256):
    B = u.shape[0]
    D = D_MODEL
    nb = D // tc
    sec = lambda s: (lambda b, c: (b, 0, s * nb + c))
    wsec = lambda s: (lambda b, c: (0, s * nb + c))
    return pl.pallas_call(
        functools.partial(_hyena_gate_kernel, L=cfg.L, Lp=cfg.Lp, ch=cfg.TL),
        grid=(B, nb),
        in_specs=[pl.BlockSpec((None, cfg.Lp, tc), sec(1)), pl.BlockSpec((None, cfg.Lp, tc), sec(2)),
                  pl.BlockSpec((3, tc), wsec(1)), pl.BlockSpec((1, tc), wsec(1)),
                  pl.BlockSpec((3, tc), wsec(2)), pl.BlockSpec((1, tc), wsec(2))],
        out_specs=pl.BlockSpec((None, cfg.NR, tc), lambda b, c: (b, 0, c)),
        out_shape=jax.ShapeDtypeStruct((B, cfg.NR, D), bf16),
        compiler_params=_cparams(("arbitrary", "arbitrary"), VMEM_LIMIT),
        name="hyena_gate",
    )(u, u, conv_w, conv_b.reshape(1, -1), conv_w, conv_b.reshape(1, -1))


def _hyena_post_kernel(y_ref, vg_ref, x0_ref, w0_ref, b0_ref, skip_ref, o_ref, *, L, Lp, ch):
    w0, b0, skip = w0_ref[...], b0_ref[...], skip_ref[...]
    for r in range(0, Lp, ch):
        cx, _ = _conv3_rows(x0_ref, r, ch, Lp, L, w0, b0)
        y = y_ref[pl.ds(r, ch), :].astype(f32) + vg_ref[pl.ds(r, ch), :].astype(f32) * skip
        o_ref[pl.ds(r, ch), :] = (y * cx).astype(o_ref.dtype)


def hyena_post(yc, vg, u, conv_w, conv_b, skip, cfg, tc=256):
    B = u.shape[0]
    D = D_MODEL
    col = lambda b, c: (b, 0, c)
    return pl.pallas_call(
        functools.partial(_hyena_post_kernel, L=cfg.L, Lp=cfg.Lp, ch=cfg.TL),
        grid=(B, D // tc),
        in_specs=[pl.BlockSpec((None, cfg.Lp, tc), col), pl.BlockSpec((None, cfg.Lp, tc), col),
                  pl.BlockSpec((None, cfg.Lp, tc), col),
                  pl.BlockSpec((3, tc), lambda b, c: (0, c)), pl.BlockSpec((1, tc), lambda b, c: (0, c)),
                  pl.BlockSpec((1, tc), lambda b, c: (0, c))],
        out_specs=pl.BlockSpec((None, cfg.Lp, tc), col),
        out_shape=jax.ShapeDtypeStruct((B, cfg.Lp, D), bf16),
        compiler_params=_cparams(("arbitrary", "arbitrary"), VMEM_LIMIT),
        name="hyena_post",
    )(yc, vg, u, conv_w, conv_b.reshape(1, -1), skip.reshape(1, -1))


HALO = 16


def _ffn_in_gate_kernel(x_ref, xp_ref, xn_ref, wg_ref, wa_ref, cw_ref, cb_ref, o_ref, x_scr, g_scr, *, L, tm, tpb):
    i, c = pl.program_id(0), pl.program_id(1)

    @pl.when(c == 0)
    def _():
        x_scr[0:HALO, :] = xp_ref[...]
        x_scr[HALO:HALO + tm, :] = x_ref[...]
        x_scr[HALO + tm:, :] = xn_ref[...]

    g_scr[...] = jnp.dot(x_scr[...], wg_ref[...], preferred_element_type=f32)
    a = jnp.dot(x_ref[...], wa_ref[...], preferred_element_type=f32)
    t = (i % tpb) * tm + lax.broadcasted_iota(jnp.int32, (tm, 1), 0)
    cw = cw_ref[...]
    prev = jnp.where(t >= 1, g_scr[HALO - 1:HALO - 1 + tm, :], 0.0)
    nxt = jnp.where(t + 1 < L, g_scr[HALO + 1:HALO + 1 + tm, :], 0.0)
    g = prev * cw[0:1, :] + g_scr[HALO:HALO + tm, :] * cw[1:2, :] + nxt * cw[2:3, :] + cb_ref[...]
    gelu = 0.5 * g * (1.0 + lax.erf(g * (2.0 ** -0.5)))
    o_ref[...] = (gelu * a).astype(o_ref.dtype)


def ffn_in_gate(x, w_in, conv_w, conv_b, cfg, nchunk=1):
    R, K = x.shape
    tm = cfg.TL
    tc = D_FF // nchunk
    tpb = cfg.Lp // tm
    nh = tm // HALO
    last = R // HALO - 1
    assert D_FF % nchunk == 0 and tc % 128 == 0 and tm % HALO == 0
    wmode = dict(pipeline_mode=pl.Buffered(1)) if nchunk == 1 else {}
    return pl.pallas_call(
        functools.partial(_ffn_in_gate_kernel, L=cfg.L, tm=tm, tpb=tpb),
        grid=(R // tm, nchunk),
        in_specs=[pl.BlockSpec((tm, K), lambda i, c: (i, 0)),
                  pl.BlockSpec((HALO, K), lambda i, c: (jnp.maximum(i * nh - 1, 0), 0)),
                  pl.BlockSpec((HALO, K), lambda i, c: (jnp.minimum((i + 1) * nh, last), 0)),
                  pl.BlockSpec((K, tc), lambda i, c: (0, c), **wmode),
                  pl.BlockSpec((K, tc), lambda i, c: (0, nchunk + c), **wmode),
                  pl.BlockSpec((3, tc), lambda i, c: (0, c)), pl.BlockSpec((1, tc), lambda i, c: (0, c))],
        out_specs=pl.BlockSpec((tm, tc), lambda i, c: (i, c)),
        out_shape=jax.ShapeDtypeStruct((R, D_FF), bf16),
        scratch_shapes=[pltpu.VMEM((tm + 2 * HALO, K), bf16), pltpu.VMEM((tm + 2 * HALO, tc), f32)],
        compiler_params=_cparams(("arbitrary", "arbitrary"), VMEM_LIMIT),
        name="ffn_in_gate",
    )(x, x, x, w_in, w_in, conv_w, conv_b.reshape(1, -1))


def _dft_tables(cfg):
    N1, NIN, N2, N = cfg.N1, cfg.NIN, FFT_N2, cfg.N
    assert N1 % 2 == 0
    k1 = jnp.arange(cfg.N1H, dtype=jnp.int32)
    wk = jnp.where((k1 == 0) | (k1 == N1 // 2), 1.0, jnp.where(k1 < N1 // 2, 2.0, 0.0)).astype(f32)
    n1 = jnp.arange(NIN, dtype=jnp.int32)
    ang = (2.0 * math.pi / N1) * ((k1[:, None] * n1[None, :]) % N1).astype(f32)
    ca, sa = jnp.cos(ang), jnp.sin(ang)
    fa = jnp.concatenate([ca, -sa], axis=0)
    fa_inv = jnp.concatenate([ca.T * wk, -sa.T * wk], axis=1) * (1.0 / N)
    k2 = jnp.arange(N2, dtype=jnp.int32)[:, None]
    n2 = jnp.arange(N2, dtype=jnp.int32)[None, :]
    t1 = (2.0 * math.pi / N2) * ((n2 * k2) % N2).astype(f32)
    t2 = (2.0 * math.pi / N) * ((n2 * k1[:, None]) % N).astype(f32)
    c1, s1 = jnp.cos(t1)[None], jnp.sin(t1)[None]
    c2, s2 = jnp.cos(t2)[:, None, :], jnp.sin(t2)[:, None, :]
    ar, ai = c1 * c2 - s1 * s2, -(s1 * c2 + c1 * s2)
    g_fwd = jnp.concatenate([jnp.concatenate([ar, -ai], axis=2), jnp.concatenate([ai, ar], axis=2)], axis=1)
    art, ait = jnp.swapaxes(ar, 1, 2), -jnp.swapaxes(ai, 1, 2)
    g_inv = jnp.concatenate([jnp.concatenate([art, -ait], axis=2), jnp.concatenate([ait, art], axis=2)], axis=1)
    return fa.astype(bf16), fa_inv.astype(bf16), g_fwd.astype(bf16), g_inv.astype(bf16)


def _fft_b_fused_kernel(z_ref, h_ref, gf_ref, gi_ref, o_ref):
    zc = z_ref[...].reshape(2 * FFT_N2, z_ref.shape[-1])
    y = jnp.dot(gf_ref[...], zc, preferred_element_type=f32)
    yr, yi = y[:FFT_N2], y[FFT_N2:]
    hr, hi = h_ref[0], h_ref[1]
    pc = jnp.concatenate([yr * hr - yi * hi, yr * hi + yi * hr], axis=0).astype(bf16)
    q = jnp.dot(gi_ref[...], pc, preferred_element_type=f32)
    o_ref[0] = q[:FFT_N2].astype(o_ref.dtype)
    o_ref[1] = q[FFT_N2:].astype(o_ref.dtype)


def fft_b_fused(z5, hspec, g_fwd, g_inv):
    B, _, N1, N2, D = z5.shape
    zspec = pl.BlockSpec((None, 2, None, N2, D), lambda k, b: (b, 0, k, 0, 0))
    gspec = pl.BlockSpec((None, 2 * N2, 2 * N2), lambda k, b: (k, 0, 0))
    return pl.pallas_call(
        _fft_b_fused_kernel,
        grid=(N1, B),
        in_specs=[zspec, pl.BlockSpec((2, None, N2, D), lambda k, b: (0, k, 0, 0)), gspec, gspec],
        out_specs=zspec,
        out_shape=jax.ShapeDtypeStruct(z5.shape, bf16),
        compiler_params=_cparams(("arbitrary", "arbitrary"), VMEM_LIMIT),
        name="fft_b_fused",
    )(z5, hspec, g_fwd, g_inv)


def _fft_b_filter_kernel(z_ref, csum_ref, gf_ref, o_ref):
    D = D_MODEL
    zc = z_ref[...].reshape(2 * FFT_N2, 2 * D)
    y = jnp.dot(gf_ref[...], zc, preferred_element_type=f32)
    scale = 1.0 / (csum_ref[...] + FILTER_NORM_EPS)
    y = y * scale
    yr, yi = y[:FFT_N2], y[FFT_N2:]
    o_ref[0] = yr[:, :D] + yr[:, D:]
    o_ref[1] = yi[:, :D] - yi[:, D:]


def fft_b_filter(z5, csum, g_fwd):
    _, N1, N2, D2 = z5.shape
    return pl.pallas_call(
        _fft_b_filter_kernel,
        grid=(N1,),
        in_specs=[pl.BlockSpec((2, None, N2, D2), lambda k: (0, k, 0, 0)), pl.BlockSpec((1, D2), lambda k: (0, 0)),
                  pl.BlockSpec((None, 2 * N2, 2 * N2), lambda k: (k, 0, 0))],
        out_specs=pl.BlockSpec((2, None, N2, D2 // 2), lambda k: (0, k, 0, 0)),
        out_shape=jax.ShapeDtypeStruct((2, N1, N2, D2 // 2), f32),
        compiler_params=_cparams(("arbitrary",), VMEM_LIMIT),
        name="fft_b_filter",
    )(z5, csum, g_fwd)


def _filter_mlp_kernel(z_ref, w1_ref, b1_ref, w2_ref, b2_ref, w3_ref, b3_ref, fr_ref, dl_ref, o_ref, cs_ref, *, L, tf):
    i = pl.program_id(0)
    hp = lax.Precision.HIGHEST
    z = z_ref[...]
    fr = fr_ref[...]
    h = jnp.sin(fr * (jnp.dot(z, w1_ref[...], preferred_element_type=f32, precision=hp) + b1_ref[...]))
    h = jnp.sin(fr * (jnp.dot(h, w2_ref[...], preferred_element_type=f32, precision=hp) + b2_ref[...]))
    h = jnp.dot(h, w3_ref[...], preferred_element_type=f32, precision=hp) + b3_ref[...]
    win = jnp.exp(-z[:, 0:1] * dl_ref[...]) + DECAY_SHIFT
    h = h * jnp.concatenate([win, win], axis=1)
    rows = i * tf + lax.broadcasted_iota(jnp.int32, (tf, 1), 0)
    h = jnp.where(rows < L, h, 0.0)

    @pl.when(i == 0)
    def _():
        cs_ref[...] = jnp.zeros_like(cs_ref)

    cs_ref[...] += jnp.sum(jnp.abs(h), axis=0, keepdims=True)
    lanes = lax.broadcasted_iota(jnp.int32, (1, h.shape[1]), 1)
    o_ref[...] = jnp.where((rows == 0) & (lanes >= D_MODEL), 0.0, h).astype(o_ref.dtype)


def filter_mlp(z, w1, b1, w2, b2, w3, b3, freq, deltas, cfg, tf=512):
    NR = cfg.NR
    D2 = 2 * D_MODEL
    fixed = lambda i: (0, 0)
    full = lambda a: pl.BlockSpec(a.shape, fixed)
    args = (w1, b1.reshape(1, -1), w2, b2.reshape(1, -1), w3, b3.reshape(1, -1), freq.reshape(1, -1), deltas.reshape(1, -1))
    return pl.pallas_call(
        functools.partial(_filter_mlp_kernel, L=cfg.L, tf=tf),
        grid=(NR // tf,),
        in_specs=[pl.BlockSpec((tf, FILTER_EMB_PAD), lambda i: (i, 0))] + [full(a) for a in args],
        out_specs=[pl.BlockSpec((tf, D2), lambda i: (i, 0)), pl.BlockSpec((1, D2), fixed)],
        out_shape=[jax.ShapeDtypeStruct((NR, D2), bf16), jax.ShapeDtypeStruct((1, D2), f32)],
        compiler_params=_cparams(("arbitrary",), VMEM_LIMIT),
        name="filter_mlp",
    )(z, *args)


def _filter_features(cfg):
    L = cfg.L
    pos = jnp.arange(cfg.NR, dtype=f32)
    t = (pos / (L - 1))[:, None]
    w = (2.0 * math.pi / L) * pos[:, None]
    bands = jnp.linspace(1e-4, FILTER_BANDS - 1, FILTER_BANDS, dtype=f32)
    z = jnp.concatenate([t, jnp.cos(w * bands), -jnp.sin(w * bands)], axis=-1)
    return jnp.pad(z, ((0, 0), (0, FILTER_EMB_PAD - FILTER_EMB_DIM)))


def hyena_filter_spectrum(cfg, tables, w1, b1, w2, b2, w3, b3, freq):
    fa, _, g_fwd, _ = tables
    z = _filter_features(cfg)
    w1p = jnp.pad(w1, ((0, FILTER_EMB_PAD - FILTER_EMB_DIM), (0, 0)))
    max_decay = math.log(1.0 / DECAY_TARGET) / FAST_DECAY_PCT
    min_decay = math.log(1.0 / DECAY_TARGET) / SLOW_DECAY_PCT
    deltas = jnp.linspace(min_decay, max_decay, D_MODEL, dtype=f32)
    xf, csum = filter_mlp(z, w1p, b1, w2, b2, w3, b3, freq, deltas, cfg)
    D2 = 2 * D_MODEL
    z2 = left_matmul(fa, xf.reshape(1, cfg.NIN, FFT_N2 * D2), bf16)
    z5 = z2.reshape(2, cfg.N1H, FFT_N2, D2)
    return fft_b_filter(z5, csum, g_fwd)


def long_conv(vg, hspec, tables, cfg):
    fa, fa_inv, g_fwd, g_inv = tables
    B = vg.shape[0]
    D = D_MODEL
    z2 = left_matmul(fa, vg.reshape(B, cfg.NIN, FFT_N2 * D), bf16)
    q5 = fft_b_fused(z2.reshape(B, 2, cfg.N1H, FFT_N2, D), hspec, g_fwd, g_inv)
    y2 = left_matmul(fa_inv, q5.reshape(B, 2 * cfg.N1H, FFT_N2 * D), bf16)
    return y2.reshape(B, cfg.NR, D)


def _qkv_prep_kernel(x_ref, cc_ref, ss_ref, qg_ref, kg_ref, q_ref, k_ref, v_ref, *, L, tl):
    rows = pl.program_id(1) * tl + lax.broadcasted_iota(jnp.int32, (tl, 1), 0)
    valid = rows < L
    cc, ss = cc_ref[...], ss_ref[...]

    def norm_rope(x, gain, scale):
        x = x.astype(f32)
        xn = x * lax.rsqrt(jnp.mean(x * x, axis=-1, keepdims=True) + RMS_EPS) * gain
        y = xn * cc + pltpu.roll(xn, HEAD_DIM // 2, axis=1) * ss
        return jnp.where(valid, y * scale, 0.0).astype(bf16)

    nq = N_HEADS * HEAD_DIM
    for hh in range(N_HEADS):
        sl = slice(hh * HEAD_DIM, (hh + 1) * HEAD_DIM)
        q_ref[:, sl] = norm_rope(x_ref[:, sl], qg_ref[...], LOG2E * HEAD_DIM ** -0.5)
    for hh in range(N_KV_HEADS):
        sl = slice(hh * HEAD_DIM, (hh + 1) * HEAD_DIM)
        k_ref[:, sl] = norm_rope(x_ref[:, nq + hh * HEAD_DIM: nq + (hh + 1) * HEAD_DIM], kg_ref[...], 1.0)
    nk = N_KV_HEADS * HEAD_DIM
    v_ref[...] = jnp.where(valid, x_ref[:, nq + nk:], jnp.zeros((), x_ref.dtype))


def qkv_prep(qkv, cc, ss, q_gain, k_gain, cfg):
    B = qkv.shape[0]
    tl = cfg.TL
    nq, nk = N_HEADS * HEAD_DIM, N_KV_HEADS * HEAD_DIM
    row = lambda b, i: (b, i, 0)
    tab = lambda b, i: (i, 0)
    fixed = lambda b, i: (0, 0)
    return pl.pallas_call(
        functools.partial(_qkv_prep_kernel, L=cfg.L, tl=tl),
        grid=(B, cfg.Lp // tl),
        in_specs=[pl.BlockSpec((None, tl, nq + 2 * nk), row), pl.BlockSpec((tl, HEAD_DIM), tab),
                  pl.BlockSpec((tl, HEAD_DIM), tab), pl.BlockSpec((1, HEAD_DIM), fixed), pl.BlockSpec((1, HEAD_DIM), fixed)],
        out_specs=[pl.BlockSpec((None, tl, nq), row), pl.BlockSpec((None, tl, nk), row), pl.BlockSpec((None, tl, nk), row)],
        out_shape=[jax.ShapeDtypeStruct((B, cfg.Lp, nq), bf16), jax.ShapeDtypeStruct((B, cfg.Lp, nk), bf16),
                   jax.ShapeDtypeStruct((B, cfg.Lp, nk), bf16)],
        compiler_params=_cparams(("arbitrary", "arbitrary"), VMEM_LIMIT),
        name="qkv_prep",
    )(qkv, cc, ss, q_gain.reshape(1, -1), k_gain.reshape(1, -1))


def _flash_kernel(q_ref, k_ref, v_ref, o_ref, s0_scr, s1_scr, p0_scr, p1_scr, *, L):
    s_bufs, p_bufs = (s0_scr, s1_scr), (p0_scr, p1_scr)
    lp = k_ref.shape[0]
    c0 = (L // 128) * 128
    for g in range(GROUP):
        s_scr, p_scr = s_bufs[g % 2], p_bufs[g % 2]
        q = q_ref[:, g * HEAD_DIM:(g + 1) * HEAD_DIM]
        s = lax.dot_general(q, k_ref[...], (((1,), (1,)), ((), ())), preferred_element_type=f32)
        if c0 < lp:
            cols = c0 + lax.broadcasted_iota(jnp.int32, (1, lp - c0), 1)
            s_scr[:, :c0] = s[:, :c0]
            s_scr[:, c0:] = jnp.where(cols < L, s[:, c0:], NEG_BIG)
        else:
            s_scr[...] = s
        s = s_scr[...]
        p = jnp.exp2(s - jnp.max(s, axis=-1, keepdims=True))
        l = jnp.sum(p, axis=-1, keepdims=True)
        p_scr[...] = p.astype(bf16)
        o = jnp.dot(p_scr[...], v_ref[...], preferred_element_type=f32)
        o_ref[:, g * HEAD_DIM:(g + 1) * HEAD_DIM] = (o / l).astype(o_ref.dtype)


def flash_attention(q, k, v, cfg):
    B = q.shape[0]
    tq, lp = cfg.TQ, cfg.Lp
    gw = GROUP * HEAD_DIM
    return pl.pallas_call(
        functools.partial(_flash_kernel, L=cfg.L),
        grid=(B, N_KV_HEADS, lp // tq),
        in_specs=[pl.BlockSpec((None, tq, gw), lambda b, h, i: (b, i, h)),
                  pl.BlockSpec((None, lp, HEAD_DIM), lambda b, h, i: (b, 0, h)),
                  pl.BlockSpec((None, lp, HEAD_DIM), lambda b, h, i: (b, 0, h))],
        out_specs=pl.BlockSpec((None, tq, gw), lambda b, h, i: (b, i, h)),
        out_shape=jax.ShapeDtypeStruct(q.shape, bf16),
        scratch_shapes=[pltpu.VMEM((tq, lp), f32), pltpu.VMEM((tq, lp), f32),
                        pltpu.VMEM((tq, lp), bf16), pltpu.VMEM((tq, lp), bf16)],
        compiler_params=_cparams(("arbitrary", "arbitrary", "arbitrary"), VMEM_LIMIT),
        name="flash",
    )(q, k, v)


def _rope_tables(cfg):
    n = cfg.L - N_META
    rows = n // GRID_W
    row = jnp.concatenate([jnp.full((N_META,), -1.0, f32), jnp.repeat(jnp.arange(rows, dtype=f32), GRID_W)])
    col = jnp.concatenate([jnp.arange(N_META, dtype=f32), jnp.tile(jnp.arange(GRID_W, dtype=f32), rows)])
    axis_rot = HEAD_DIM // 2
    inv_freq = ROPE_THETA ** (-jnp.arange(0, axis_rot, 2, dtype=f32) / axis_rot)
    ang = jnp.concatenate([row[:, None] * inv_freq, col[:, None] * inv_freq], axis=-1)
    ang = jnp.pad(ang, ((0, cfg.Lp - cfg.L), (0, 0)))
    c, s = jnp.cos(ang), jnp.sin(ang)
    return jnp.concatenate([c, c], axis=-1), jnp.concatenate([-s, s], axis=-1)


def _run_trunk(x, cfg, meta_tokens, hy, at, ln, ffn):
    B, n, D = x.shape
    Lp, tm = cfg.Lp, cfg.TM
    R = B * Lp
    meta = jnp.broadcast_to(meta_tokens[None], (B, N_META, D))
    h = jnp.concatenate([meta, x, jnp.zeros((B, Lp - cfg.L, D), x.dtype)], axis=1).reshape(R, D)
    hb = h.astype(bf16)
    cc, ss = _rope_tables(cfg)
    tables = _dft_tables(cfg)
    for i in range(DEPTH):
        j = i // 2
        if i % 2 == 0:
            hspec = hyena_filter_spectrum(cfg, tables, hy["f_w1"][j], hy["f_b1"][j], hy["f_w2"][j], hy["f_b2"][j],
                                          hy["f_w3"][j], hy["f_b3"][j], hy["f_freq"][j])
            u = matmul(hb, hy["w_in"][j], bf16, tm, 1024).reshape(B, Lp, 3 * D)
            vg = hyena_gate(u, hy["conv_w"][j], hy["conv_b"][j], cfg)
            yc = long_conv(vg, hspec, tables, cfg)
            y = hyena_post(yc, vg, u, hy["conv_w"][j], hy["conv_b"][j], hy["skip"][j], cfg)
            h, hb = matmul_residual_ln(y.reshape(R, D), hy["w_out"][j], h, ln["g1"][i], ln["b1"][i], cfg.TL)
        else:
            qkv = matmul(hb, at["w_qkv"][j], bf16, tm, 512).reshape(B, Lp, -1)
            q, k, v = qkv_prep(qkv, cc, ss, at["q_gain"][j], at["k_gain"][j], cfg)
            o = flash_attention(q, k, v, cfg)
            h, hb = matmul_residual_ln(o.reshape(R, D), at["w_out"][j], h, ln["g1"][i], ln["b1"][i], cfg.TL)
        gated = ffn_in_gate(hb, ffn["w_in"][i], ffn["conv_w"][i], ffn["conv_b"][i], cfg)
        h, hb = matmul_residual_ln(gated, ffn["w_out"][i], h, ln["g2"][i], ln["b2"][i], cfg.TL)
    return h.reshape(B, Lp, D)[:, N_META:cfg.L]


def kernel(x_prompt, x_sample, meta_tokens, hy_w_in, hy_conv_w, hy_conv_b, hy_filt_w1, hy_filt_b1, hy_filt_w2, hy_filt_b2, hy_filt_w3, hy_filt_b3, hy_filt_freq, hy_skip, hy_w_out, at_w_qkv, at_q_gain, at_k_gain, at_w_out, ln1_g, ln1_b, ln2_g, ln2_b, ffn_w_in, ffn_conv_w, ffn_conv_b, ffn_w_out):
    hy = dict(w_in=hy_w_in.astype(bf16), conv_w=hy_conv_w, conv_b=hy_conv_b, f_w1=hy_filt_w1, f_b1=hy_filt_b1,
              f_w2=hy_filt_w2, f_b2=hy_filt_b2, f_w3=hy_filt_w3, f_b3=hy_filt_b3, f_freq=hy_filt_freq, skip=hy_skip,
              w_out=hy_w_out.astype(bf16))
    at = dict(w_qkv=at_w_qkv.astype(bf16), q_gain=at_q_gain, k_gain=at_k_gain, w_out=at_w_out.astype(bf16))
    ln = dict(g1=ln1_g, b1=ln1_b, g2=ln2_g, b2=ln2_b)
    ffn = dict(w_in=ffn_w_in.astype(bf16), conv_w=ffn_conv_w, conv_b=ffn_conv_b, w_out=ffn_w_out.astype(bf16))
    y_prompt = _run_trunk(x_prompt, Cfg(x_prompt.shape[1]), meta_tokens, hy, at, ln, ffn)
    y_sample = _run_trunk(x_sample, Cfg(x_sample.shape[1]), meta_tokens, hy, at, ln, ffn)
    return (y_prompt, y_sample)
```

```python
import functools
import math

import jax
import jax.numpy as jnp
from jax import lax
from jax.experimental import pallas as pl
from jax.experimental.pallas import tpu as pltpu

f32 = jnp.float32
bf16 = jnp.bfloat16

D_MODEL = 1024
DEPTH = 4
N_META = 16
GRID_W = 64
HEAD_DIM = 128
N_HEADS = 8
N_KV_HEADS = 2
GROUP = N_HEADS // N_KV_HEADS
ROPE_THETA = 10000.0
FILTER_EMB_DIM = 33
FILTER_EMB_PAD = 40
FILTER_BANDS = 16
FILTER_HIDDEN = 64
DECAY_TARGET = 1e-2
FAST_DECAY_PCT = 0.3
SLOW_DECAY_PCT = 1.5
DECAY_SHIFT = 0.05
FILTER_NORM_EPS = 1e-6
D_FF = 2816
DEEPNORM_ALPHA = (2 * DEPTH) ** 0.25
LN_EPS = 1e-5
RMS_EPS = 1e-6
NEG_BIG = -1e30
LOG2E = 1.4426950408889634

LANES = 128
FFT_N2 = 128
V7X_VMEM_BYTES = 64 * 1024 * 1024
VMEM_LIMIT = 52 * 1024 * 1024


class Cfg:
    def __init__(self, n):
        self.L = n + N_META
        if n == 8192:
            self.Lp, self.TL, self.TQ, self.N1, self.NIN = 8320, 640, 128, 136, 80
        elif n == 2048:
            self.Lp, self.TL, self.TQ, self.N1, self.NIN = 2304, 768, 256, 40, 48
        else:
            up = lambda a, m: -(-a // m) * m
            self.TL = self.TQ = 128
            self.Lp = up(self.L, 128)
            self.N1 = up(-(-(2 * self.L - 1) // FFT_N2), 8)
            self.NIN = up(-(-self.Lp // FFT_N2), 16)
        self.NR = self.NIN * FFT_N2
        self.N1H = -(-(self.N1 // 2 + 1) // 8) * 8
        self.N = self.N1 * FFT_N2
        assert self.N >= 2 * self.L - 1 and self.NR >= self.Lp
        assert self.Lp % self.TL == 0 and self.Lp % self.TQ == 0 and self.NR % self.TL == 0


def _cparams(sem, vmem=None):
    return pltpu.CompilerParams(dimension_semantics=sem, vmem_limit_bytes=vmem)


def _residual_ln(m, h_ref, g_ref, b_ref, o_ref, obf_ref):
    y = DEEPNORM_ALPHA * h_ref[...] + m
    mu = jnp.mean(y, axis=-1, keepdims=True)
    yc = y - mu
    var = jnp.mean(yc * yc, axis=-1, keepdims=True)
    out = yc * lax.rsqrt(var + LN_EPS) * g_ref[...] + b_ref[...]
    o_ref[...] = out
    obf_ref[...] = out.astype(bf16)


def _mm_ln_kernel(x_ref, w_ref, h_ref, g_ref, b_ref, o_ref, obf_ref):
    _residual_ln(jnp.dot(x_ref[...], w_ref[...], preferred_element_type=f32), h_ref, g_ref, b_ref, o_ref, obf_ref)


def matmul_residual_ln(x, w, h, g, b, tm):
    R, K = x.shape
    N = w.shape[1]
    row = lambda i: (i, 0)
    fixed = lambda i: (0, 0)
    return pl.pallas_call(
        _mm_ln_kernel,
        grid=(R // tm,),
        in_specs=[pl.BlockSpec((tm, K), row), pl.BlockSpec((K, N), fixed), pl.BlockSpec((tm, N), row),
                  pl.BlockSpec((1, N), fixed), pl.BlockSpec((1, N), fixed)],
        out_specs=[pl.BlockSpec((tm, N), row), pl.BlockSpec((tm, N), row)],
        out_shape=[jax.ShapeDtypeStruct((R, N), f32), jax.ShapeDtypeStruct((R, N), bf16)],
        compiler_params=_cparams(("arbitrary",), VMEM_LIMIT),
        name="mm_ln",
    )(x, w, h, g.reshape(1, N), b.reshape(1, N))


def _left_mm_kernel(a_ref, x_ref, o_ref, *, cw):
    a = a_ref[...]
    for c in range(0, x_ref.shape[1], cw):
        o_ref[:, c:c + cw] = jnp.dot(a, x_ref[:, c:c + cw], preferred_element_type=f32).astype(o_ref.dtype)


def left_matmul(a, x, out_dtype, tn=32768, cw=512):
    M, K = a.shape
    B, _, C = x.shape
    assert C % tn == 0 and tn % cw == 0
    return pl.pallas_call(
        functools.partial(_left_mm_kernel, cw=cw),
        grid=(B, C // tn),
        in_specs=[pl.BlockSpec((M, K), lambda b, j: (0, 0)), pl.BlockSpec((None, K, tn), lambda b, j: (b, 0, j))],
        out_specs=pl.BlockSpec((None, M, tn), lambda b, j: (b, 0, j)),
        out_shape=jax.ShapeDtypeStruct((B, M, C), out_dtype),
        compiler_params=_cparams(("arbitrary", "arbitrary"), VMEM_LIMIT),
        name="left_mm",
    )(a, x)


HALO = 16


def _halo_specs(tm, K, R, row_block):
    nh = tm // HALO
    last = R // HALO - 1
    return [pl.BlockSpec((tm, K), lambda *g: (row_block(*g), 0)),
            pl.BlockSpec((HALO, K), lambda *g: (jnp.maximum(row_block(*g) * nh - 1, 0), 0)),
            pl.BlockSpec((HALO, K), lambda *g: (jnp.minimum((row_block(*g) + 1) * nh, last), 0))]


def _fill_halo_tile(x_scr, x_ref, xp_ref, xn_ref, tm):
    x_scr[0:HALO, :] = xp_ref[...]
    x_scr[HALO:HALO + tm, :] = x_ref[...]
    x_scr[HALO + tm:, :] = xn_ref[...]


def _zero_rows_outside_sequence(u_scr, j, L, tm):
    assert L % tm != 0
    j_end, r_end = L // tm, HALO + L % tm
    g_end = (r_end // 8) * 8
    sub8 = lax.broadcasted_iota(jnp.int32, (8, 1), 0)
    u_scr[HALO - 8:HALO, :] = u_scr[HALO - 8:HALO, :] * jnp.where((j == 0) & (sub8 == 7), 0.0, 1.0)
    u_scr[g_end:g_end + 8, :] = u_scr[g_end:g_end + 8, :] * jnp.where((j == j_end) & (sub8 == r_end - g_end), 0.0, 1.0)


def _conv3_centre(u_scr, cw, cb, tm):
    return (u_scr[HALO - 1:HALO - 1 + tm, :] * cw[0:1, :] + u_scr[HALO:HALO + tm, :] * cw[1:2, :]
            + u_scr[HALO + 1:HALO + 1 + tm, :] * cw[2:3, :] + cb)


def _hyena_in_kernel(x_ref, xp_ref, xn_ref, w_ref, cw_ref, cb_ref, x0_ref, vg_ref, x_scr, u0_scr, u1_scr, uv_scr,
                     *, L, tm, tpb):
    j = pl.program_id(1)
    D = D_MODEL

    @pl.when(j < tpb)
    def _():
        _fill_halo_tile(x_scr, x_ref, xp_ref, xn_ref, tm)
        xs = x_scr[...]
        t = j * tm + lax.broadcasted_iota(jnp.int32, (tm, 1), 0)
        cw, cb = cw_ref[...], cb_ref[...]
        conv = []
        for s, scr in enumerate((u0_scr, u1_scr, uv_scr)):
            sl = slice(s * D, (s + 1) * D)
            scr[...] = jnp.dot(xs, w_ref[:, sl], preferred_element_type=f32)
            _zero_rows_outside_sequence(scr, j, L, tm)
            conv.append(_conv3_centre(scr, cw[:, sl], cb[:, sl], tm))
        x0_ref[...] = conv[0].astype(x0_ref.dtype)
        vg_ref[...] = jnp.where(t < L, conv[1] * conv[2], 0.0).astype(vg_ref.dtype)

    @pl.when(j >= tpb)
    def _():
        vg_ref[...] = jnp.zeros_like(vg_ref)


def hyena_in(x, w_in, conv_w, conv_b, cfg):
    R, K = x.shape
    D = D_MODEL
    tm, tpb, ntr = cfg.TL, cfg.Lp // cfg.TL, cfg.NR // cfg.TL
    B = R // cfg.Lp
    row_block = lambda b, j: b * tpb + jnp.minimum(j, tpb - 1)
    fixed = lambda b, j: (0, 0)
    return pl.pallas_call(
        functools.partial(_hyena_in_kernel, L=cfg.L, tm=tm, tpb=tpb),
        grid=(B, ntr),
        in_specs=_halo_specs(tm, K, R, row_block) +
                 [pl.BlockSpec((K, 3 * D), fixed, pipeline_mode=pl.Buffered(1)),
                  pl.BlockSpec((3, 3 * D), fixed), pl.BlockSpec((1, 3 * D), fixed)],
        out_specs=[pl.BlockSpec((tm, D), lambda b, j: (row_block(b, j), 0)),
                   pl.BlockSpec((tm, D), lambda b, j: (b * ntr + j, 0))],
        out_shape=[jax.ShapeDtypeStruct((R, D), bf16), jax.ShapeDtypeStruct((B * cfg.NR, D), bf16)],
        scratch_shapes=[pltpu.VMEM((tm + 2 * HALO, K), bf16)] + [pltpu.VMEM((tm + 2 * HALO, D), f32)] * 3,
        compiler_params=_cparams(("arbitrary", "arbitrary"), VMEM_LIMIT),
        name="hyena_in",
    )(x, x, x, w_in, conv_w, conv_b.reshape(1, -1))


def _hyena_out_kernel(y_ref, vg_ref, x0_ref, skip_ref, w_ref, h_ref, g_ref, b_ref, o_ref, obf_ref):
    y = y_ref[...].astype(f32) + vg_ref[...].astype(f32) * skip_ref[...]
    x = (y * x0_ref[...].astype(f32)).astype(bf16)
    _residual_ln(jnp.dot(x, w_ref[...], preferred_element_type=f32), h_ref, g_ref, b_ref, o_ref, obf_ref)


def hyena_out(yc, vg, x0, skip, w, h, g, b, cfg):
    R, N = h.shape
    tm, tpb, ntr = cfg.TL, cfg.Lp // cfg.TL, cfg.NR // cfg.TL
    row = lambda i: (i, 0)
    padded_row = lambda i: ((i // tpb) * ntr + i % tpb, 0)
    fixed = lambda i: (0, 0)
    return pl.pallas_call(
        _hyena_out_kernel,
        grid=(R // tm,),
        in_specs=[pl.BlockSpec((tm, N), padded_row), pl.BlockSpec((tm, N), padded_row), pl.BlockSpec((tm, N), row),
                  pl.BlockSpec((1, N), fixed), pl.BlockSpec((N, N), fixed), pl.BlockSpec((tm, N), row),
                  pl.BlockSpec((1, N), fixed), pl.BlockSpec((1, N), fixed)],
        out_specs=[pl.BlockSpec((tm, N), row), pl.BlockSpec((tm, N), row)],
        out_shape=[jax.ShapeDtypeStruct((R, N), f32), jax.ShapeDtypeStruct((R, N), bf16)],
        compiler_params=_cparams(("arbitrary",), VMEM_LIMIT),
        name="hyena_out",
    )(yc, vg, x0, skip.reshape(1, N), w, h, g.reshape(1, N), b.reshape(1, N))


def _ffn_in_gate_kernel(x_ref, xp_ref, xn_ref, wg_ref, wa_ref, cw_ref, cb_ref, o_ref, x_scr, g_scr, *, L, tm, tpb):
    i, c = pl.program_id(0), pl.program_id(1)

    @pl.when(c == 0)
    def _():
        _fill_halo_tile(x_scr, x_ref, xp_ref, xn_ref, tm)

    g_scr[...] = jnp.dot(x_scr[...], wg_ref[...], preferred_element_type=f32)
    a = jnp.dot(x_ref[...], wa_ref[...], preferred_element_type=f32)
    _zero_rows_outside_sequence(g_scr, i % tpb, L, tm)
    g = _conv3_centre(g_scr, cw_ref[...], cb_ref[...], tm)
    gelu = 0.5 * g * (1.0 + lax.erf(g * (2.0 ** -0.5)))
    o_ref[...] = (gelu * a).astype(o_ref.dtype)


def ffn_in_gate(x, w_in, conv_w, conv_b, cfg, nchunk=1):
    R, K = x.shape
    tm = cfg.TL
    tc = D_FF // nchunk
    tpb = cfg.Lp // tm
    assert D_FF % nchunk == 0 and tc % LANES == 0 and tm % HALO == 0
    wmode = dict(pipeline_mode=pl.Buffered(1)) if nchunk == 1 else {}
    return pl.pallas_call(
        functools.partial(_ffn_in_gate_kernel, L=cfg.L, tm=tm, tpb=tpb),
        grid=(R // tm, nchunk),
        in_specs=_halo_specs(tm, K, R, lambda i, c: i) + [
                  pl.BlockSpec((K, tc), lambda i, c: (0, c), **wmode),
                  pl.BlockSpec((K, tc), lambda i, c: (0, nchunk + c), **wmode),
                  pl.BlockSpec((3, tc), lambda i, c: (0, c)), pl.BlockSpec((1, tc), lambda i, c: (0, c))],
        out_specs=pl.BlockSpec((tm, tc), lambda i, c: (i, c)),
        out_shape=jax.ShapeDtypeStruct((R, D_FF), bf16),
        scratch_shapes=[pltpu.VMEM((tm + 2 * HALO, K), bf16), pltpu.VMEM((tm + 2 * HALO, tc), f32)],
        compiler_params=_cparams(("arbitrary", "arbitrary"), VMEM_LIMIT),
        name="ffn_in_gate",
    )(x, x, x, w_in, w_in, conv_w, conv_b.reshape(1, -1))


def _dft_tables(cfg):
    N1, NIN, N2, N = cfg.N1, cfg.NIN, FFT_N2, cfg.N
    assert N1 % 2 == 0
    k1 = jnp.arange(cfg.N1H, dtype=jnp.int32)
    wk = jnp.where((k1 == 0) | (k1 == N1 // 2), 1.0, jnp.where(k1 < N1 // 2, 2.0, 0.0)).astype(f32)
    n1 = jnp.arange(NIN, dtype=jnp.int32)
    ang = (2.0 * math.pi / N1) * ((k1[:, None] * n1[None, :]) % N1).astype(f32)
    ca, sa = jnp.cos(ang), jnp.sin(ang)
    fa = jnp.concatenate([ca, -sa], axis=0)
    fa_inv = jnp.concatenate([ca.T * wk, -sa.T * wk], axis=1) * (1.0 / N)
    k2 = jnp.arange(N2, dtype=jnp.int32)[:, None]
    n2 = jnp.arange(N2, dtype=jnp.int32)[None, :]
    t1 = (2.0 * math.pi / N2) * ((n2 * k2) % N2).astype(f32)
    t2 = (2.0 * math.pi / N) * ((n2 * k1[:, None]) % N).astype(f32)
    c1, s1 = jnp.cos(t1)[None], jnp.sin(t1)[None]
    c2, s2 = jnp.cos(t2)[:, None, :], jnp.sin(t2)[:, None, :]
    ar, ai = c1 * c2 - s1 * s2, -(s1 * c2 + c1 * s2)
    g_fwd = jnp.concatenate([jnp.concatenate([ar, -ai], axis=2), jnp.concatenate([ai, ar], axis=2)], axis=1)
    art, ait = jnp.swapaxes(ar, 1, 2), -jnp.swapaxes(ai, 1, 2)
    g_inv = jnp.concatenate([jnp.concatenate([art, -ait], axis=2), jnp.concatenate([ait, art], axis=2)], axis=1)
    return fa.astype(bf16), fa_inv.astype(bf16), g_fwd.astype(bf16), g_inv.astype(bf16)


def _fft_b_fused_kernel(z_ref, h_ref, gf_ref, gi_ref, o_ref):
    zc = z_ref[...].reshape(2 * FFT_N2, z_ref.shape[-1])
    y = jnp.dot(gf_ref[...], zc, preferred_element_type=f32)
    yr, yi = y[:FFT_N2], y[FFT_N2:]
    hr, hi = h_ref[0], h_ref[1]
    pc = jnp.concatenate([yr * hr - yi * hi, yr * hi + yi * hr], axis=0).astype(bf16)
    q = jnp.dot(gi_ref[...], pc, preferred_element_type=f32)
    o_ref[0] = q[:FFT_N2].astype(o_ref.dtype)
    o_ref[1] = q[FFT_N2:].astype(o_ref.dtype)


def fft_b_fused(z5, hspec, g_fwd, g_inv):
    B, _, N1, N2, D = z5.shape
    zspec = pl.BlockSpec((None, 2, None, N2, D), lambda k, b: (b, 0, k, 0, 0))
    gspec = pl.BlockSpec((None, 2 * N2, 2 * N2), lambda k, b: (k, 0, 0))
    return pl.pallas_call(
        _fft_b_fused_kernel,
        grid=(N1, B),
        in_specs=[zspec, pl.BlockSpec((2, None, N2, D), lambda k, b: (0, k, 0, 0)), gspec, gspec],
        out_specs=zspec,
        out_shape=jax.ShapeDtypeStruct(z5.shape, bf16),
        compiler_params=_cparams(("arbitrary", "arbitrary"), VMEM_LIMIT),
        name="fft_b_fused",
    )(z5, hspec, g_fwd, g_inv)


def _fft_b_filter_kernel(z_ref, csum_ref, gf_ref, o_ref):
    D = D_MODEL
    zc = z_ref[...].reshape(2 * FFT_N2, 2 * D)
    y = jnp.dot(gf_ref[...], zc, preferred_element_type=f32)
    scale = 1.0 / (csum_ref[...] + FILTER_NORM_EPS)
    y = y * scale
    yr, yi = y[:FFT_N2], y[FFT_N2:]
    o_ref[0] = yr[:, :D] + yr[:, D:]
    o_ref[1] = yi[:, :D] - yi[:, D:]


def fft_b_filter(z5, csum, g_fwd):
    _, N1, N2, D2 = z5.shape
    return pl.pallas_call(
        _fft_b_filter_kernel,
        grid=(N1,),
        in_specs=[pl.BlockSpec((2, None, N2, D2), lambda k: (0, k, 0, 0)), pl.BlockSpec((1, D2), lambda k: (0, 0)),
                  pl.BlockSpec((None, 2 * N2, 2 * N2), lambda k: (k, 0, 0))],
        out_specs=pl.BlockSpec((2, None, N2, D2 // 2), lambda k: (0, k, 0, 0)),
        out_shape=jax.ShapeDtypeStruct((2, N1, N2, D2 // 2), f32),
        compiler_params=_cparams(("arbitrary",), VMEM_LIMIT),
        name="fft_b_filter",
    )(z5, csum, g_fwd)


def _filter_mlp_kernel(z_ref, w1_ref, b1_ref, w2_ref, b2_ref, w3_ref, b3_ref, fr_ref, dl_ref, o_ref, cs_ref, *, L, tf):
    i = pl.program_id(0)
    hp = lax.Precision.HIGHEST
    z = z_ref[...]
    fr = fr_ref[...]
    h = jnp.sin(fr * (jnp.dot(z, w1_ref[...], preferred_element_type=f32, precision=hp) + b1_ref[...]))
    h = jnp.sin(fr * (jnp.dot(h, w2_ref[...], preferred_element_type=f32, precision=hp) + b2_ref[...]))
    h = jnp.dot(h, w3_ref[...], preferred_element_type=f32, precision=hp) + b3_ref[...]
    win = jnp.exp(-z[:, 0:1] * dl_ref[...]) + DECAY_SHIFT
    h = h * jnp.concatenate([win, win], axis=1)
    rows = i * tf + lax.broadcasted_iota(jnp.int32, (tf, 1), 0)
    h = jnp.where(rows < L, h, 0.0)

    @pl.when(i == 0)
    def _():
        cs_ref[...] = jnp.zeros_like(cs_ref)

    cs_ref[...] += jnp.sum(jnp.abs(h), axis=0, keepdims=True)
    lanes = lax.broadcasted_iota(jnp.int32, (1, h.shape[1]), 1)
    o_ref[...] = jnp.where((rows == 0) & (lanes >= D_MODEL), 0.0, h).astype(o_ref.dtype)


def filter_mlp(z, w1, b1, w2, b2, w3, b3, freq, deltas, cfg, tf=512):
    NR = cfg.NR
    D2 = 2 * D_MODEL
    fixed = lambda i: (0, 0)
    full = lambda a: pl.BlockSpec(a.shape, fixed)
    args = (w1, b1.reshape(1, -1), w2, b2.reshape(1, -1), w3, b3.reshape(1, -1), freq.reshape(1, -1), deltas.reshape(1, -1))
    return pl.pallas_call(
        functools.partial(_filter_mlp_kernel, L=cfg.L, tf=tf),
        grid=(NR // tf,),
        in_specs=[pl.BlockSpec((tf, FILTER_EMB_PAD), lambda i: (i, 0))] + [full(a) for a in args],
        out_specs=[pl.BlockSpec((tf, D2), lambda i: (i, 0)), pl.BlockSpec((1, D2), fixed)],
        out_shape=[jax.ShapeDtypeStruct((NR, D2), bf16), jax.ShapeDtypeStruct((1, D2), f32)],
        compiler_params=_cparams(("arbitrary",), VMEM_LIMIT),
        name="filter_mlp",
    )(z, *args)


def _filter_features(cfg):
    L = cfg.L
    pos = jnp.arange(cfg.NR, dtype=f32)
    t = (pos / (L - 1))[:, None]
    w = (2.0 * math.pi / L) * pos[:, None]
    bands = jnp.linspace(1e-4, FILTER_BANDS - 1, FILTER_BANDS, dtype=f32)
    z = jnp.concatenate([t, jnp.cos(w * bands), -jnp.sin(w * bands)], axis=-1)
    return jnp.pad(z, ((0, 0), (0, FILTER_EMB_PAD - FILTER_EMB_DIM)))


def hyena_filter_spectrum(cfg, tables, w1, b1, w2, b2, w3, b3, freq):
    fa, _, g_fwd, _ = tables
    z = _filter_features(cfg)
    w1p = jnp.pad(w1, ((0, FILTER_EMB_PAD - FILTER_EMB_DIM), (0, 0)))
    max_decay = math.log(1.0 / DECAY_TARGET) / FAST_DECAY_PCT
    min_decay = math.log(1.0 / DECAY_TARGET) / SLOW_DECAY_PCT
    deltas = jnp.linspace(min_decay, max_decay, D_MODEL, dtype=f32)
    xf, csum = filter_mlp(z, w1p, b1, w2, b2, w3, b3, freq, deltas, cfg)
    D2 = 2 * D_MODEL
    z2 = left_matmul(fa, xf.reshape(1, cfg.NIN, FFT_N2 * D2), bf16)
    z5 = z2.reshape(2, cfg.N1H, FFT_N2, D2)
    return fft_b_filter(z5, csum, g_fwd)


def long_conv(vg, hspec, tables, cfg):
    fa, fa_inv, g_fwd, g_inv = tables
    B = vg.shape[0]
    D = D_MODEL
    z2 = left_matmul(fa, vg.reshape(B, cfg.NIN, FFT_N2 * D), bf16)
    q5 = fft_b_fused(z2.reshape(B, 2, cfg.N1H, FFT_N2, D), hspec, g_fwd, g_inv)
    y2 = left_matmul(fa_inv, q5.reshape(B, 2 * cfg.N1H, FFT_N2 * D), bf16)
    return y2.reshape(B, cfg.NR, D)


def _qkv_proj_kernel(x_ref, w_ref, cc_ref, ss_ref, qg_ref, kg_ref, q_ref, k_ref, v_ref, *, L, tm, tpb, sub):
    nq, nk = N_HEADS * HEAD_DIM, N_KV_HEADS * HEAD_DIM
    for r in range(0, tm, sub):
        rows = slice(r, r + sub)
        t = (pl.program_id(0) % tpb) * tm + r + lax.broadcasted_iota(jnp.int32, (sub, 1), 0)
        valid = t < L
        cc, ss = cc_ref[rows, :], ss_ref[rows, :]
        qkv = jnp.dot(x_ref[rows, :], w_ref[...], preferred_element_type=f32)

        def norm_rope(x, gain, scale):
            xn = x * lax.rsqrt(jnp.mean(x * x, axis=-1, keepdims=True) + RMS_EPS) * gain
            y = xn * cc + pltpu.roll(xn, HEAD_DIM // 2, axis=1) * ss
            return jnp.where(valid, y * scale, 0.0).astype(bf16)

        for hh in range(N_HEADS):
            sl = slice(hh * HEAD_DIM, (hh + 1) * HEAD_DIM)
            q_ref[rows, sl] = norm_rope(qkv[:, sl], qg_ref[...], LOG2E * HEAD_DIM ** -0.5)
        for hh in range(N_KV_HEADS):
            sl = slice(hh * HEAD_DIM, (hh + 1) * HEAD_DIM)
            k_ref[rows, sl] = norm_rope(qkv[:, nq + hh * HEAD_DIM:nq + (hh + 1) * HEAD_DIM], kg_ref[...], 1.0)
        v_ref[rows, :] = jnp.where(valid, qkv[:, nq + nk:], 0.0).astype(bf16)


def qkv_proj(x, w, cc, ss, q_gain, k_gain, cfg):
    R, K = x.shape
    tm, tpb = cfg.TL, cfg.Lp // cfg.TL
    nq, nk = N_HEADS * HEAD_DIM, N_KV_HEADS * HEAD_DIM
    assert tm % 2 == 0 and (tm // 2) % HALO == 0
    row = lambda i: (i, 0)
    tab = lambda i: (i % tpb, 0)
    fixed = lambda i: (0, 0)
    return pl.pallas_call(
        functools.partial(_qkv_proj_kernel, L=cfg.L, tm=tm, tpb=tpb, sub=tm // 2),
        grid=(R // tm,),
        in_specs=[pl.BlockSpec((tm, K), row), pl.BlockSpec((K, nq + 2 * nk), fixed), pl.BlockSpec((tm, HEAD_DIM), tab),
                  pl.BlockSpec((tm, HEAD_DIM), tab), pl.BlockSpec((1, HEAD_DIM), fixed), pl.BlockSpec((1, HEAD_DIM), fixed)],
        out_specs=[pl.BlockSpec((tm, nq), row), pl.BlockSpec((tm, nk), row), pl.BlockSpec((tm, nk), row)],
        out_shape=[jax.ShapeDtypeStruct((R, nq), bf16), jax.ShapeDtypeStruct((R, nk), bf16),
                   jax.ShapeDtypeStruct((R, nk), bf16)],
        compiler_params=_cparams(("arbitrary",), VMEM_LIMIT),
        name="qkv_proj",
    )(x, w, cc, ss, q_gain.reshape(1, -1), k_gain.reshape(1, -1))


def _flash_kernel(q_ref, k_ref, v_ref, o_ref, s0_scr, s1_scr, p0_scr, p1_scr, *, L):
    s_bufs, p_bufs = (s0_scr, s1_scr), (p0_scr, p1_scr)
    lp = k_ref.shape[0]
    c0 = (L // LANES) * LANES
    for g in range(GROUP):
        s_scr, p_scr = s_bufs[g % 2], p_bufs[g % 2]
        q = q_ref[:, g * HEAD_DIM:(g + 1) * HEAD_DIM]
        s = lax.dot_general(q, k_ref[...], (((1,), (1,)), ((), ())), preferred_element_type=f32)
        if c0 < lp:
            cols = c0 + lax.broadcasted_iota(jnp.int32, (1, lp - c0), 1)
            s_scr[:, :c0] = s[:, :c0]
            s_scr[:, c0:] = jnp.where(cols < L, s[:, c0:], NEG_BIG)
        else:
            s_scr[...] = s
        s = s_scr[...]
        p = jnp.exp2(s - jnp.max(s, axis=-1, keepdims=True))
        l = jnp.sum(p, axis=-1, keepdims=True)
        p_scr[...] = p.astype(bf16)
        o = jnp.dot(p_scr[...], v_ref[...], preferred_element_type=f32)
        o_ref[:, g * HEAD_DIM:(g + 1) * HEAD_DIM] = (o / l).astype(o_ref.dtype)


def flash_attention(q, k, v, cfg):
    B = q.shape[0]
    tq, lp = cfg.TQ, cfg.Lp
    gw = GROUP * HEAD_DIM
    return pl.pallas_call(
        functools.partial(_flash_kernel, L=cfg.L),
        grid=(B, N_KV_HEADS, lp // tq),
        in_specs=[pl.BlockSpec((None, tq, gw), lambda b, h, i: (b, i, h)),
                  pl.BlockSpec((None, lp, HEAD_DIM), lambda b, h, i: (b, 0, h)),
                  pl.BlockSpec((None, lp, HEAD_DIM), lambda b, h, i: (b, 0, h))],
        out_specs=pl.BlockSpec((None, tq, gw), lambda b, h, i: (b, i, h)),
        out_shape=jax.ShapeDtypeStruct(q.shape, bf16),
        scratch_shapes=[pltpu.VMEM((tq, lp), f32), pltpu.VMEM((tq, lp), f32),
                        pltpu.VMEM((tq, lp), bf16), pltpu.VMEM((tq, lp), bf16)],
        compiler_params=_cparams(("arbitrary", "arbitrary", "arbitrary"), VMEM_LIMIT),
        name="flash",
    )(q, k, v)


def _rope_tables(cfg):
    n = cfg.L - N_META
    rows = n // GRID_W
    row = jnp.concatenate([jnp.full((N_META,), -1.0, f32), jnp.repeat(jnp.arange(rows, dtype=f32), GRID_W)])
    col = jnp.concatenate([jnp.arange(N_META, dtype=f32), jnp.tile(jnp.arange(GRID_W, dtype=f32), rows)])
    axis_rot = HEAD_DIM // 2
    inv_freq = ROPE_THETA ** (-jnp.arange(0, axis_rot, 2, dtype=f32) / axis_rot)
    ang = jnp.concatenate([row[:, None] * inv_freq, col[:, None] * inv_freq], axis=-1)
    ang = jnp.pad(ang, ((0, cfg.Lp - cfg.L), (0, 0)))
    c, s = jnp.cos(ang), jnp.sin(ang)
    return jnp.concatenate([c, c], axis=-1), jnp.concatenate([-s, s], axis=-1)


def _run_trunk(x, cfg, meta_tokens, hy, at, ln, ffn):
    B, n, D = x.shape
    Lp = cfg.Lp
    R = B * Lp
    meta = jnp.broadcast_to(meta_tokens[None], (B, N_META, D))
    h = jnp.concatenate([meta, x, jnp.zeros((B, Lp - cfg.L, D), x.dtype)], axis=1).reshape(R, D)
    hb = h.astype(bf16)
    cc, ss = _rope_tables(cfg)
    tables = _dft_tables(cfg)
    for i in range(DEPTH):
        j = i // 2
        if i % 2 == 0:
            hspec = hyena_filter_spectrum(cfg, tables, hy["f_w1"][j], hy["f_b1"][j], hy["f_w2"][j], hy["f_b2"][j],
                                          hy["f_w3"][j], hy["f_b3"][j], hy["f_freq"][j])
            x0c, vg = hyena_in(hb, hy["w_in"][j], hy["conv_w"][j], hy["conv_b"][j], cfg)
            yc = long_conv(vg.reshape(B, cfg.NR, D), hspec, tables, cfg)
            h, hb = hyena_out(yc.reshape(B * cfg.NR, D), vg, x0c, hy["skip"][j], hy["w_out"][j], h,
                              ln["g1"][i], ln["b1"][i], cfg)
        else:
            q, k, v = qkv_proj(hb, at["w_qkv"][j], cc, ss, at["q_gain"][j], at["k_gain"][j], cfg)
            o = flash_attention(q.reshape(B, Lp, -1), k.reshape(B, Lp, -1), v.reshape(B, Lp, -1), cfg)
            h, hb = matmul_residual_ln(o.reshape(R, D), at["w_out"][j], h, ln["g1"][i], ln["b1"][i], cfg.TL)
        gated = ffn_in_gate(hb, ffn["w_in"][i], ffn["conv_w"][i], ffn["conv_b"][i], cfg)
        h, hb = matmul_residual_ln(gated, ffn["w_out"][i], h, ln["g2"][i], ln["b2"][i], cfg.TL)
    return h.reshape(B, Lp, D)[:, N_META:cfg.L]


def kernel(x_prompt, x_sample, meta_tokens, hy_w_in, hy_conv_w, hy_conv_b, hy_filt_w1, hy_filt_b1, hy_filt_w2, hy_filt_b2, hy_filt_w3, hy_filt_b3, hy_filt_freq, hy_skip, hy_w_out, at_w_qkv, at_q_gain, at_k_gain, at_w_out, ln1_g, ln1_b, ln2_g, ln2_b, ffn_w_in, ffn_conv_w, ffn_conv_b, ffn_w_out):
    hy = dict(w_in=hy_w_in.astype(bf16), conv_w=hy_conv_w, conv_b=hy_conv_b, f_w1=hy_filt_w1, f_b1=hy_filt_b1,
              f_w2=hy_filt_w2, f_b2=hy_filt_b2, f_w3=hy_filt_w3, f_b3=hy_filt_b3, f_freq=hy_filt_freq, skip=hy_skip,
              w_out=hy_w_out.astype(bf16))
    at = dict(w_qkv=at_w_qkv.astype(bf16), q_gain=at_q_gain, k_gain=at_k_gain, w_out=at_w_out.astype(bf16))
    ln = dict(g1=ln1_g, b1=ln1_b, g2=ln2_g, b2=ln2_b)
    ffn = dict(w_in=ffn_w_in.astype(bf16), conv_w=ffn_conv_w, conv_b=ffn_conv_b, w_out=ffn_w_out.astype(bf16))
    y_prompt = _run_trunk(x_prompt, Cfg(x_prompt.shape[1]), meta_tokens, hy, at, ln, ffn)
    y_sample = _run_trunk(x_sample, Cfg(x_sample.shape[1]), meta_tokens, hy, at, ln, ffn)
    return (y_prompt, y_sample)
```

```python
import functools
import math

import jax
import jax.numpy as jnp
from jax import lax
from jax.experimental import pallas as pl
from jax.experimental.pallas import tpu as pltpu

f32 = jnp.float32
bf16 = jnp.bfloat16

D_MODEL = 1024
DEPTH = 4
N_META = 16
GRID_W = 64
HEAD_DIM = 128
N_HEADS = 8
N_KV_HEADS = 2
GROUP = N_HEADS // N_KV_HEADS
ROPE_THETA = 10000.0
FILTER_EMB_DIM = 33
FILTER_EMB_PAD = 40
FILTER_BANDS = 16
FILTER_HIDDEN = 64
DECAY_TARGET = 1e-2
FAST_DECAY_PCT = 0.3
SLOW_DECAY_PCT = 1.5
DECAY_SHIFT = 0.05
FILTER_NORM_EPS = 1e-6
D_FF = 2816
DEEPNORM_ALPHA = (2 * DEPTH) ** 0.25
LN_EPS = 1e-5
RMS_EPS = 1e-6
NEG_BIG = -1e30
LOG2E = 1.4426950408889634

LANES = 128
FFT_N2 = 128
V7X_VMEM_BYTES = 64 * 1024 * 1024
VMEM_LIMIT = 52 * 1024 * 1024


class Cfg:
    def __init__(self, n):
        self.L = n + N_META
        if n == 8192:
            self.Lp, self.TL, self.TQ, self.N1, self.NIN = 8448, 768, 256, 136, 96
        elif n == 2048:
            self.Lp, self.TL, self.TQ, self.N1, self.NIN = 2304, 768, 256, 40, 48
        else:
            up = lambda a, m: -(-a // m) * m
            self.TL = self.TQ = 128
            self.Lp = up(self.L, 128)
            self.N1 = up(-(-(2 * self.L - 1) // FFT_N2), 8)
            self.NIN = up(-(-self.Lp // FFT_N2), 16)
        self.NR = self.NIN * FFT_N2
        self.N1H = -(-(self.N1 // 2 + 1) // 8) * 8
        self.N = self.N1 * FFT_N2
        assert self.N >= 2 * self.L - 1 and self.NR >= self.Lp
        assert self.Lp % self.TL == 0 and self.Lp % self.TQ == 0 and self.NR % self.TL == 0


def _cparams(sem, vmem=None):
    return pltpu.CompilerParams(dimension_semantics=sem, vmem_limit_bytes=vmem)


def _residual_ln(m, h_ref, g_ref, b_ref, o_ref, obf_ref):
    y = DEEPNORM_ALPHA * h_ref[...] + m
    mu = jnp.mean(y, axis=-1, keepdims=True)
    yc = y - mu
    var = jnp.mean(yc * yc, axis=-1, keepdims=True)
    out = yc * lax.rsqrt(var + LN_EPS) * g_ref[...] + b_ref[...]
    o_ref[...] = out
    obf_ref[...] = out.astype(bf16)


def _mm_ln_kernel(x_ref, w_ref, h_ref, g_ref, b_ref, o_ref, obf_ref):
    half = x_ref.shape[0] // 2
    for r in (0, half):
        rows = pl.ds(r, half)
        _residual_ln(jnp.dot(x_ref[rows, :], w_ref[...], preferred_element_type=f32),
                     h_ref.at[rows, :], g_ref, b_ref, o_ref.at[rows, :], obf_ref.at[rows, :])


def matmul_residual_ln(x, w, h, g, b, tm):
    R, K = x.shape
    N = w.shape[1]
    row = lambda i: (i, 0)
    fixed = lambda i: (0, 0)
    return pl.pallas_call(
        _mm_ln_kernel,
        grid=(R // tm,),
        in_specs=[pl.BlockSpec((tm, K), row), pl.BlockSpec((K, N), fixed), pl.BlockSpec((tm, N), row),
                  pl.BlockSpec((1, N), fixed), pl.BlockSpec((1, N), fixed)],
        out_specs=[pl.BlockSpec((tm, N), row), pl.BlockSpec((tm, N), row)],
        out_shape=[jax.ShapeDtypeStruct((R, N), f32), jax.ShapeDtypeStruct((R, N), bf16)],
        compiler_params=_cparams(("arbitrary",), VMEM_LIMIT),
        name="mm_ln",
    )(x, w, h, g.reshape(1, N), b.reshape(1, N))


def _left_mm_kernel(a_ref, x_ref, o_ref, *, cw):
    a = a_ref[...]
    for c in range(0, x_ref.shape[1], cw):
        o_ref[:, c:c + cw] = jnp.dot(a, x_ref[:, c:c + cw], preferred_element_type=f32).astype(o_ref.dtype)


def left_matmul(a, x, out_dtype, tn=32768, cw=512):
    M, K = a.shape
    B, _, C = x.shape
    assert C % tn == 0 and tn % cw == 0
    return pl.pallas_call(
        functools.partial(_left_mm_kernel, cw=cw),
        grid=(B, C // tn),
        in_specs=[pl.BlockSpec((M, K), lambda b, j: (0, 0)), pl.BlockSpec((None, K, tn), lambda b, j: (b, 0, j))],
        out_specs=pl.BlockSpec((None, M, tn), lambda b, j: (b, 0, j)),
        out_shape=jax.ShapeDtypeStruct((B, M, C), out_dtype),
        compiler_params=_cparams(("arbitrary", "arbitrary"), VMEM_LIMIT),
        name="left_mm",
    )(a, x)


HALO = 16


def _halo_specs(tm, K, R, row_block):
    nh = tm // HALO
    last = R // HALO - 1
    return [pl.BlockSpec((tm, K), lambda *g: (row_block(*g), 0)),
            pl.BlockSpec((HALO, K), lambda *g: (jnp.maximum(row_block(*g) * nh - 1, 0), 0)),
            pl.BlockSpec((HALO, K), lambda *g: (jnp.minimum((row_block(*g) + 1) * nh, last), 0))]


def _fill_halo_tile(x_scr, x_ref, xp_ref, xn_ref, tm):
    x_scr[0:HALO, :] = xp_ref[...]
    x_scr[HALO:HALO + tm, :] = x_ref[...]
    x_scr[HALO + tm:, :] = xn_ref[...]


def _zero_rows_outside_sequence(u_scr, j, L, tm):
    assert L % tm != 0
    j_end, r_end = L // tm, HALO + L % tm
    g_end = (r_end // 8) * 8
    sub8 = lax.broadcasted_iota(jnp.int32, (8, 1), 0)
    u_scr[HALO - 8:HALO, :] = u_scr[HALO - 8:HALO, :] * jnp.where((j == 0) & (sub8 == 7), 0.0, 1.0)
    u_scr[g_end:g_end + 8, :] = u_scr[g_end:g_end + 8, :] * jnp.where((j == j_end) & (sub8 == r_end - g_end), 0.0, 1.0)


def _conv3_centre(u_scr, cw, cb, tm):
    return (u_scr[HALO - 1:HALO - 1 + tm, :] * cw[0:1, :] + u_scr[HALO:HALO + tm, :] * cw[1:2, :]
            + u_scr[HALO + 1:HALO + 1 + tm, :] * cw[2:3, :] + cb)


def _hyena_in_kernel(x_ref, xp_ref, xn_ref, w_ref, cw_ref, cb_ref, x0_ref, vg_ref, x_scr, u0_scr, u1_scr, uv_scr,
                     *, L, tm, tpb):
    j = pl.program_id(1)
    D = D_MODEL

    @pl.when(j < tpb)
    def _():
        _fill_halo_tile(x_scr, x_ref, xp_ref, xn_ref, tm)
        xs = x_scr[...]
        t = j * tm + lax.broadcasted_iota(jnp.int32, (tm, 1), 0)
        cw, cb = cw_ref[...], cb_ref[...]
        conv = []
        for s, scr in enumerate((u0_scr, u1_scr, uv_scr)):
            sl = slice(s * D, (s + 1) * D)
            scr[...] = jnp.dot(xs, w_ref[:, sl], preferred_element_type=f32)
            _zero_rows_outside_sequence(scr, j, L, tm)
            conv.append(_conv3_centre(scr, cw[:, sl], cb[:, sl], tm))
        x0_ref[...] = conv[0].astype(x0_ref.dtype)
        vg_ref[...] = jnp.where(t < L, conv[1] * conv[2], 0.0).astype(vg_ref.dtype)

    @pl.when(j >= tpb)
    def _():
        vg_ref[...] = jnp.zeros_like(vg_ref)


def hyena_in(x, w_in, conv_w, conv_b, cfg):
    R, K = x.shape
    D = D_MODEL
    tm, tpb, ntr = cfg.TL, cfg.Lp // cfg.TL, cfg.NR // cfg.TL
    B = R // cfg.Lp
    row_block = lambda b, j: b * tpb + jnp.minimum(j, tpb - 1)
    fixed = lambda b, j: (0, 0)
    return pl.pallas_call(
        functools.partial(_hyena_in_kernel, L=cfg.L, tm=tm, tpb=tpb),
        grid=(B, ntr),
        in_specs=_halo_specs(tm, K, R, row_block) +
                 [pl.BlockSpec((K, 3 * D), fixed, pipeline_mode=pl.Buffered(1)),
                  pl.BlockSpec((3, 3 * D), fixed), pl.BlockSpec((1, 3 * D), fixed)],
        out_specs=[pl.BlockSpec((tm, D), lambda b, j: (row_block(b, j), 0)),
                   pl.BlockSpec((tm, D), lambda b, j: (b * ntr + j, 0))],
        out_shape=[jax.ShapeDtypeStruct((R, D), bf16), jax.ShapeDtypeStruct((B * cfg.NR, D), bf16)],
        scratch_shapes=[pltpu.VMEM((tm + 2 * HALO, K), bf16)] + [pltpu.VMEM((tm + 2 * HALO, D), f32)] * 3,
        compiler_params=_cparams(("arbitrary", "arbitrary"), VMEM_LIMIT),
        name="hyena_in",
    )(x, x, x, w_in, conv_w, conv_b.reshape(1, -1))


def _hyena_out_kernel(y_ref, vg_ref, x0_ref, skip_ref, w_ref, h_ref, g_ref, b_ref, o_ref, obf_ref):
    y = y_ref[...].astype(f32) + vg_ref[...].astype(f32) * skip_ref[...]
    x = (y * x0_ref[...].astype(f32)).astype(bf16)
    _residual_ln(jnp.dot(x, w_ref[...], preferred_element_type=f32), h_ref, g_ref, b_ref, o_ref, obf_ref)


def hyena_out(yc, vg, x0, skip, w, h, g, b, cfg):
    R, N = h.shape
    tm, tpb, ntr = cfg.TL, cfg.Lp // cfg.TL, cfg.NR // cfg.TL
    row = lambda i: (i, 0)
    padded_row = lambda i: ((i // tpb) * ntr + i % tpb, 0)
    fixed = lambda i: (0, 0)
    return pl.pallas_call(
        _hyena_out_kernel,
        grid=(R // tm,),
        in_specs=[pl.BlockSpec((tm, N), padded_row), pl.BlockSpec((tm, N), padded_row), pl.BlockSpec((tm, N), row),
                  pl.BlockSpec((1, N), fixed), pl.BlockSpec((N, N), fixed), pl.BlockSpec((tm, N), row),
                  pl.BlockSpec((1, N), fixed), pl.BlockSpec((1, N), fixed)],
        out_specs=[pl.BlockSpec((tm, N), row), pl.BlockSpec((tm, N), row)],
        out_shape=[jax.ShapeDtypeStruct((R, N), f32), jax.ShapeDtypeStruct((R, N), bf16)],
        compiler_params=_cparams(("arbitrary",), VMEM_LIMIT),
        name="hyena_out",
    )(yc, vg, x0, skip.reshape(1, N), w, h, g.reshape(1, N), b.reshape(1, N))


def _ffn_in_gate_kernel(x_ref, xp_ref, xn_ref, wg_ref, wa_ref, cw_ref, cb_ref, o_ref, x_scr, g_scr, *, L, tm, tpb):
    i, c = pl.program_id(0), pl.program_id(1)

    @pl.when(c == 0)
    def _():
        _fill_halo_tile(x_scr, x_ref, xp_ref, xn_ref, tm)

    g_scr[...] = jnp.dot(x_scr[...], wg_ref[...], preferred_element_type=f32)
    a = jnp.dot(x_ref[...], wa_ref[...], preferred_element_type=f32)
    _zero_rows_outside_sequence(g_scr, i % tpb, L, tm)
    g = _conv3_centre(g_scr, cw_ref[...], cb_ref[...], tm)
    gelu = 0.5 * g * (1.0 + lax.erf(g * (2.0 ** -0.5)))
    o_ref[...] = (gelu * a).astype(o_ref.dtype)


def ffn_in_gate(x, w_in, conv_w, conv_b, cfg, nchunk=1):
    R, K = x.shape
    tm = cfg.TL
    tc = D_FF // nchunk
    tpb = cfg.Lp // tm
    assert D_FF % nchunk == 0 and tc % LANES == 0 and tm % HALO == 0
    wmode = dict(pipeline_mode=pl.Buffered(1)) if nchunk == 1 else {}
    return pl.pallas_call(
        functools.partial(_ffn_in_gate_kernel, L=cfg.L, tm=tm, tpb=tpb),
        grid=(R // tm, nchunk),
        in_specs=_halo_specs(tm, K, R, lambda i, c: i) + [
                  pl.BlockSpec((K, tc), lambda i, c: (0, c), **wmode),
                  pl.BlockSpec((K, tc), lambda i, c: (0, nchunk + c), **wmode),
                  pl.BlockSpec((3, tc), lambda i, c: (0, c)), pl.BlockSpec((1, tc), lambda i, c: (0, c))],
        out_specs=pl.BlockSpec((tm, tc), lambda i, c: (i, c)),
        out_shape=jax.ShapeDtypeStruct((R, D_FF), bf16),
        scratch_shapes=[pltpu.VMEM((tm + 2 * HALO, K), bf16), pltpu.VMEM((tm + 2 * HALO, tc), f32)],
        compiler_params=_cparams(("arbitrary", "arbitrary"), VMEM_LIMIT),
        name="ffn_in_gate",
    )(x, x, x, w_in, w_in, conv_w, conv_b.reshape(1, -1))


def _dft_tables(cfg):
    N1, NIN, N2, N = cfg.N1, cfg.NIN, FFT_N2, cfg.N
    assert N1 % 2 == 0
    k1 = jnp.arange(cfg.N1H, dtype=jnp.int32)
    wk = jnp.where((k1 == 0) | (k1 == N1 // 2), 1.0, jnp.where(k1 < N1 // 2, 2.0, 0.0)).astype(f32)
    n1 = jnp.arange(NIN, dtype=jnp.int32)
    ang = (2.0 * math.pi / N1) * ((k1[:, None] * n1[None, :]) % N1).astype(f32)
    ca, sa = jnp.cos(ang), jnp.sin(ang)
    fa = jnp.concatenate([ca, -sa], axis=0)
    fa_inv = jnp.concatenate([ca.T * wk, -sa.T * wk], axis=1) * (1.0 / N)
    k2 = jnp.arange(N2, dtype=jnp.int32)[:, None]
    n2 = jnp.arange(N2, dtype=jnp.int32)[None, :]
    t1 = (2.0 * math.pi / N2) * ((n2 * k2) % N2).astype(f32)
    t2 = (2.0 * math.pi / N) * ((n2 * k1[:, None]) % N).astype(f32)
    c1, s1 = jnp.cos(t1)[None], jnp.sin(t1)[None]
    c2, s2 = jnp.cos(t2)[:, None, :], jnp.sin(t2)[:, None, :]
    ar, ai = c1 * c2 - s1 * s2, -(s1 * c2 + c1 * s2)
    g_fwd = jnp.concatenate([jnp.concatenate([ar, -ai], axis=2), jnp.concatenate([ai, ar], axis=2)], axis=1)
    art, ait = jnp.swapaxes(ar, 1, 2), -jnp.swapaxes(ai, 1, 2)
    g_inv = jnp.concatenate([jnp.concatenate([art, -ait], axis=2), jnp.concatenate([ait, art], axis=2)], axis=1)
    return fa.astype(bf16), fa_inv.astype(bf16), g_fwd.astype(bf16), g_inv.astype(bf16)


FFT_KB = 4


def _fft_b_fused_kernel(z_ref, h_ref, gf_ref, gi_ref, o_ref):
    for kb in range(FFT_KB):
        zc = jnp.concatenate([z_ref[0, kb], z_ref[1, kb]], axis=0)
        y = jnp.dot(gf_ref[kb], zc, preferred_element_type=f32)
        yr, yi = y[:FFT_N2], y[FFT_N2:]
        hr, hi = h_ref[0, kb], h_ref[1, kb]
        pc = jnp.concatenate([yr * hr - yi * hi, yr * hi + yi * hr], axis=0).astype(bf16)
        q = jnp.dot(gi_ref[kb], pc, preferred_element_type=f32)
        o_ref[0, kb] = q[:FFT_N2].astype(o_ref.dtype)
        o_ref[1, kb] = q[FFT_N2:].astype(o_ref.dtype)


def fft_b_fused(z5, hspec, g_fwd, g_inv):
    B, _, N1, N2, D = z5.shape
    assert N1 % FFT_KB == 0
    zspec = pl.BlockSpec((None, 2, FFT_KB, N2, D), lambda k, b: (b, 0, k, 0, 0))
    gspec = pl.BlockSpec((FFT_KB, 2 * N2, 2 * N2), lambda k, b: (k, 0, 0))
    return pl.pallas_call(
        _fft_b_fused_kernel,
        grid=(N1 // FFT_KB, B),
        in_specs=[zspec, pl.BlockSpec((2, FFT_KB, N2, D), lambda k, b: (0, k, 0, 0)), gspec, gspec],
        out_specs=zspec,
        out_shape=jax.ShapeDtypeStruct(z5.shape, bf16),
        compiler_params=_cparams(("arbitrary", "arbitrary"), VMEM_LIMIT),
        name="fft_b_fused",
    )(z5, hspec, g_fwd, g_inv)


def _fft_b_filter_kernel(z_ref, csum_ref, gf_ref, o_ref):
    D = D_MODEL
    zc = z_ref[...].reshape(2 * FFT_N2, 2 * D)
    y = jnp.dot(gf_ref[...], zc, preferred_element_type=f32)
    scale = 1.0 / (csum_ref[...] + FILTER_NORM_EPS)
    y = y * scale
    yr, yi = y[:FFT_N2], y[FFT_N2:]
    o_ref[0] = yr[:, :D] + yr[:, D:]
    o_ref[1] = yi[:, :D] - yi[:, D:]


def fft_b_filter(z5, csum, g_fwd):
    _, N1, N2, D2 = z5.shape
    return pl.pallas_call(
        _fft_b_filter_kernel,
        grid=(N1,),
        in_specs=[pl.BlockSpec((2, None, N2, D2), lambda k: (0, k, 0, 0)), pl.BlockSpec((1, D2), lambda k: (0, 0)),
                  pl.BlockSpec((None, 2 * N2, 2 * N2), lambda k: (k, 0, 0))],
        out_specs=pl.BlockSpec((2, None, N2, D2 // 2), lambda k: (0, k, 0, 0)),
        out_shape=jax.ShapeDtypeStruct((2, N1, N2, D2 // 2), f32),
        compiler_params=_cparams(("arbitrary",), VMEM_LIMIT),
        name="fft_b_filter",
    )(z5, csum, g_fwd)


def _filter_mlp_kernel(z_ref, w1_ref, b1_ref, w2_ref, b2_ref, w3_ref, b3_ref, fr_ref, dl_ref, o_ref, cs_ref, *, L, tf):
    i = pl.program_id(0)
    hp = lax.Precision.HIGHEST
    z = z_ref[...]
    fr = fr_ref[...]
    h = jnp.sin(fr * (jnp.dot(z, w1_ref[...], preferred_element_type=f32, precision=hp) + b1_ref[...]))
    h = jnp.sin(fr * (jnp.dot(h, w2_ref[...], preferred_element_type=f32, precision=hp) + b2_ref[...]))
    h = jnp.dot(h, w3_ref[...], preferred_element_type=f32, precision=hp) + b3_ref[...]
    win = jnp.exp(-z[:, 0:1] * dl_ref[...]) + DECAY_SHIFT
    h = h * jnp.concatenate([win, win], axis=1)
    rows = i * tf + lax.broadcasted_iota(jnp.int32, (tf, 1), 0)
    h = jnp.where(rows < L, h, 0.0)

    @pl.when(i == 0)
    def _():
        cs_ref[...] = jnp.zeros_like(cs_ref)

    cs_ref[...] += jnp.sum(jnp.abs(h), axis=0, keepdims=True)
    lanes = lax.broadcasted_iota(jnp.int32, (1, h.shape[1]), 1)
    o_ref[...] = jnp.where((rows == 0) & (lanes >= D_MODEL), 0.0, h).astype(o_ref.dtype)


def filter_mlp(z, w1, b1, w2, b2, w3, b3, freq, deltas, cfg, tf=512):
    NR = cfg.NR
    D2 = 2 * D_MODEL
    fixed = lambda i: (0, 0)
    full = lambda a: pl.BlockSpec(a.shape, fixed)
    args = (w1, b1.reshape(1, -1), w2, b2.reshape(1, -1), w3, b3.reshape(1, -1), freq.reshape(1, -1), deltas.reshape(1, -1))
    return pl.pallas_call(
        functools.partial(_filter_mlp_kernel, L=cfg.L, tf=tf),
        grid=(NR // tf,),
        in_specs=[pl.BlockSpec((tf, FILTER_EMB_PAD), lambda i: (i, 0))] + [full(a) for a in args],
        out_specs=[pl.BlockSpec((tf, D2), lambda i: (i, 0)), pl.BlockSpec((1, D2), fixed)],
        out_shape=[jax.ShapeDtypeStruct((NR, D2), bf16), jax.ShapeDtypeStruct((1, D2), f32)],
        compiler_params=_cparams(("arbitrary",), VMEM_LIMIT),
        name="filter_mlp",
    )(z, *args)


def _filter_features(cfg):
    L = cfg.L
    pos = jnp.arange(cfg.NR, dtype=f32)
    t = (pos / (L - 1))[:, None]
    w = (2.0 * math.pi / L) * pos[:, None]
    bands = jnp.linspace(1e-4, FILTER_BANDS - 1, FILTER_BANDS, dtype=f32)
    z = jnp.concatenate([t, jnp.cos(w * bands), -jnp.sin(w * bands)], axis=-1)
    return jnp.pad(z, ((0, 0), (0, FILTER_EMB_PAD - FILTER_EMB_DIM)))


def hyena_filter_spectrum(cfg, tables, w1, b1, w2, b2, w3, b3, freq):
    fa, _, g_fwd, _ = tables
    z = _filter_features(cfg)
    w1p = jnp.pad(w1, ((0, FILTER_EMB_PAD - FILTER_EMB_DIM), (0, 0)))
    max_decay = math.log(1.0 / DECAY_TARGET) / FAST_DECAY_PCT
    min_decay = math.log(1.0 / DECAY_TARGET) / SLOW_DECAY_PCT
    deltas = jnp.linspace(min_decay, max_decay, D_MODEL, dtype=f32)
    xf, csum = filter_mlp(z, w1p, b1, w2, b2, w3, b3, freq, deltas, cfg)
    D2 = 2 * D_MODEL
    z2 = left_matmul(fa, xf.reshape(1, cfg.NIN, FFT_N2 * D2), bf16)
    z5 = z2.reshape(2, cfg.N1H, FFT_N2, D2)
    return fft_b_filter(z5, csum, g_fwd)


def long_conv(vg, hspec, tables, cfg):
    fa, fa_inv, g_fwd, g_inv = tables
    B = vg.shape[0]
    D = D_MODEL
    z2 = left_matmul(fa, vg.reshape(B, cfg.NIN, FFT_N2 * D), bf16)
    q5 = fft_b_fused(z2.reshape(B, 2, cfg.N1H, FFT_N2, D), hspec, g_fwd, g_inv)
    y2 = left_matmul(fa_inv, q5.reshape(B, 2 * cfg.N1H, FFT_N2 * D), bf16)
    return y2.reshape(B, cfg.NR, D)


def _qkv_proj_kernel(x_ref, w_ref, cc_ref, ss_ref, qg_ref, kg_ref, q_ref, k_ref, v_ref, *, L, tm, tpb, sub):
    nq, nk = N_HEADS * HEAD_DIM, N_KV_HEADS * HEAD_DIM
    for r in range(0, tm, sub):
        rows = slice(r, r + sub)
        t = (pl.program_id(0) % tpb) * tm + r + lax.broadcasted_iota(jnp.int32, (sub, 1), 0)
        valid = t < L
        cc, ss = cc_ref[rows, :], ss_ref[rows, :]
        qkv = jnp.dot(x_ref[rows, :], w_ref[...], preferred_element_type=f32)

        def norm_rope(x, gain, scale):
            xn = x * lax.rsqrt(jnp.mean(x * x, axis=-1, keepdims=True) + RMS_EPS) * gain
            y = xn * cc + pltpu.roll(xn, HEAD_DIM // 2, axis=1) * ss
            return jnp.where(valid, y * scale, 0.0).astype(bf16)

        for hh in range(N_HEADS):
            sl = slice(hh * HEAD_DIM, (hh + 1) * HEAD_DIM)
            q_ref[rows, sl] = norm_rope(qkv[:, sl], qg_ref[...], LOG2E * HEAD_DIM ** -0.5)
        for hh in range(N_KV_HEADS):
            sl = slice(hh * HEAD_DIM, (hh + 1) * HEAD_DIM)
            k_ref[rows, sl] = norm_rope(qkv[:, nq + hh * HEAD_DIM:nq + (hh + 1) * HEAD_DIM], kg_ref[...], 1.0)
        v_ref[rows, :] = jnp.where(valid, qkv[:, nq + nk:], 0.0).astype(bf16)


def qkv_proj(x, w, cc, ss, q_gain, k_gain, cfg):
    R, K = x.shape
    tm, tpb = cfg.TL, cfg.Lp // cfg.TL
    nq, nk = N_HEADS * HEAD_DIM, N_KV_HEADS * HEAD_DIM
    assert tm % 2 == 0 and (tm // 2) % HALO == 0
    row = lambda i: (i, 0)
    tab = lambda i: (i % tpb, 0)
    fixed = lambda i: (0, 0)
    return pl.pallas_call(
        functools.partial(_qkv_proj_kernel, L=cfg.L, tm=tm, tpb=tpb, sub=tm // 2),
        grid=(R // tm,),
        in_specs=[pl.BlockSpec((tm, K), row), pl.BlockSpec((K, nq + 2 * nk), fixed), pl.BlockSpec((tm, HEAD_DIM), tab),
                  pl.BlockSpec((tm, HEAD_DIM), tab), pl.BlockSpec((1, HEAD_DIM), fixed), pl.BlockSpec((1, HEAD_DIM), fixed)],
        out_specs=[pl.BlockSpec((tm, nq), row), pl.BlockSpec((tm, nk), row), pl.BlockSpec((tm, nk), row)],
        out_shape=[jax.ShapeDtypeStruct((R, nq), bf16), jax.ShapeDtypeStruct((R, nk), bf16),
                   jax.ShapeDtypeStruct((R, nk), bf16)],
        compiler_params=_cparams(("arbitrary",), VMEM_LIMIT),
        name="qkv_proj",
    )(x, w, cc, ss, q_gain.reshape(1, -1), k_gain.reshape(1, -1))


def _flash_kernel(q_ref, k_ref, v_ref, o_ref, s0_scr, s1_scr, p0_scr, p1_scr, *, L):
    s_bufs, p_bufs = (s0_scr, s1_scr), (p0_scr, p1_scr)
    lp = k_ref.shape[0]
    c0 = (L // LANES) * LANES
    for g in range(GROUP):
        s_scr, p_scr = s_bufs[g % 2], p_bufs[g % 2]
        q = q_ref[:, g * HEAD_DIM:(g + 1) * HEAD_DIM]
        s = lax.dot_general(q, k_ref[...], (((1,), (1,)), ((), ())), preferred_element_type=f32)
        if c0 < lp:
            cols = c0 + lax.broadcasted_iota(jnp.int32, (1, lp - c0), 1)
            s_scr[:, :c0] = s[:, :c0]
            s_scr[:, c0:] = jnp.where(cols < L, s[:, c0:], NEG_BIG)
        else:
            s_scr[...] = s
        s = s_scr[...]
        p = jnp.exp2(s - jnp.max(s, axis=-1, keepdims=True))
        l = jnp.sum(p, axis=-1, keepdims=True)
        p_scr[...] = p.astype(bf16)
        o = jnp.dot(p_scr[...], v_ref[...], preferred_element_type=f32)
        o_ref[:, g * HEAD_DIM:(g + 1) * HEAD_DIM] = (o / l).astype(o_ref.dtype)


def flash_attention(q, k, v, cfg):
    B = q.shape[0]
    tq, lp = cfg.TQ, cfg.Lp
    gw = GROUP * HEAD_DIM
    return pl.pallas_call(
        functools.partial(_flash_kernel, L=cfg.L),
        grid=(B, N_KV_HEADS, lp // tq),
        in_specs=[pl.BlockSpec((None, tq, gw), lambda b, h, i: (b, i, h)),
                  pl.BlockSpec((None, lp, HEAD_DIM), lambda b, h, i: (b, 0, h)),
                  pl.BlockSpec((None, lp, HEAD_DIM), lambda b, h, i: (b, 0, h))],
        out_specs=pl.BlockSpec((None, tq, gw), lambda b, h, i: (b, i, h)),
        out_shape=jax.ShapeDtypeStruct(q.shape, bf16),
        scratch_shapes=[pltpu.VMEM((tq, lp), f32), pltpu.VMEM((tq, lp), f32),
                        pltpu.VMEM((tq, lp), bf16), pltpu.VMEM((tq, lp), bf16)],
        compiler_params=_cparams(("arbitrary", "arbitrary", "arbitrary"), VMEM_LIMIT),
        name="flash",
    )(q, k, v)


def _rope_tables(cfg):
    n = cfg.L - N_META
    rows = n // GRID_W
    row = jnp.concatenate([jnp.full((N_META,), -1.0, f32), jnp.repeat(jnp.arange(rows, dtype=f32), GRID_W)])
    col = jnp.concatenate([jnp.arange(N_META, dtype=f32), jnp.tile(jnp.arange(GRID_W, dtype=f32), rows)])
    axis_rot = HEAD_DIM // 2
    inv_freq = ROPE_THETA ** (-jnp.arange(0, axis_rot, 2, dtype=f32) / axis_rot)
    ang = jnp.concatenate([row[:, None] * inv_freq, col[:, None] * inv_freq], axis=-1)
    ang = jnp.pad(ang, ((0, cfg.Lp - cfg.L), (0, 0)))
    c, s = jnp.cos(ang), jnp.sin(ang)
    return jnp.concatenate([c, c], axis=-1), jnp.concatenate([-s, s], axis=-1)


def _run_trunk(x, cfg, meta_tokens, hy, at, ln, ffn):
    B, n, D = x.shape
    Lp = cfg.Lp
    R = B * Lp
    meta = jnp.broadcast_to(meta_tokens[None], (B, N_META, D))
    h = jnp.concatenate([meta, x, jnp.zeros((B, Lp - cfg.L, D), x.dtype)], axis=1).reshape(R, D)
    hb = h.astype(bf16)
    cc, ss = _rope_tables(cfg)
    tables = _dft_tables(cfg)
    for i in range(DEPTH):
        j = i // 2
        if i % 2 == 0:
            hspec = hyena_filter_spectrum(cfg, tables, hy["f_w1"][j], hy["f_b1"][j], hy["f_w2"][j], hy["f_b2"][j],
                                          hy["f_w3"][j], hy["f_b3"][j], hy["f_freq"][j])
            x0c, vg = hyena_in(hb, hy["w_in"][j], hy["conv_w"][j], hy["conv_b"][j], cfg)
            yc = long_conv(vg.reshape(B, cfg.NR, D), hspec, tables, cfg)
            h, hb = hyena_out(yc.reshape(B * cfg.NR, D), vg, x0c, hy["skip"][j], hy["w_out"][j], h,
                              ln["g1"][i], ln["b1"][i], cfg)
        else:
            q, k, v = qkv_proj(hb, at["w_qkv"][j], cc, ss, at["q_gain"][j], at["k_gain"][j], cfg)
            o = flash_attention(q.reshape(B, Lp, -1), k.reshape(B, Lp, -1), v.reshape(B, Lp, -1), cfg)
            h, hb = matmul_residual_ln(o.reshape(R, D), at["w_out"][j], h, ln["g1"][i], ln["b1"][i], cfg.TL)
        gated = ffn_in_gate(hb, ffn["w_in"][i], ffn["conv_w"][i], ffn["conv_b"][i], cfg)
        h, hb = matmul_residual_ln(gated, ffn["w_out"][i], h, ln["g2"][i], ln["b2"][i], cfg.TL)
    return h.reshape(B, Lp, D)[:, N_META:cfg.L]


def kernel(x_prompt, x_sample, meta_tokens, hy_w_in, hy_conv_w, hy_conv_b, hy_filt_w1, hy_filt_b1, hy_filt_w2, hy_filt_b2, hy_filt_w3, hy_filt_b3, hy_filt_freq, hy_skip, hy_w_out, at_w_qkv, at_q_gain, at_k_gain, at_w_out, ln1_g, ln1_b, ln2_g, ln2_b, ffn_w_in, ffn_conv_w, ffn_conv_b, ffn_w_out):
    hy = dict(w_in=hy_w_in.astype(bf16), conv_w=hy_conv_w, conv_b=hy_conv_b, f_w1=hy_filt_w1, f_b1=hy_filt_b1,
              f_w2=hy_filt_w2, f_b2=hy_filt_b2, f_w3=hy_filt_w3, f_b3=hy_filt_b3, f_freq=hy_filt_freq, skip=hy_skip,
              w_out=hy_w_out.astype(bf16))
    at = dict(w_qkv=at_w_qkv.astype(bf16), q_gain=at_q_gain, k_gain=at_k_gain, w_out=at_w_out.astype(bf16))
    ln = dict(g1=ln1_g, b1=ln1_b, g2=ln2_g, b2=ln2_b)
    ffn = dict(w_in=ffn_w_in.astype(bf16), conv_w=ffn_conv_w, conv_b=ffn_conv_b, w_out=ffn_w_out.astype(bf16))
    y_prompt = _run_trunk(x_prompt, Cfg(x_prompt.shape[1]), meta_tokens, hy, at, ln, ffn)
    y_sample = _run_trunk(x_sample, Cfg(x_sample.shape[1]), meta_tokens, hy, at, ln, ffn)
    return (y_prompt, y_sample)
```

```python
import functools
import math

import jax
import jax.numpy as jnp
from jax import lax
from jax.experimental import pallas as pl
from jax.experimental.pallas import tpu as pltpu

f32 = jnp.float32
bf16 = jnp.bfloat16

D_MODEL = 1024
DEPTH = 4
N_META = 16
GRID_W = 64
HEAD_DIM = 128
N_HEADS = 8
N_KV_HEADS = 2
GROUP = N_HEADS // N_KV_HEADS
ROPE_THETA = 10000.0
FILTER_EMB_DIM = 33
FILTER_EMB_PAD = 40
FILTER_BANDS = 16
FILTER_HIDDEN = 64
DECAY_TARGET = 1e-2
FAST_DECAY_PCT = 0.3
SLOW_DECAY_PCT = 1.5
DECAY_SHIFT = 0.05
FILTER_NORM_EPS = 1e-6
D_FF = 2816
DEEPNORM_ALPHA = (2 * DEPTH) ** 0.25
LN_EPS = 1e-5
RMS_EPS = 1e-6
NEG_BIG = -1e30
LOG2E = 1.4426950408889634

LANES = 128
FFT_N2 = 128
V7X_VMEM_BYTES = 64 * 1024 * 1024
VMEM_LIMIT = 52 * 1024 * 1024


class Cfg:
    def __init__(self, n):
        self.L = n + N_META
        if n == 8192:
            self.Lp, self.TL, self.TQ, self.N1, self.NIN = 8448, 768, 256, 136, 96
        elif n == 2048:
            self.Lp, self.TL, self.TQ, self.N1, self.NIN = 2304, 768, 256, 40, 48
        else:
            up = lambda a, m: -(-a // m) * m
            self.TL = self.TQ = 128
            self.Lp = up(self.L, 128)
            self.N1 = up(-(-(2 * self.L - 1) // FFT_N2), 8)
            self.NIN = up(-(-self.Lp // FFT_N2), 16)
        self.NR = self.NIN * FFT_N2
        self.N1H = -(-(self.N1 // 2 + 1) // 8) * 8
        self.N = self.N1 * FFT_N2
        assert self.N >= 2 * self.L - 1 and self.NR >= self.Lp
        assert self.Lp % self.TL == 0 and self.Lp % self.TQ == 0 and self.NR % self.TL == 0


def _cparams(sem, vmem=None):
    return pltpu.CompilerParams(dimension_semantics=sem, vmem_limit_bytes=vmem)


def _residual_ln(m, h_ref, g_ref, b_ref, o_ref, obf_ref):
    y = DEEPNORM_ALPHA * h_ref[...] + m
    mu = jnp.mean(y, axis=-1, keepdims=True)
    yc = y - mu
    var = jnp.mean(yc * yc, axis=-1, keepdims=True)
    out = yc * lax.rsqrt(var + LN_EPS) * g_ref[...] + b_ref[...]
    o_ref[...] = out
    obf_ref[...] = out.astype(bf16)


def _mm_ln_kernel(x_ref, w_ref, h_ref, g_ref, b_ref, o_ref, obf_ref):
    half = x_ref.shape[0] // 2
    for r in (0, half):
        rows = pl.ds(r, half)
        _residual_ln(jnp.dot(x_ref[rows, :], w_ref[...], preferred_element_type=f32),
                     h_ref.at[rows, :], g_ref, b_ref, o_ref.at[rows, :], obf_ref.at[rows, :])


def matmul_residual_ln(x, w, h, g, b, tm):
    R, K = x.shape
    N = w.shape[1]
    row = lambda i: (i, 0)
    fixed = lambda i: (0, 0)
    return pl.pallas_call(
        _mm_ln_kernel,
        grid=(R // tm,),
        in_specs=[pl.BlockSpec((tm, K), row), pl.BlockSpec((K, N), fixed), pl.BlockSpec((tm, N), row),
                  pl.BlockSpec((1, N), fixed), pl.BlockSpec((1, N), fixed)],
        out_specs=[pl.BlockSpec((tm, N), row), pl.BlockSpec((tm, N), row)],
        out_shape=[jax.ShapeDtypeStruct((R, N), f32), jax.ShapeDtypeStruct((R, N), bf16)],
        compiler_params=_cparams(("arbitrary",), VMEM_LIMIT),
        name="mm_ln",
    )(x, w, h, g.reshape(1, N), b.reshape(1, N))


def _left_mm_kernel(a_ref, x_ref, o_ref, *, cw):
    a = a_ref[...]
    for c in range(0, x_ref.shape[1], cw):
        o_ref[:, c:c + cw] = jnp.dot(a, x_ref[:, c:c + cw], preferred_element_type=f32).astype(o_ref.dtype)


def left_matmul(a, x, out_dtype, tn=32768, cw=512):
    M, K = a.shape
    B, _, C = x.shape
    assert C % tn == 0 and tn % cw == 0
    return pl.pallas_call(
        functools.partial(_left_mm_kernel, cw=cw),
        grid=(B, C // tn),
        in_specs=[pl.BlockSpec((M, K), lambda b, j: (0, 0)), pl.BlockSpec((None, K, tn), lambda b, j: (b, 0, j))],
        out_specs=pl.BlockSpec((None, M, tn), lambda b, j: (b, 0, j)),
        out_shape=jax.ShapeDtypeStruct((B, M, C), out_dtype),
        compiler_params=_cparams(("arbitrary", "arbitrary"), VMEM_LIMIT),
        name="left_mm",
    )(a, x)


HALO = 16


def _halo_specs(tm, K, R, row_block):
    nh = tm // HALO
    last = R // HALO - 1
    return [pl.BlockSpec((tm, K), lambda *g: (row_block(*g), 0)),
            pl.BlockSpec((HALO, K), lambda *g: (jnp.maximum(row_block(*g) * nh - 1, 0), 0)),
            pl.BlockSpec((HALO, K), lambda *g: (jnp.minimum((row_block(*g) + 1) * nh, last), 0))]


def _fill_halo_tile(x_scr, x_ref, xp_ref, xn_ref, tm):
    x_scr[0:HALO, :] = xp_ref[...]
    x_scr[HALO:HALO + tm, :] = x_ref[...]
    x_scr[HALO + tm:, :] = xn_ref[...]


def _zero_rows_outside_sequence(u_scr, j, L, tm):
    assert L % tm != 0
    j_end, r_end = L // tm, HALO + L % tm
    g_end = (r_end // 8) * 8
    sub8 = lax.broadcasted_iota(jnp.int32, (8, 1), 0)
    u_scr[HALO - 8:HALO, :] = u_scr[HALO - 8:HALO, :] * jnp.where((j == 0) & (sub8 == 7), 0.0, 1.0)
    u_scr[g_end:g_end + 8, :] = u_scr[g_end:g_end + 8, :] * jnp.where((j == j_end) & (sub8 == r_end - g_end), 0.0, 1.0)


def _conv3_centre(u_scr, cw, cb, tm):
    return (u_scr[HALO - 1:HALO - 1 + tm, :] * cw[0:1, :] + u_scr[HALO:HALO + tm, :] * cw[1:2, :]
            + u_scr[HALO + 1:HALO + 1 + tm, :] * cw[2:3, :] + cb)


def _hyena_in_kernel(x_ref, xp_ref, xn_ref, w_ref, cw_ref, cb_ref, x0_ref, vg_ref, x_scr, u0_scr, u1_scr, uv_scr,
                     *, L, tm, tpb):
    j = pl.program_id(1)
    D = D_MODEL

    @pl.when(j < tpb)
    def _():
        _fill_halo_tile(x_scr, x_ref, xp_ref, xn_ref, tm)
        xs = x_scr[...]
        t = j * tm + lax.broadcasted_iota(jnp.int32, (tm, 1), 0)
        cw, cb = cw_ref[...], cb_ref[...]
        conv = []
        for s, scr in enumerate((u0_scr, u1_scr, uv_scr)):
            sl = slice(s * D, (s + 1) * D)
            scr[...] = jnp.dot(xs, w_ref[:, sl], preferred_element_type=f32)
            _zero_rows_outside_sequence(scr, j, L, tm)
            conv.append(_conv3_centre(scr, cw[:, sl], cb[:, sl], tm))
        x0_ref[...] = conv[0].astype(x0_ref.dtype)
        vg_ref[...] = jnp.where(t < L, conv[1] * conv[2], 0.0).astype(vg_ref.dtype)

    @pl.when(j >= tpb)
    def _():
        vg_ref[...] = jnp.zeros_like(vg_ref)


def hyena_in(x, w_in, conv_w, conv_b, cfg):
    R, K = x.shape
    D = D_MODEL
    tm, tpb, ntr = cfg.TL, cfg.Lp // cfg.TL, cfg.NR // cfg.TL
    B = R // cfg.Lp
    row_block = lambda b, j: b * tpb + jnp.minimum(j, tpb - 1)
    fixed = lambda b, j: (0, 0)
    return pl.pallas_call(
        functools.partial(_hyena_in_kernel, L=cfg.L, tm=tm, tpb=tpb),
        grid=(B, ntr),
        in_specs=_halo_specs(tm, K, R, row_block) +
                 [pl.BlockSpec((K, 3 * D), fixed, pipeline_mode=pl.Buffered(1)),
                  pl.BlockSpec((3, 3 * D), fixed), pl.BlockSpec((1, 3 * D), fixed)],
        out_specs=[pl.BlockSpec((tm, D), lambda b, j: (row_block(b, j), 0)),
                   pl.BlockSpec((tm, D), lambda b, j: (b * ntr + j, 0))],
        out_shape=[jax.ShapeDtypeStruct((R, D), bf16), jax.ShapeDtypeStruct((B * cfg.NR, D), bf16)],
        scratch_shapes=[pltpu.VMEM((tm + 2 * HALO, K), bf16)] + [pltpu.VMEM((tm + 2 * HALO, D), f32)] * 3,
        compiler_params=_cparams(("arbitrary", "arbitrary"), VMEM_LIMIT),
        name="hyena_in",
    )(x, x, x, w_in, conv_w, conv_b.reshape(1, -1))


def _hyena_out_kernel(y_ref, vg_ref, x0_ref, skip_ref, w_ref, h_ref, g_ref, b_ref, o_ref, obf_ref):
    y = y_ref[...].astype(f32) + vg_ref[...].astype(f32) * skip_ref[...]
    x = (y * x0_ref[...].astype(f32)).astype(bf16)
    _residual_ln(jnp.dot(x, w_ref[...], preferred_element_type=f32), h_ref, g_ref, b_ref, o_ref, obf_ref)


def hyena_out(yc, vg, x0, skip, w, h, g, b, cfg):
    R, N = h.shape
    tm, tpb, ntr = cfg.TL, cfg.Lp // cfg.TL, cfg.NR // cfg.TL
    row = lambda i: (i, 0)
    padded_row = lambda i: ((i // tpb) * ntr + i % tpb, 0)
    fixed = lambda i: (0, 0)
    return pl.pallas_call(
        _hyena_out_kernel,
        grid=(R // tm,),
        in_specs=[pl.BlockSpec((tm, N), padded_row), pl.BlockSpec((tm, N), padded_row), pl.BlockSpec((tm, N), row),
                  pl.BlockSpec((1, N), fixed), pl.BlockSpec((N, N), fixed), pl.BlockSpec((tm, N), row),
                  pl.BlockSpec((1, N), fixed), pl.BlockSpec((1, N), fixed)],
        out_specs=[pl.BlockSpec((tm, N), row), pl.BlockSpec((tm, N), row)],
        out_shape=[jax.ShapeDtypeStruct((R, N), f32), jax.ShapeDtypeStruct((R, N), bf16)],
        compiler_params=_cparams(("arbitrary",), VMEM_LIMIT),
        name="hyena_out",
    )(yc, vg, x0, skip.reshape(1, N), w, h, g.reshape(1, N), b.reshape(1, N))


def _ffn_in_gate_kernel(x_ref, xp_ref, xn_ref, wg_ref, wa_ref, cw_ref, cb_ref, o_ref, x_scr, g_scr, *, L, tm, tpb):
    i, c = pl.program_id(0), pl.program_id(1)

    @pl.when(c == 0)
    def _():
        _fill_halo_tile(x_scr, x_ref, xp_ref, xn_ref, tm)

    g_scr[...] = jnp.dot(x_scr[...], wg_ref[...], preferred_element_type=f32)
    a = jnp.dot(x_ref[...], wa_ref[...], preferred_element_type=f32)
    _zero_rows_outside_sequence(g_scr, i % tpb, L, tm)
    g = _conv3_centre(g_scr, cw_ref[...], cb_ref[...], tm)
    gelu = 0.5 * g * (1.0 + lax.erf(g * (2.0 ** -0.5)))
    o_ref[...] = (gelu * a).astype(o_ref.dtype)


def ffn_in_gate(x, w_in, conv_w, conv_b, cfg, nchunk=1):
    R, K = x.shape
    tm = cfg.TL
    tc = D_FF // nchunk
    tpb = cfg.Lp // tm
    assert D_FF % nchunk == 0 and tc % LANES == 0 and tm % HALO == 0
    wmode = dict(pipeline_mode=pl.Buffered(1)) if nchunk == 1 else {}
    return pl.pallas_call(
        functools.partial(_ffn_in_gate_kernel, L=cfg.L, tm=tm, tpb=tpb),
        grid=(R // tm, nchunk),
        in_specs=_halo_specs(tm, K, R, lambda i, c: i) + [
                  pl.BlockSpec((K, tc), lambda i, c: (0, c), **wmode),
                  pl.BlockSpec((K, tc), lambda i, c: (0, nchunk + c), **wmode),
                  pl.BlockSpec((3, tc), lambda i, c: (0, c)), pl.BlockSpec((1, tc), lambda i, c: (0, c))],
        out_specs=pl.BlockSpec((tm, tc), lambda i, c: (i, c)),
        out_shape=jax.ShapeDtypeStruct((R, D_FF), bf16),
        scratch_shapes=[pltpu.VMEM((tm + 2 * HALO, K), bf16), pltpu.VMEM((tm + 2 * HALO, tc), f32)],
        compiler_params=_cparams(("arbitrary", "arbitrary"), VMEM_LIMIT),
        name="ffn_in_gate",
    )(x, x, x, w_in, w_in, conv_w, conv_b.reshape(1, -1))


def _dft_tables(cfg):
    N1, NIN, N2, N = cfg.N1, cfg.NIN, FFT_N2, cfg.N
    assert N1 % 2 == 0
    k1 = jnp.arange(cfg.N1H, dtype=jnp.int32)
    wk = jnp.where((k1 == 0) | (k1 == N1 // 2), 1.0, jnp.where(k1 < N1 // 2, 2.0, 0.0)).astype(f32)
    n1 = jnp.arange(NIN, dtype=jnp.int32)
    ang = (2.0 * math.pi / N1) * ((k1[:, None] * n1[None, :]) % N1).astype(f32)
    ca, sa = jnp.cos(ang), jnp.sin(ang)
    fa = jnp.concatenate([ca, -sa], axis=0)
    fa_inv = jnp.concatenate([ca.T * wk, -sa.T * wk], axis=1) * (1.0 / N)
    k2 = jnp.arange(N2, dtype=jnp.int32)[:, None]
    n2 = jnp.arange(N2, dtype=jnp.int32)[None, :]
    t1 = (2.0 * math.pi / N2) * ((n2 * k2) % N2).astype(f32)
    t2 = (2.0 * math.pi / N) * ((n2 * k1[:, None]) % N).astype(f32)
    c1, s1 = jnp.cos(t1)[None], jnp.sin(t1)[None]
    c2, s2 = jnp.cos(t2)[:, None, :], jnp.sin(t2)[:, None, :]
    ar, ai = c1 * c2 - s1 * s2, -(s1 * c2 + c1 * s2)
    g_fwd = jnp.concatenate([jnp.concatenate([ar, -ai], axis=2), jnp.concatenate([ai, ar], axis=2)], axis=1)
    art, ait = jnp.swapaxes(ar, 1, 2), -jnp.swapaxes(ai, 1, 2)
    g_inv = jnp.concatenate([jnp.concatenate([art, -ait], axis=2), jnp.concatenate([ait, art], axis=2)], axis=1)
    return fa.astype(bf16), fa_inv.astype(bf16), g_fwd.astype(bf16), g_inv.astype(bf16)


FFT_KB = 4


def _fft_b_fused_kernel(z_ref, h_ref, gf_ref, gi_ref, o_ref):
    for kb in range(FFT_KB):
        zc = jnp.concatenate([z_ref[0, kb], z_ref[1, kb]], axis=0)
        y = jnp.dot(gf_ref[kb], zc, preferred_element_type=f32)
        yr, yi = y[:FFT_N2], y[FFT_N2:]
        hr, hi = h_ref[0, kb], h_ref[1, kb]
        pc = jnp.concatenate([yr * hr - yi * hi, yr * hi + yi * hr], axis=0).astype(bf16)
        q = jnp.dot(gi_ref[kb], pc, preferred_element_type=f32)
        o_ref[0, kb] = q[:FFT_N2].astype(o_ref.dtype)
        o_ref[1, kb] = q[FFT_N2:].astype(o_ref.dtype)


def fft_b_fused(z5, hspec, g_fwd, g_inv):
    B, _, N1, N2, D = z5.shape
    assert N1 % FFT_KB == 0
    zspec = pl.BlockSpec((None, 2, FFT_KB, N2, D), lambda k, b: (b, 0, k, 0, 0))
    gspec = pl.BlockSpec((FFT_KB, 2 * N2, 2 * N2), lambda k, b: (k, 0, 0))
    return pl.pallas_call(
        _fft_b_fused_kernel,
        grid=(N1 // FFT_KB, B),
        in_specs=[zspec, pl.BlockSpec((2, FFT_KB, N2, D), lambda k, b: (0, k, 0, 0)), gspec, gspec],
        out_specs=zspec,
        out_shape=jax.ShapeDtypeStruct(z5.shape, bf16),
        compiler_params=_cparams(("arbitrary", "arbitrary"), VMEM_LIMIT),
        name="fft_b_fused",
    )(z5, hspec, g_fwd, g_inv)


def _fft_b_filter_kernel(z_ref, csum_ref, gf_ref, o_ref):
    D = D_MODEL
    zc = z_ref[...].reshape(2 * FFT_N2, 2 * D)
    y = jnp.dot(gf_ref[...], zc, preferred_element_type=f32)
    scale = 1.0 / (csum_ref[...] + FILTER_NORM_EPS)
    y = y * scale
    yr, yi = y[:FFT_N2], y[FFT_N2:]
    o_ref[0] = yr[:, :D] + yr[:, D:]
    o_ref[1] = yi[:, :D] - yi[:, D:]


def fft_b_filter(z5, csum, g_fwd):
    _, N1, N2, D2 = z5.shape
    return pl.pallas_call(
        _fft_b_filter_kernel,
        grid=(N1,),
        in_specs=[pl.BlockSpec((2, None, N2, D2), lambda k: (0, k, 0, 0)), pl.BlockSpec((1, D2), lambda k: (0, 0)),
                  pl.BlockSpec((None, 2 * N2, 2 * N2), lambda k: (k, 0, 0))],
        out_specs=pl.BlockSpec((2, None, N2, D2 // 2), lambda k: (0, k, 0, 0)),
        out_shape=jax.ShapeDtypeStruct((2, N1, N2, D2 // 2), f32),
        compiler_params=_cparams(("arbitrary",), VMEM_LIMIT),
        name="fft_b_filter",
    )(z5, csum, g_fwd)


def _filter_mlp_kernel(z_ref, w1_ref, b1_ref, w2_ref, b2_ref, w3_ref, b3_ref, fr_ref, dl_ref, o_ref, cs_ref, *, L, tf):
    i = pl.program_id(0)
    hp = lax.Precision.HIGHEST
    z = z_ref[...]
    fr = fr_ref[...]
    h = jnp.sin(fr * (jnp.dot(z, w1_ref[...], preferred_element_type=f32, precision=hp) + b1_ref[...]))
    h = jnp.sin(fr * (jnp.dot(h, w2_ref[...], preferred_element_type=f32, precision=hp) + b2_ref[...]))
    h = jnp.dot(h.astype(bf16), w3_ref[...].astype(bf16), preferred_element_type=f32) + b3_ref[...]
    win = jnp.exp(-z[:, 0:1] * dl_ref[...]) + DECAY_SHIFT
    h = h * jnp.concatenate([win, win], axis=1)
    rows = i * tf + lax.broadcasted_iota(jnp.int32, (tf, 1), 0)
    h = jnp.where(rows < L, h, 0.0)

    @pl.when(i == 0)
    def _():
        cs_ref[...] = jnp.zeros_like(cs_ref)

    cs_ref[...] += jnp.sum(jnp.abs(h), axis=0, keepdims=True)
    lanes = lax.broadcasted_iota(jnp.int32, (1, h.shape[1]), 1)
    o_ref[...] = jnp.where((rows == 0) & (lanes >= D_MODEL), 0.0, h).astype(o_ref.dtype)


def filter_mlp(z, w1, b1, w2, b2, w3, b3, freq, deltas, cfg, tf=512):
    NR = cfg.NR
    D2 = 2 * D_MODEL
    fixed = lambda i: (0, 0)
    full = lambda a: pl.BlockSpec(a.shape, fixed)
    args = (w1, b1.reshape(1, -1), w2, b2.reshape(1, -1), w3, b3.reshape(1, -1), freq.reshape(1, -1), deltas.reshape(1, -1))
    return pl.pallas_call(
        functools.partial(_filter_mlp_kernel, L=cfg.L, tf=tf),
        grid=(NR // tf,),
        in_specs=[pl.BlockSpec((tf, FILTER_EMB_PAD), lambda i: (i, 0))] + [full(a) for a in args],
        out_specs=[pl.BlockSpec((tf, D2), lambda i: (i, 0)), pl.BlockSpec((1, D2), fixed)],
        out_shape=[jax.ShapeDtypeStruct((NR, D2), bf16), jax.ShapeDtypeStruct((1, D2), f32)],
        compiler_params=_cparams(("arbitrary",), VMEM_LIMIT),
        name="filter_mlp",
    )(z, *args)


def _filter_features(cfg):
    L = cfg.L
    pos = jnp.arange(cfg.NR, dtype=f32)
    t = (pos / (L - 1))[:, None]
    w = (2.0 * math.pi / L) * pos[:, None]
    bands = jnp.linspace(1e-4, FILTER_BANDS - 1, FILTER_BANDS, dtype=f32)
    z = jnp.concatenate([t, jnp.cos(w * bands), -jnp.sin(w * bands)], axis=-1)
    return jnp.pad(z, ((0, 0), (0, FILTER_EMB_PAD - FILTER_EMB_DIM)))


def hyena_filter_spectrum(cfg, tables, w1, b1, w2, b2, w3, b3, freq):
    fa, _, g_fwd, _ = tables
    z = _filter_features(cfg)
    w1p = jnp.pad(w1, ((0, FILTER_EMB_PAD - FILTER_EMB_DIM), (0, 0)))
    max_decay = math.log(1.0 / DECAY_TARGET) / FAST_DECAY_PCT
    min_decay = math.log(1.0 / DECAY_TARGET) / SLOW_DECAY_PCT
    deltas = jnp.linspace(min_decay, max_decay, D_MODEL, dtype=f32)
    xf, csum = filter_mlp(z, w1p, b1, w2, b2, w3, b3, freq, deltas, cfg)
    D2 = 2 * D_MODEL
    z2 = left_matmul(fa, xf.reshape(1, cfg.NIN, FFT_N2 * D2), bf16)
    z5 = z2.reshape(2, cfg.N1H, FFT_N2, D2)
    return fft_b_filter(z5, csum, g_fwd)


def long_conv(vg, hspec, tables, cfg):
    fa, fa_inv, g_fwd, g_inv = tables
    B = vg.shape[0]
    D = D_MODEL
    z2 = left_matmul(fa, vg.reshape(B, cfg.NIN, FFT_N2 * D), bf16)
    q5 = fft_b_fused(z2.reshape(B, 2, cfg.N1H, FFT_N2, D), hspec, g_fwd, g_inv)
    y2 = left_matmul(fa_inv, q5.reshape(B, 2 * cfg.N1H, FFT_N2 * D), bf16)
    return y2.reshape(B, cfg.NR, D)


def _qkv_proj_kernel(x_ref, w_ref, cc_ref, ss_ref, qg_ref, kg_ref, q_ref, k_ref, v_ref, *, L, tm, tpb, sub):
    nq, nk = N_HEADS * HEAD_DIM, N_KV_HEADS * HEAD_DIM
    for r in range(0, tm, sub):
        rows = slice(r, r + sub)
        t = (pl.program_id(0) % tpb) * tm + r + lax.broadcasted_iota(jnp.int32, (sub, 1), 0)
        valid = t < L
        cc, ss = cc_ref[rows, :], ss_ref[rows, :]
        qkv = jnp.dot(x_ref[rows, :], w_ref[...], preferred_element_type=f32)

        def norm_rope(x, gain, scale):
            xn = x * lax.rsqrt(jnp.mean(x * x, axis=-1, keepdims=True) + RMS_EPS) * gain
            y = xn * cc + pltpu.roll(xn, HEAD_DIM // 2, axis=1) * ss
            return jnp.where(valid, y * scale, 0.0).astype(bf16)

        for hh in range(N_HEADS):
            sl = slice(hh * HEAD_DIM, (hh + 1) * HEAD_DIM)
            q_ref[rows, sl] = norm_rope(qkv[:, sl], qg_ref[...], LOG2E * HEAD_DIM ** -0.5)
        for hh in range(N_KV_HEADS):
            sl = slice(hh * HEAD_DIM, (hh + 1) * HEAD_DIM)
            k_ref[rows, sl] = norm_rope(qkv[:, nq + hh * HEAD_DIM:nq + (hh + 1) * HEAD_DIM], kg_ref[...], 1.0)
        v_ref[rows, :] = jnp.where(valid, qkv[:, nq + nk:], 0.0).astype(bf16)


def qkv_proj(x, w, cc, ss, q_gain, k_gain, cfg):
    R, K = x.shape
    tm, tpb = cfg.TL, cfg.Lp // cfg.TL
    nq, nk = N_HEADS * HEAD_DIM, N_KV_HEADS * HEAD_DIM
    sub = 256 if tm % 256 == 0 else tm // 2
    assert tm % sub == 0 and sub % HALO == 0
    row = lambda i: (i, 0)
    tab = lambda i: (i % tpb, 0)
    fixed = lambda i: (0, 0)
    return pl.pallas_call(
        functools.partial(_qkv_proj_kernel, L=cfg.L, tm=tm, tpb=tpb, sub=sub),
        grid=(R // tm,),
        in_specs=[pl.BlockSpec((tm, K), row), pl.BlockSpec((K, nq + 2 * nk), fixed), pl.BlockSpec((tm, HEAD_DIM), tab),
                  pl.BlockSpec((tm, HEAD_DIM), tab), pl.BlockSpec((1, HEAD_DIM), fixed), pl.BlockSpec((1, HEAD_DIM), fixed)],
        out_specs=[pl.BlockSpec((tm, nq), row), pl.BlockSpec((tm, nk), row), pl.BlockSpec((tm, nk), row)],
        out_shape=[jax.ShapeDtypeStruct((R, nq), bf16), jax.ShapeDtypeStruct((R, nk), bf16),
                   jax.ShapeDtypeStruct((R, nk), bf16)],
        compiler_params=_cparams(("arbitrary",), VMEM_LIMIT),
        name="qkv_proj",
    )(x, w, cc, ss, q_gain.reshape(1, -1), k_gain.reshape(1, -1))


def _flash_kernel(q_ref, k_ref, v_ref, o_ref, s0_scr, s1_scr, p0_scr, p1_scr, *, L):
    s_bufs, p_bufs = (s0_scr, s1_scr), (p0_scr, p1_scr)
    lp = k_ref.shape[0]
    c0 = (L // LANES) * LANES
    for g in range(GROUP):
        s_scr, p_scr = s_bufs[g % 2], p_bufs[g % 2]
        q = q_ref[:, g * HEAD_DIM:(g + 1) * HEAD_DIM]
        s = lax.dot_general(q, k_ref[...], (((1,), (1,)), ((), ())), preferred_element_type=f32)
        if c0 < lp:
            cols = c0 + lax.broadcasted_iota(jnp.int32, (1, lp - c0), 1)
            s_scr[:, :c0] = s[:, :c0]
            s_scr[:, c0:] = jnp.where(cols < L, s[:, c0:], NEG_BIG)
        else:
            s_scr[...] = s
        s = s_scr[...]
        p = jnp.exp2(s - jnp.max(s, axis=-1, keepdims=True))
        l = jnp.sum(p, axis=-1, keepdims=True)
        p_scr[...] = p.astype(bf16)
        o = jnp.dot(p_scr[...], v_ref[...], preferred_element_type=f32)
        o_ref[:, g * HEAD_DIM:(g + 1) * HEAD_DIM] = (o / l).astype(o_ref.dtype)


def flash_attention(q, k, v, cfg):
    B = q.shape[0]
    tq, lp = cfg.TQ, cfg.Lp
    gw = GROUP * HEAD_DIM
    return pl.pallas_call(
        functools.partial(_flash_kernel, L=cfg.L),
        grid=(B, N_KV_HEADS, lp // tq),
        in_specs=[pl.BlockSpec((None, tq, gw), lambda b, h, i: (b, i, h)),
                  pl.BlockSpec((None, lp, HEAD_DIM), lambda b, h, i: (b, 0, h)),
                  pl.BlockSpec((None, lp, HEAD_DIM), lambda b, h, i: (b, 0, h))],
        out_specs=pl.BlockSpec((None, tq, gw), lambda b, h, i: (b, i, h)),
        out_shape=jax.ShapeDtypeStruct(q.shape, bf16),
        scratch_shapes=[pltpu.VMEM((tq, lp), f32), pltpu.VMEM((tq, lp), f32),
                        pltpu.VMEM((tq, lp), bf16), pltpu.VMEM((tq, lp), bf16)],
        compiler_params=_cparams(("arbitrary", "arbitrary", "arbitrary"), VMEM_LIMIT),
        name="flash",
    )(q, k, v)


def _rope_tables(cfg):
    n = cfg.L - N_META
    rows = n // GRID_W
    row = jnp.concatenate([jnp.full((N_META,), -1.0, f32), jnp.repeat(jnp.arange(rows, dtype=f32), GRID_W)])
    col = jnp.concatenate([jnp.arange(N_META, dtype=f32), jnp.tile(jnp.arange(GRID_W, dtype=f32), rows)])
    axis_rot = HEAD_DIM // 2
    inv_freq = ROPE_THETA ** (-jnp.arange(0, axis_rot, 2, dtype=f32) / axis_rot)
    ang = jnp.concatenate([row[:, None] * inv_freq, col[:, None] * inv_freq], axis=-1)
    ang = jnp.pad(ang, ((0, cfg.Lp - cfg.L), (0, 0)))
    c, s = jnp.cos(ang), jnp.sin(ang)
    return jnp.concatenate([c, c], axis=-1), jnp.concatenate([-s, s], axis=-1)


def _run_trunk(x, cfg, meta_tokens, hy, at, ln, ffn):
    B, n, D = x.shape
    Lp = cfg.Lp
    R = B * Lp
    meta = jnp.broadcast_to(meta_tokens[None], (B, N_META, D))
    h = jnp.concatenate([meta, x, jnp.zeros((B, Lp - cfg.L, D), x.dtype)], axis=1).reshape(R, D)
    hb = h.astype(bf16)
    cc, ss = _rope_tables(cfg)
    tables = _dft_tables(cfg)
    for i in range(DEPTH):
        j = i // 2
        if i % 2 == 0:
            hspec = hyena_filter_spectrum(cfg, tables, hy["f_w1"][j], hy["f_b1"][j], hy["f_w2"][j], hy["f_b2"][j],
                                          hy["f_w3"][j], hy["f_b3"][j], hy["f_freq"][j])
            x0c, vg = hyena_in(hb, hy["w_in"][j], hy["conv_w"][j], hy["conv_b"][j], cfg)
            yc = long_conv(vg.reshape(B, cfg.NR, D), hspec, tables, cfg)
            h, hb = hyena_out(yc.reshape(B * cfg.NR, D), vg, x0c, hy["skip"][j], hy["w_out"][j], h,
                              ln["g1"][i], ln["b1"][i], cfg)
        else:
            q, k, v = qkv_proj(hb, at["w_qkv"][j], cc, ss, at["q_gain"][j], at["k_gain"][j], cfg)
            o = flash_attention(q.reshape(B, Lp, -1), k.reshape(B, Lp, -1), v.reshape(B, Lp, -1), cfg)
            h, hb = matmul_residual_ln(o.reshape(R, D), at["w_out"][j], h, ln["g1"][i], ln["b1"][i], cfg.TL)
        gated = ffn_in_gate(hb, ffn["w_in"][i], ffn["conv_w"][i], ffn["conv_b"][i], cfg)
        h, hb = matmul_residual_ln(gated, ffn["w_out"][i], h, ln["g2"][i], ln["b2"][i], cfg.TL)
    return h.reshape(B, Lp, D)[:, N_META:cfg.L]


def kernel(x_prompt, x_sample, meta_tokens, hy_w_in, hy_conv_w, hy_conv_b, hy_filt_w1, hy_filt_b1, hy_filt_w2, hy_filt_b2, hy_filt_w3, hy_filt_b3, hy_filt_freq, hy_skip, hy_w_out, at_w_qkv, at_q_gain, at_k_gain, at_w_out, ln1_g, ln1_b, ln2_g, ln2_b, ffn_w_in, ffn_conv_w, ffn_conv_b, ffn_w_out):
    hy = dict(w_in=hy_w_in.astype(bf16), conv_w=hy_conv_w, conv_b=hy_conv_b, f_w1=hy_filt_w1, f_b1=hy_filt_b1,
              f_w2=hy_filt_w2, f_b2=hy_filt_b2, f_w3=hy_filt_w3, f_b3=hy_filt_b3, f_freq=hy_filt_freq, skip=hy_skip,
              w_out=hy_w_out.astype(bf16))
    at = dict(w_qkv=at_w_qkv.astype(bf16), q_gain=at_q_gain, k_gain=at_k_gain, w_out=at_w_out.astype(bf16))
    ln = dict(g1=ln1_g, b1=ln1_b, g2=ln2_g, b2=ln2_b)
    ffn = dict(w_in=ffn_w_in.astype(bf16), conv_w=ffn_conv_w, conv_b=ffn_conv_b, w_out=ffn_w_out.astype(bf16))
    y_prompt = _run_trunk(x_prompt, Cfg(x_prompt.shape[1]), meta_tokens, hy, at, ln, ffn)
    y_sample = _run_trunk(x_sample, Cfg(x_sample.shape[1]), meta_tokens, hy, at, ln, ffn)
    return (y_prompt, y_sample)
```

```python
import functools
import math

import jax
import jax.numpy as jnp
from jax import lax
from jax.experimental import pallas as pl
from jax.experimental.pallas import tpu as pltpu

f32 = jnp.float32
bf16 = jnp.bfloat16

D_MODEL = 1024
DEPTH = 4
N_META = 16
GRID_W = 64
HEAD_DIM = 128
N_HEADS = 8
N_KV_HEADS = 2
GROUP = N_HEADS // N_KV_HEADS
ROPE_THETA = 10000.0
FILTER_EMB_DIM = 33
FILTER_EMB_PAD = 40
FILTER_BANDS = 16
FILTER_HIDDEN = 64
DECAY_TARGET = 1e-2
FAST_DECAY_PCT = 0.3
SLOW_DECAY_PCT = 1.5
DECAY_SHIFT = 0.05
FILTER_NORM_EPS = 1e-6
D_FF = 2816
DEEPNORM_ALPHA = (2 * DEPTH) ** 0.25
LN_EPS = 1e-5
RMS_EPS = 1e-6
NEG_BIG = -1e30
LOG2E = 1.4426950408889634

LANES = 128
FFT_N2 = 128
V7X_VMEM_BYTES = 64 * 1024 * 1024
VMEM_LIMIT = 52 * 1024 * 1024


class Cfg:
    def __init__(self, n):
        self.L = n + N_META
        if n == 8192:
            self.Lp, self.TL, self.TQ, self.N1, self.NIN = 8448, 768, 256, 136, 80
        elif n == 2048:
            self.Lp, self.TL, self.TQ, self.N1, self.NIN = 2304, 768, 256, 40, 32
        else:
            up = lambda a, m: -(-a // m) * m
            self.TL = self.TQ = 128
            self.Lp = up(self.L, 128)
            self.N1 = up(-(-(2 * self.L - 1) // FFT_N2), 8)
            self.NIN = up(-(-self.Lp // FFT_N2), 16)
        self.NR = self.NIN * FFT_N2
        self.N1H = -(-(self.N1 // 2 + 1) // 8) * 8
        self.N = self.N1 * FFT_N2
        assert self.N >= 2 * self.L - 1 and self.NR >= self.Lp
        assert self.Lp % self.TL == 0 and self.Lp % self.TQ == 0


def _cparams(sem, vmem=None):
    return pltpu.CompilerParams(dimension_semantics=sem, vmem_limit_bytes=vmem)


def _residual_ln(m, h_ref, g_ref, b_ref, o_ref, obf_ref):
    y = DEEPNORM_ALPHA * h_ref[...] + m
    mu = jnp.mean(y, axis=-1, keepdims=True)
    yc = y - mu
    var = jnp.mean(yc * yc, axis=-1, keepdims=True)
    out = yc * lax.rsqrt(var + LN_EPS) * g_ref[...] + b_ref[...]
    o_ref[...] = out
    obf_ref[...] = out.astype(bf16)


def _mm_ln_kernel(x_ref, w_ref, h_ref, g_ref, b_ref, o_ref, obf_ref):
    half = x_ref.shape[0] // 2
    for r in (0, half):
        rows = pl.ds(r, half)
        _residual_ln(jnp.dot(x_ref[rows, :], w_ref[...], preferred_element_type=f32),
                     h_ref.at[rows, :], g_ref, b_ref, o_ref.at[rows, :], obf_ref.at[rows, :])


def matmul_residual_ln(x, w, h, g, b, tm):
    R, K = x.shape
    N = w.shape[1]
    row = lambda i: (i, 0)
    fixed = lambda i: (0, 0)
    return pl.pallas_call(
        _mm_ln_kernel,
        grid=(R // tm,),
        in_specs=[pl.BlockSpec((tm, K), row), pl.BlockSpec((K, N), fixed), pl.BlockSpec((tm, N), row),
                  pl.BlockSpec((1, N), fixed), pl.BlockSpec((1, N), fixed)],
        out_specs=[pl.BlockSpec((tm, N), row), pl.BlockSpec((tm, N), row)],
        out_shape=[jax.ShapeDtypeStruct((R, N), f32), jax.ShapeDtypeStruct((R, N), bf16)],
        compiler_params=_cparams(("arbitrary",), VMEM_LIMIT),
        name="mm_ln",
    )(x, w, h, g.reshape(1, N), b.reshape(1, N))


def _left_mm_kernel(a_ref, x_ref, o_ref, *, cw):
    a = a_ref[...]
    for c in range(0, x_ref.shape[1], cw):
        o_ref[:, c:c + cw] = jnp.dot(a, x_ref[:, c:c + cw], preferred_element_type=f32).astype(o_ref.dtype)


def left_matmul(a, x, out_dtype, tn=32768, cw=512):
    M, K = a.shape
    B, _, C = x.shape
    assert C % tn == 0 and tn % cw == 0
    return pl.pallas_call(
        functools.partial(_left_mm_kernel, cw=cw),
        grid=(B, C // tn),
        in_specs=[pl.BlockSpec((M, K), lambda b, j: (0, 0)), pl.BlockSpec((None, K, tn), lambda b, j: (b, 0, j))],
        out_specs=pl.BlockSpec((None, M, tn), lambda b, j: (b, 0, j)),
        out_shape=jax.ShapeDtypeStruct((B, M, C), out_dtype),
        compiler_params=_cparams(("arbitrary", "arbitrary"), VMEM_LIMIT),
        name="left_mm",
    )(a, x)


HALO = 16


def _halo_specs(tm, K, R, row_block):
    nh = tm // HALO
    last = R // HALO - 1
    return [pl.BlockSpec((tm, K), lambda *g: (row_block(*g), 0)),
            pl.BlockSpec((HALO, K), lambda *g: (jnp.maximum(row_block(*g) * nh - 1, 0), 0)),
            pl.BlockSpec((HALO, K), lambda *g: (jnp.minimum((row_block(*g) + 1) * nh, last), 0))]


def _fill_halo_tile(x_scr, x_ref, xp_ref, xn_ref, tm):
    x_scr[0:HALO, :] = xp_ref[...]
    x_scr[HALO:HALO + tm, :] = x_ref[...]
    x_scr[HALO + tm:, :] = xn_ref[...]


def _zero_rows_outside_sequence(u_scr, j, L, tm):
    assert L % tm != 0
    j_end, r_end = L // tm, HALO + L % tm
    g_end = (r_end // 8) * 8
    sub8 = lax.broadcasted_iota(jnp.int32, (8, 1), 0)
    u_scr[HALO - 8:HALO, :] = u_scr[HALO - 8:HALO, :] * jnp.where((j == 0) & (sub8 == 7), 0.0, 1.0)
    u_scr[g_end:g_end + 8, :] = u_scr[g_end:g_end + 8, :] * jnp.where((j == j_end) & (sub8 == r_end - g_end), 0.0, 1.0)


def _conv3_centre(u_scr, cw, cb, tm):
    return (u_scr[HALO - 1:HALO - 1 + tm, :] * cw[0:1, :] + u_scr[HALO:HALO + tm, :] * cw[1:2, :]
            + u_scr[HALO + 1:HALO + 1 + tm, :] * cw[2:3, :] + cb)


def _hyena_in_kernel(x_ref, xp_ref, xn_ref, w_ref, cw_ref, cb_ref, x0_ref, vg_ref, x_scr, u0_scr, u1_scr, uv_scr,
                     *, L, tm, tpb):
    j = pl.program_id(1)
    D = D_MODEL

    @pl.when(j < tpb)
    def _():
        _fill_halo_tile(x_scr, x_ref, xp_ref, xn_ref, tm)
        xs = x_scr[...]
        t = j * tm + lax.broadcasted_iota(jnp.int32, (tm, 1), 0)
        cw, cb = cw_ref[...], cb_ref[...]
        conv = []
        for s, scr in enumerate((u0_scr, u1_scr, uv_scr)):
            sl = slice(s * D, (s + 1) * D)
            scr[...] = jnp.dot(xs, w_ref[:, sl], preferred_element_type=f32)
            _zero_rows_outside_sequence(scr, j, L, tm)
            conv.append(_conv3_centre(scr, cw[:, sl], cb[:, sl], tm))
        x0_ref[...] = conv[0].astype(x0_ref.dtype)
        vg_ref[...] = jnp.where(t < L, conv[1] * conv[2], 0.0).astype(vg_ref.dtype)

    @pl.when(j >= tpb)
    def _():
        vg_ref[...] = jnp.zeros_like(vg_ref)


def hyena_in(x, w_in, conv_w, conv_b, cfg):
    R, K = x.shape
    D = D_MODEL
    tm, tpb, ntr = cfg.TL, cfg.Lp // cfg.TL, pl.cdiv(cfg.NR, cfg.TL)
    B = R // cfg.Lp
    row_block = lambda b, j: b * tpb + jnp.minimum(j, tpb - 1)
    fixed = lambda b, j: (0, 0)
    return pl.pallas_call(
        functools.partial(_hyena_in_kernel, L=cfg.L, tm=tm, tpb=tpb),
        grid=(B, ntr),
        in_specs=_halo_specs(tm, K, R, row_block) +
                 [pl.BlockSpec((K, 3 * D), fixed, pipeline_mode=pl.Buffered(1)),
                  pl.BlockSpec((3, 3 * D), fixed), pl.BlockSpec((1, 3 * D), fixed)],
        out_specs=[pl.BlockSpec((tm, D), lambda b, j: (row_block(b, j), 0)),
                   pl.BlockSpec((None, tm, D), lambda b, j: (b, j, 0))],
        out_shape=[jax.ShapeDtypeStruct((R, D), bf16), jax.ShapeDtypeStruct((B, cfg.NR, D), bf16)],
        scratch_shapes=[pltpu.VMEM((tm + 2 * HALO, K), bf16)] + [pltpu.VMEM((tm + 2 * HALO, D), f32)] * 3,
        compiler_params=_cparams(("arbitrary", "arbitrary"), VMEM_LIMIT),
        name="hyena_in",
    )(x, x, x, w_in, conv_w, conv_b.reshape(1, -1))


def _hyena_out_kernel(y_ref, vg_ref, x0_ref, skip_ref, w_ref, h_ref, g_ref, b_ref, o_ref, obf_ref):
    y = y_ref[...].astype(f32) + vg_ref[...].astype(f32) * skip_ref[...]
    x = (y * x0_ref[...].astype(f32)).astype(bf16)
    _residual_ln(jnp.dot(x, w_ref[...], preferred_element_type=f32), h_ref, g_ref, b_ref, o_ref, obf_ref)


def hyena_out(yc, vg, x0, skip, w, h, g, b, cfg):
    R, N = h.shape
    tm, tpb = cfg.TL, cfg.Lp // cfg.TL
    row = lambda i: (i, 0)
    padded_row = lambda i: (i // tpb, i % tpb, 0)
    fixed = lambda i: (0, 0)
    return pl.pallas_call(
        _hyena_out_kernel,
        grid=(R // tm,),
        in_specs=[pl.BlockSpec((None, tm, N), padded_row), pl.BlockSpec((None, tm, N), padded_row), pl.BlockSpec((tm, N), row),
                  pl.BlockSpec((1, N), fixed), pl.BlockSpec((N, N), fixed), pl.BlockSpec((tm, N), row),
                  pl.BlockSpec((1, N), fixed), pl.BlockSpec((1, N), fixed)],
        out_specs=[pl.BlockSpec((tm, N), row), pl.BlockSpec((tm, N), row)],
        out_shape=[jax.ShapeDtypeStruct((R, N), f32), jax.ShapeDtypeStruct((R, N), bf16)],
        compiler_params=_cparams(("arbitrary",), VMEM_LIMIT),
        name="hyena_out",
    )(yc, vg, x0, skip.reshape(1, N), w, h, g.reshape(1, N), b.reshape(1, N))


def _ffn_in_gate_kernel(x_ref, xp_ref, xn_ref, wg_ref, wa_ref, cw_ref, cb_ref, o_ref, x_scr, g_scr, *, L, tm, tpb):
    i, c = pl.program_id(0), pl.program_id(1)

    @pl.when(c == 0)
    def _():
        _fill_halo_tile(x_scr, x_ref, xp_ref, xn_ref, tm)

    g_scr[...] = jnp.dot(x_scr[...], wg_ref[...], preferred_element_type=f32)
    a = jnp.dot(x_ref[...], wa_ref[...], preferred_element_type=f32)
    _zero_rows_outside_sequence(g_scr, i % tpb, L, tm)
    g = _conv3_centre(g_scr, cw_ref[...], cb_ref[...], tm)
    gelu = 0.5 * g * (1.0 + lax.erf(g * (2.0 ** -0.5)))
    o_ref[...] = (gelu * a).astype(o_ref.dtype)


def ffn_in_gate(x, w_in, conv_w, conv_b, cfg, nchunk=1):
    R, K = x.shape
    tm = cfg.TL
    tc = D_FF // nchunk
    tpb = cfg.Lp // tm
    assert D_FF % nchunk == 0 and tc % LANES == 0 and tm % HALO == 0
    wmode = dict(pipeline_mode=pl.Buffered(1)) if nchunk == 1 else {}
    return pl.pallas_call(
        functools.partial(_ffn_in_gate_kernel, L=cfg.L, tm=tm, tpb=tpb),
        grid=(R // tm, nchunk),
        in_specs=_halo_specs(tm, K, R, lambda i, c: i) + [
                  pl.BlockSpec((K, tc), lambda i, c: (0, c), **wmode),
                  pl.BlockSpec((K, tc), lambda i, c: (0, nchunk + c), **wmode),
                  pl.BlockSpec((3, tc), lambda i, c: (0, c)), pl.BlockSpec((1, tc), lambda i, c: (0, c))],
        out_specs=pl.BlockSpec((tm, tc), lambda i, c: (i, c)),
        out_shape=jax.ShapeDtypeStruct((R, D_FF), bf16),
        scratch_shapes=[pltpu.VMEM((tm + 2 * HALO, K), bf16), pltpu.VMEM((tm + 2 * HALO, tc), f32)],
        compiler_params=_cparams(("arbitrary", "arbitrary"), VMEM_LIMIT),
        name="ffn_in_gate",
    )(x, x, x, w_in, w_in, conv_w, conv_b.reshape(1, -1))


def _dft_tables(cfg):
    N1, NIN, N2, N = cfg.N1, cfg.NIN, FFT_N2, cfg.N
    assert N1 % 2 == 0
    k1 = jnp.arange(cfg.N1H, dtype=jnp.int32)
    wk = jnp.where((k1 == 0) | (k1 == N1 // 2), 1.0, jnp.where(k1 < N1 // 2, 2.0, 0.0)).astype(f32)
    n1 = jnp.arange(NIN, dtype=jnp.int32)
    ang = (2.0 * math.pi / N1) * ((k1[:, None] * n1[None, :]) % N1).astype(f32)
    ca, sa = jnp.cos(ang), jnp.sin(ang)
    fa = jnp.concatenate([ca, -sa], axis=0)
    fa_inv = jnp.concatenate([ca.T * wk, -sa.T * wk], axis=1) * (1.0 / N)
    k2 = jnp.arange(N2, dtype=jnp.int32)[:, None]
    n2 = jnp.arange(N2, dtype=jnp.int32)[None, :]
    t1 = (2.0 * math.pi / N2) * ((n2 * k2) % N2).astype(f32)
    t2 = (2.0 * math.pi / N) * ((n2 * k1[:, None]) % N).astype(f32)
    c1, s1 = jnp.cos(t1)[None], jnp.sin(t1)[None]
    c2, s2 = jnp.cos(t2)[:, None, :], jnp.sin(t2)[:, None, :]
    ar, ai = c1 * c2 - s1 * s2, -(s1 * c2 + c1 * s2)
    g_fwd = jnp.concatenate([jnp.concatenate([ar, -ai], axis=2), jnp.concatenate([ai, ar], axis=2)], axis=1)
    art, ait = jnp.swapaxes(ar, 1, 2), -jnp.swapaxes(ai, 1, 2)
    g_inv = jnp.concatenate([jnp.concatenate([art, -ait], axis=2), jnp.concatenate([ait, art], axis=2)], axis=1)
    return fa.astype(bf16), fa_inv.astype(bf16), g_fwd.astype(bf16), g_inv.astype(bf16)


FFT_KB = 4


def _fft_b_fused_kernel(z_ref, h_ref, gf_ref, gi_ref, o_ref):
    for kb in range(FFT_KB):
        zc = jnp.concatenate([z_ref[0, kb], z_ref[1, kb]], axis=0)
        y = jnp.dot(gf_ref[kb], zc, preferred_element_type=f32)
        yr, yi = y[:FFT_N2], y[FFT_N2:]
        hr, hi = h_ref[0, kb], h_ref[1, kb]
        pc = jnp.concatenate([yr * hr - yi * hi, yr * hi + yi * hr], axis=0).astype(bf16)
        q = jnp.dot(gi_ref[kb], pc, preferred_element_type=f32)
        o_ref[0, kb] = q[:FFT_N2].astype(o_ref.dtype)
        o_ref[1, kb] = q[FFT_N2:].astype(o_ref.dtype)


def fft_b_fused(z5, hspec, g_fwd, g_inv):
    B, _, N1, N2, D = z5.shape
    assert N1 % FFT_KB == 0
    zspec = pl.BlockSpec((None, 2, FFT_KB, N2, D), lambda k, b: (b, 0, k, 0, 0))
    gspec = pl.BlockSpec((FFT_KB, 2 * N2, 2 * N2), lambda k, b: (k, 0, 0))
    return pl.pallas_call(
        _fft_b_fused_kernel,
        grid=(N1 // FFT_KB, B),
        in_specs=[zspec, pl.BlockSpec((2, FFT_KB, N2, D), lambda k, b: (0, k, 0, 0)), gspec, gspec],
        out_specs=zspec,
        out_shape=jax.ShapeDtypeStruct(z5.shape, bf16),
        compiler_params=_cparams(("arbitrary", "arbitrary"), VMEM_LIMIT),
        name="fft_b_fused",
    )(z5, hspec, g_fwd, g_inv)


def _fft_b_filter_kernel(z_ref, csum_ref, gf_ref, o_ref):
    D = D_MODEL
    zc = z_ref[...].reshape(2 * FFT_N2, 2 * D)
    y = jnp.dot(gf_ref[...], zc, preferred_element_type=f32)
    scale = 1.0 / (csum_ref[...] + FILTER_NORM_EPS)
    y = y * scale
    yr, yi = y[:FFT_N2], y[FFT_N2:]
    o_ref[0] = yr[:, :D] + yr[:, D:]
    o_ref[1] = yi[:, :D] - yi[:, D:]


def fft_b_filter(z5, csum, g_fwd):
    _, N1, N2, D2 = z5.shape
    return pl.pallas_call(
        _fft_b_filter_kernel,
        grid=(N1,),
        in_specs=[pl.BlockSpec((2, None, N2, D2), lambda k: (0, k, 0, 0)), pl.BlockSpec((1, D2), lambda k: (0, 0)),
                  pl.BlockSpec((None, 2 * N2, 2 * N2), lambda k: (k, 0, 0))],
        out_specs=pl.BlockSpec((2, None, N2, D2 // 2), lambda k: (0, k, 0, 0)),
        out_shape=jax.ShapeDtypeStruct((2, N1, N2, D2 // 2), f32),
        compiler_params=_cparams(("arbitrary",), VMEM_LIMIT),
        name="fft_b_filter",
    )(z5, csum, g_fwd)


def _filter_mlp_kernel(z_ref, w1_ref, b1_ref, w2_ref, b2_ref, w3_ref, b3_ref, fr_ref, dl_ref, o_ref, cs_ref, *, L, tf):
    i = pl.program_id(0)
    hp = lax.Precision.HIGHEST
    z = z_ref[...]
    fr = fr_ref[...]
    h = jnp.sin(fr * (jnp.dot(z, w1_ref[...], preferred_element_type=f32, precision=hp) + b1_ref[...]))
    h = jnp.sin(fr * (jnp.dot(h, w2_ref[...], preferred_element_type=f32, precision=hp) + b2_ref[...]))
    h = jnp.dot(h.astype(bf16), w3_ref[...].astype(bf16), preferred_element_type=f32) + b3_ref[...]
    win = jnp.exp(-z[:, 0:1] * dl_ref[...]) + DECAY_SHIFT
    h = h * jnp.concatenate([win, win], axis=1)
    rows = i * tf + lax.broadcasted_iota(jnp.int32, (tf, 1), 0)
    h = jnp.where(rows < L, h, 0.0)

    @pl.when(i == 0)
    def _():
        cs_ref[...] = jnp.zeros_like(cs_ref)

    cs_ref[...] += jnp.sum(jnp.abs(h), axis=0, keepdims=True)
    lanes = lax.broadcasted_iota(jnp.int32, (1, h.shape[1]), 1)
    o_ref[...] = jnp.where((rows == 0) & (lanes >= D_MODEL), 0.0, h).astype(o_ref.dtype)


def filter_mlp(z, w1, b1, w2, b2, w3, b3, freq, deltas, cfg, tf=512):
    NR = cfg.NR
    D2 = 2 * D_MODEL
    fixed = lambda i: (0, 0)
    full = lambda a: pl.BlockSpec(a.shape, fixed)
    args = (w1, b1.reshape(1, -1), w2, b2.reshape(1, -1), w3, b3.reshape(1, -1), freq.reshape(1, -1), deltas.reshape(1, -1))
    return pl.pallas_call(
        functools.partial(_filter_mlp_kernel, L=cfg.L, tf=tf),
        grid=(NR // tf,),
        in_specs=[pl.BlockSpec((tf, FILTER_EMB_PAD), lambda i: (i, 0))] + [full(a) for a in args],
        out_specs=[pl.BlockSpec((tf, D2), lambda i: (i, 0)), pl.BlockSpec((1, D2), fixed)],
        out_shape=[jax.ShapeDtypeStruct((NR, D2), bf16), jax.ShapeDtypeStruct((1, D2), f32)],
        compiler_params=_cparams(("arbitrary",), VMEM_LIMIT),
        name="filter_mlp",
    )(z, *args)


def _filter_features(cfg):
    L = cfg.L
    pos = jnp.arange(cfg.NR, dtype=f32)
    t = (pos / (L - 1))[:, None]
    w = (2.0 * math.pi / L) * pos[:, None]
    bands = jnp.linspace(1e-4, FILTER_BANDS - 1, FILTER_BANDS, dtype=f32)
    z = jnp.concatenate([t, jnp.cos(w * bands), -jnp.sin(w * bands)], axis=-1)
    return jnp.pad(z, ((0, 0), (0, FILTER_EMB_PAD - FILTER_EMB_DIM)))


def hyena_filter_spectrum(cfg, tables, w1, b1, w2, b2, w3, b3, freq):
    fa, _, g_fwd, _ = tables
    z = _filter_features(cfg)
    w1p = jnp.pad(w1, ((0, FILTER_EMB_PAD - FILTER_EMB_DIM), (0, 0)))
    max_decay = math.log(1.0 / DECAY_TARGET) / FAST_DECAY_PCT
    min_decay = math.log(1.0 / DECAY_TARGET) / SLOW_DECAY_PCT
    deltas = jnp.linspace(min_decay, max_decay, D_MODEL, dtype=f32)
    xf, csum = filter_mlp(z, w1p, b1, w2, b2, w3, b3, freq, deltas, cfg)
    D2 = 2 * D_MODEL
    z2 = left_matmul(fa, xf.reshape(1, cfg.NIN, FFT_N2 * D2), bf16)
    z5 = z2.reshape(2, cfg.N1H, FFT_N2, D2)
    return fft_b_filter(z5, csum, g_fwd)


def long_conv(vg, hspec, tables, cfg):
    fa, fa_inv, g_fwd, g_inv = tables
    B = vg.shape[0]
    D = D_MODEL
    z2 = left_matmul(fa, vg.reshape(B, cfg.NIN, FFT_N2 * D), bf16)
    q5 = fft_b_fused(z2.reshape(B, 2, cfg.N1H, FFT_N2, D), hspec, g_fwd, g_inv)
    y2 = left_matmul(fa_inv, q5.reshape(B, 2 * cfg.N1H, FFT_N2 * D), bf16)
    return y2.reshape(B, cfg.NR, D)


def _qkv_proj_kernel(x_ref, w_ref, cc_ref, ss_ref, qg_ref, kg_ref, q_ref, k_ref, v_ref, *, L, tm, tpb, sub):
    nq, nk = N_HEADS * HEAD_DIM, N_KV_HEADS * HEAD_DIM
    for r in range(0, tm, sub):
        rows = slice(r, r + sub)
        t = (pl.program_id(0) % tpb) * tm + r + lax.broadcasted_iota(jnp.int32, (sub, 1), 0)
        valid = t < L
        cc, ss = cc_ref[rows, :], ss_ref[rows, :]
        qkv = jnp.dot(x_ref[rows, :], w_ref[...], preferred_element_type=f32)

        def norm_rope(x, gain, scale):
            xn = x * lax.rsqrt(jnp.mean(x * x, axis=-1, keepdims=True) + RMS_EPS) * gain
            y = xn * cc + pltpu.roll(xn, HEAD_DIM // 2, axis=1) * ss
            return jnp.where(valid, y * scale, 0.0).astype(bf16)

        for hh in range(N_HEADS):
            sl = slice(hh * HEAD_DIM, (hh + 1) * HEAD_DIM)
            q_ref[rows, sl] = norm_rope(qkv[:, sl], qg_ref[...], LOG2E * HEAD_DIM ** -0.5)
        for hh in range(N_KV_HEADS):
            sl = slice(hh * HEAD_DIM, (hh + 1) * HEAD_DIM)
            k_ref[rows, sl] = norm_rope(qkv[:, nq + hh * HEAD_DIM:nq + (hh + 1) * HEAD_DIM], kg_ref[...], 1.0)
        v_ref[rows, :] = jnp.where(valid, qkv[:, nq + nk:], 0.0).astype(bf16)


def qkv_proj(x, w, cc, ss, q_gain, k_gain, cfg):
    R, K = x.shape
    tm, tpb = cfg.TL, cfg.Lp // cfg.TL
    nq, nk = N_HEADS * HEAD_DIM, N_KV_HEADS * HEAD_DIM
    sub = 256 if tm % 256 == 0 else tm // 2
    assert tm % sub == 0 and sub % HALO == 0
    row = lambda i: (i, 0)
    tab = lambda i: (i % tpb, 0)
    fixed = lambda i: (0, 0)
    return pl.pallas_call(
        functools.partial(_qkv_proj_kernel, L=cfg.L, tm=tm, tpb=tpb, sub=sub),
        grid=(R // tm,),
        in_specs=[pl.BlockSpec((tm, K), row), pl.BlockSpec((K, nq + 2 * nk), fixed), pl.BlockSpec((tm, HEAD_DIM), tab),
                  pl.BlockSpec((tm, HEAD_DIM), tab), pl.BlockSpec((1, HEAD_DIM), fixed), pl.BlockSpec((1, HEAD_DIM), fixed)],
        out_specs=[pl.BlockSpec((tm, nq), row), pl.BlockSpec((tm, nk), row), pl.BlockSpec((tm, nk), row)],
        out_shape=[jax.ShapeDtypeStruct((R, nq), bf16), jax.ShapeDtypeStruct((R, nk), bf16),
                   jax.ShapeDtypeStruct((R, nk), bf16)],
        compiler_params=_cparams(("arbitrary",), VMEM_LIMIT),
        name="qkv_proj",
    )(x, w, cc, ss, q_gain.reshape(1, -1), k_gain.reshape(1, -1))


def _flash_kernel(q_ref, k_ref, v_ref, o_ref, s0_scr, s1_scr, p0_scr, p1_scr, *, L):
    s_bufs, p_bufs = (s0_scr, s1_scr), (p0_scr, p1_scr)
    lp = k_ref.shape[0]
    c0 = (L // LANES) * LANES
    for g in range(GROUP):
        s_scr, p_scr = s_bufs[g % 2], p_bufs[g % 2]
        q = q_ref[:, g * HEAD_DIM:(g + 1) * HEAD_DIM]
        s = lax.dot_general(q, k_ref[...], (((1,), (1,)), ((), ())), preferred_element_type=f32)
        if c0 < lp:
            cols = c0 + lax.broadcasted_iota(jnp.int32, (1, lp - c0), 1)
            s_scr[:, :c0] = s[:, :c0]
            s_scr[:, c0:] = jnp.where(cols < L, s[:, c0:], NEG_BIG)
        else:
            s_scr[...] = s
        s = s_scr[...]
        p = jnp.exp2(s - jnp.max(s, axis=-1, keepdims=True))
        l = jnp.sum(p, axis=-1, keepdims=True)
        p_scr[...] = p.astype(bf16)
        o = jnp.dot(p_scr[...], v_ref[...], preferred_element_type=f32)
        o_ref[:, g * HEAD_DIM:(g + 1) * HEAD_DIM] = (o / l).astype(o_ref.dtype)


def flash_attention(q, k, v, cfg):
    B = q.shape[0]
    tq, lp = cfg.TQ, cfg.Lp
    gw = GROUP * HEAD_DIM
    return pl.pallas_call(
        functools.partial(_flash_kernel, L=cfg.L),
        grid=(B, N_KV_HEADS, lp // tq),
        in_specs=[pl.BlockSpec((None, tq, gw), lambda b, h, i: (b, i, h)),
                  pl.BlockSpec((None, lp, HEAD_DIM), lambda b, h, i: (b, 0, h)),
                  pl.BlockSpec((None, lp, HEAD_DIM), lambda b, h, i: (b, 0, h))],
        out_specs=pl.BlockSpec((None, tq, gw), lambda b, h, i: (b, i, h)),
        out_shape=jax.ShapeDtypeStruct(q.shape, bf16),
        scratch_shapes=[pltpu.VMEM((tq, lp), f32), pltpu.VMEM((tq, lp), f32),
                        pltpu.VMEM((tq, lp), bf16), pltpu.VMEM((tq, lp), bf16)],
        compiler_params=_cparams(("arbitrary", "arbitrary", "arbitrary"), VMEM_LIMIT),
        name="flash",
    )(q, k, v)


def _rope_tables(cfg):
    n = cfg.L - N_META
    rows = n // GRID_W
    row = jnp.concatenate([jnp.full((N_META,), -1.0, f32), jnp.repeat(jnp.arange(rows, dtype=f32), GRID_W)])
    col = jnp.concatenate([jnp.arange(N_META, dtype=f32), jnp.tile(jnp.arange(GRID_W, dtype=f32), rows)])
    axis_rot = HEAD_DIM // 2
    inv_freq = ROPE_THETA ** (-jnp.arange(0, axis_rot, 2, dtype=f32) / axis_rot)
    ang = jnp.concatenate([row[:, None] * inv_freq, col[:, None] * inv_freq], axis=-1)
    ang = jnp.pad(ang, ((0, cfg.Lp - cfg.L), (0, 0)))
    c, s = jnp.cos(ang), jnp.sin(ang)
    return jnp.concatenate([c, c], axis=-1), jnp.concatenate([-s, s], axis=-1)


def _run_trunk(x, cfg, meta_tokens, hy, at, ln, ffn):
    B, n, D = x.shape
    Lp = cfg.Lp
    R = B * Lp
    meta = jnp.broadcast_to(meta_tokens[None], (B, N_META, D))
    h = jnp.concatenate([meta, x, jnp.zeros((B, Lp - cfg.L, D), x.dtype)], axis=1).reshape(R, D)
    hb = h.astype(bf16)
    cc, ss = _rope_tables(cfg)
    tables = _dft_tables(cfg)
    for i in range(DEPTH):
        j = i // 2
        if i % 2 == 0:
            hspec = hyena_filter_spectrum(cfg, tables, hy["f_w1"][j], hy["f_b1"][j], hy["f_w2"][j], hy["f_b2"][j],
                                          hy["f_w3"][j], hy["f_b3"][j], hy["f_freq"][j])
            x0c, vg = hyena_in(hb, hy["w_in"][j], hy["conv_w"][j], hy["conv_b"][j], cfg)
            yc = long_conv(vg, hspec, tables, cfg)
            h, hb = hyena_out(yc, vg, x0c, hy["skip"][j], hy["w_out"][j], h,
                              ln["g1"][i], ln["b1"][i], cfg)
        else:
            q, k, v = qkv_proj(hb, at["w_qkv"][j], cc, ss, at["q_gain"][j], at["k_gain"][j], cfg)
            o = flash_attention(q.reshape(B, Lp, -1), k.reshape(B, Lp, -1), v.reshape(B, Lp, -1), cfg)
            h, hb = matmul_residual_ln(o.reshape(R, D), at["w_out"][j], h, ln["g1"][i], ln["b1"][i], cfg.TL)
        gated = ffn_in_gate(hb, ffn["w_in"][i], ffn["conv_w"][i], ffn["conv_b"][i], cfg)
        h, hb = matmul_residual_ln(gated, ffn["w_out"][i], h, ln["g2"][i], ln["b2"][i], cfg.TL)
    return h.reshape(B, Lp, D)[:, N_META:cfg.L]


def kernel(x_prompt, x_sample, meta_tokens, hy_w_in, hy_conv_w, hy_conv_b, hy_filt_w1, hy_filt_b1, hy_filt_w2, hy_filt_b2, hy_filt_w3, hy_filt_b3, hy_filt_freq, hy_skip, hy_w_out, at_w_qkv, at_q_gain, at_k_gain, at_w_out, ln1_g, ln1_b, ln2_g, ln2_b, ffn_w_in, ffn_conv_w, ffn_conv_b, ffn_w_out):
    hy = dict(w_in=hy_w_in.astype(bf16), conv_w=hy_conv_w, conv_b=hy_conv_b, f_w1=hy_filt_w1, f_b1=hy_filt_b1,
              f_w2=hy_filt_w2, f_b2=hy_filt_b2, f_w3=hy_filt_w3, f_b3=hy_filt_b3, f_freq=hy_filt_freq, skip=hy_skip,
              w_out=hy_w_out.astype(bf16))
    at = dict(w_qkv=at_w_qkv.astype(bf16), q_gain=at_q_gain, k_gain=at_k_gain, w_out=at_w_out.astype(bf16))
    ln = dict(g1=ln1_g, b1=ln1_b, g2=ln2_g, b2=ln2_b)
    ffn = dict(w_in=ffn_w_in.astype(bf16), conv_w=ffn_conv_w, conv_b=ffn_conv_b, w_out=ffn_w_out.astype(bf16))
    y_prompt = _run_trunk(x_prompt, Cfg(x_prompt.shape[1]), meta_tokens, hy, at, ln, ffn)
    y_sample = _run_trunk(x_sample, Cfg(x_sample.shape[1]), meta_tokens, hy, at, ln, ffn)
    return (y_prompt, y_sample)
```

```python
import functools
import math

import jax
import jax.numpy as jnp
from jax import lax
from jax.experimental import pallas as pl
from jax.experimental.pallas import tpu as pltpu

f32 = jnp.float32
bf16 = jnp.bfloat16

D_MODEL = 1024
DEPTH = 4
N_META = 16
GRID_W = 64
HEAD_DIM = 128
N_HEADS = 8
N_KV_HEADS = 2
GROUP = N_HEADS // N_KV_HEADS
ROPE_THETA = 10000.0
FILTER_EMB_DIM = 33
FILTER_EMB_PAD = 40
FILTER_BANDS = 16
DECAY_TARGET = 1e-2
FAST_DECAY_PCT = 0.3
SLOW_DECAY_PCT = 1.5
DECAY_SHIFT = 0.05
FILTER_NORM_EPS = 1e-6
D_FF = 2816
DEEPNORM_ALPHA = (2 * DEPTH) ** 0.25
LN_EPS = 1e-5
RMS_EPS = 1e-6
NEG_BIG = -1e30
LOG2E = 1.4426950408889634

LANES = 128
SUBLANES = 8
FFT_N2 = 128
V7X_VMEM_BYTES = 64 * 1024 * 1024
VMEM_LIMIT = 52 * 1024 * 1024
assert VMEM_LIMIT < V7X_VMEM_BYTES


class Cfg:
    def __init__(self, n):
        self.L = n + N_META
        if n == 8192:
            self.Lp, self.TL, self.TQ, self.N1, self.NIN = 8448, 768, 256, 136, 80
        elif n == 2048:
            self.Lp, self.TL, self.TQ, self.N1, self.NIN = 2304, 768, 256, 40, 32
        else:
            up = lambda a, m: -(-a // m) * m
            self.TL = self.TQ = 128
            self.Lp = up(self.L, LANES)
            self.N1 = up(-(-(2 * self.L - 1) // FFT_N2), 8)
            self.NIN = up(-(-self.Lp // FFT_N2), 16)
        self.NR = self.NIN * FFT_N2
        self.N1H = -(-(self.N1 // 2 + 1) // SUBLANES) * SUBLANES
        self.N = self.N1 * FFT_N2
        assert self.N >= 2 * self.L - 1 and self.NR >= self.Lp
        assert self.Lp % self.TL == 0 and self.Lp % self.TQ == 0


def _cparams(sem, vmem=None):
    return pltpu.CompilerParams(dimension_semantics=sem, vmem_limit_bytes=vmem)


def _residual_ln(m, h_ref, g_ref, b_ref, o_ref, obf_ref):
    y = DEEPNORM_ALPHA * h_ref[...] + m
    mu = jnp.mean(y, axis=-1, keepdims=True)
    yc = y - mu
    var = jnp.mean(yc * yc, axis=-1, keepdims=True)
    out = yc * lax.rsqrt(var + LN_EPS) * g_ref[...] + b_ref[...]
    o_ref[...] = out
    obf_ref[...] = out.astype(bf16)


def _row_subtile(tm):
    sub = 256 if tm % 256 == 0 else tm // 2
    assert tm % sub == 0 and sub % 16 == 0
    return sub


def _mm_ln_kernel(x_ref, w_ref, h_ref, g_ref, b_ref, o_ref, obf_ref):
    sub = _row_subtile(x_ref.shape[0])
    for r in range(0, x_ref.shape[0], sub):
        rows = pl.ds(r, sub)
        _residual_ln(jnp.dot(x_ref[rows, :], w_ref[...], preferred_element_type=f32),
                     h_ref.at[rows, :], g_ref, b_ref, o_ref.at[rows, :], obf_ref.at[rows, :])


def matmul_residual_ln(x, w, h, g, b, tm):
    R, K = x.shape
    N = w.shape[1]
    row = lambda i: (i, 0)
    fixed = lambda i: (0, 0)
    return pl.pallas_call(
        _mm_ln_kernel,
        grid=(R // tm,),
        in_specs=[pl.BlockSpec((tm, K), row), pl.BlockSpec((K, N), fixed), pl.BlockSpec((tm, N), row),
                  pl.BlockSpec((1, N), fixed), pl.BlockSpec((1, N), fixed)],
        out_specs=[pl.BlockSpec((tm, N), row), pl.BlockSpec((tm, N), row)],
        out_shape=[jax.ShapeDtypeStruct((R, N), f32), jax.ShapeDtypeStruct((R, N), bf16)],
        compiler_params=_cparams(("arbitrary",), VMEM_LIMIT),
        name="mm_ln",
    )(x, w, h, g.reshape(1, N), b.reshape(1, N))


def _left_mm_kernel(a_ref, x_ref, o_ref, *, cw):
    a = a_ref[...]
    for c in range(0, x_ref.shape[1], cw):
        o_ref[:, c:c + cw] = jnp.dot(a, x_ref[:, c:c + cw], preferred_element_type=f32).astype(o_ref.dtype)


def left_matmul(a, x, out_dtype, tn=32768, cw=512):
    M, K = a.shape
    B, _, C = x.shape
    assert C % tn == 0 and tn % cw == 0
    return pl.pallas_call(
        functools.partial(_left_mm_kernel, cw=cw),
        grid=(B, C // tn),
        in_specs=[pl.BlockSpec((M, K), lambda b, j: (0, 0)), pl.BlockSpec((None, K, tn), lambda b, j: (b, 0, j))],
        out_specs=pl.BlockSpec((None, M, tn), lambda b, j: (b, 0, j)),
        out_shape=jax.ShapeDtypeStruct((B, M, C), out_dtype),
        compiler_params=_cparams(("arbitrary", "arbitrary"), VMEM_LIMIT),
        name="left_mm",
    )(a, x)


HALO = 16


def _halo_specs(tm, K, R, row_block):
    nh = tm // HALO
    last = R // HALO - 1
    return [pl.BlockSpec((tm, K), lambda *g: (row_block(*g), 0)),
            pl.BlockSpec((HALO, K), lambda *g: (jnp.maximum(row_block(*g) * nh - 1, 0), 0)),
            pl.BlockSpec((HALO, K), lambda *g: (jnp.minimum((row_block(*g) + 1) * nh, last), 0))]


def _fill_halo_tile(x_scr, x_ref, xp_ref, xn_ref, tm):
    x_scr[0:HALO, :] = xp_ref[...]
    x_scr[HALO:HALO + tm, :] = x_ref[...]
    x_scr[HALO + tm:, :] = xn_ref[...]


def _zero_rows_outside_sequence(u_scr, j, L, tm):
    assert L % tm != 0
    j_end, r_end = L // tm, HALO + L % tm
    g_end = (r_end // SUBLANES) * SUBLANES
    sub = lax.broadcasted_iota(jnp.int32, (SUBLANES, 1), 0)
    first = slice(HALO - SUBLANES, HALO)
    end = slice(g_end, g_end + SUBLANES)
    u_scr[first, :] = jnp.where((j == 0) & (sub == SUBLANES - 1), 0.0, u_scr[first, :])
    u_scr[end, :] = jnp.where((j == j_end) & (sub == r_end - g_end), 0.0, u_scr[end, :])


def _conv3_centre(u_scr, cw, cb, tm):
    return (u_scr[HALO - 1:HALO - 1 + tm, :] * cw[0:1, :] + u_scr[HALO:HALO + tm, :] * cw[1:2, :]
            + u_scr[HALO + 1:HALO + 1 + tm, :] * cw[2:3, :] + cb)


def _hyena_in_kernel(x_ref, xp_ref, xn_ref, w_ref, cw_ref, cb_ref, x0_ref, vg_ref, x_scr, u0_scr, u1_scr, uv_scr,
                     *, L, tm, tpb):
    j = pl.program_id(1)
    D = D_MODEL

    @pl.when(j < tpb)
    def _():
        _fill_halo_tile(x_scr, x_ref, xp_ref, xn_ref, tm)
        xs = x_scr[...]
        t = j * tm + lax.broadcasted_iota(jnp.int32, (tm, 1), 0)
        cw, cb = cw_ref[...], cb_ref[...]
        conv = []
        for s, scr in enumerate((u0_scr, u1_scr, uv_scr)):
            sl = slice(s * D, (s + 1) * D)
            scr[...] = jnp.dot(xs, w_ref[:, sl], preferred_element_type=f32)
            _zero_rows_outside_sequence(scr, j, L, tm)
            conv.append(_conv3_centre(scr, cw[:, sl], cb[:, sl], tm))
        x0_ref[...] = conv[0].astype(x0_ref.dtype)
        vg_ref[...] = jnp.where(t < L, conv[1] * conv[2], 0.0).astype(vg_ref.dtype)

    @pl.when(j >= tpb)
    def _():
        vg_ref[...] = jnp.zeros_like(vg_ref)


def hyena_in(x, w_in, conv_w, conv_b, cfg):
    R, K = x.shape
    D = D_MODEL
    tm, tpb, ntr = cfg.TL, cfg.Lp // cfg.TL, pl.cdiv(cfg.NR, cfg.TL)
    B = R // cfg.Lp
    row_block = lambda b, j: b * tpb + jnp.minimum(j, tpb - 1)
    fixed = lambda b, j: (0, 0)
    return pl.pallas_call(
        functools.partial(_hyena_in_kernel, L=cfg.L, tm=tm, tpb=tpb),
        grid=(B, ntr),
        in_specs=_halo_specs(tm, K, R, row_block) +
                 [pl.BlockSpec((K, 3 * D), fixed, pipeline_mode=pl.Buffered(1)),
                  pl.BlockSpec((3, 3 * D), fixed), pl.BlockSpec((1, 3 * D), fixed)],
        out_specs=[pl.BlockSpec((tm, D), lambda b, j: (row_block(b, j), 0)),
                   pl.BlockSpec((None, tm, D), lambda b, j: (b, j, 0))],
        out_shape=[jax.ShapeDtypeStruct((R, D), bf16), jax.ShapeDtypeStruct((B, cfg.NR, D), bf16)],
        scratch_shapes=[pltpu.VMEM((tm + 2 * HALO, K), bf16)] + [pltpu.VMEM((tm + 2 * HALO, D), f32)] * 3,
        compiler_params=_cparams(("arbitrary", "arbitrary"), VMEM_LIMIT),
        name="hyena_in",
    )(x, x, x, w_in, conv_w, conv_b.reshape(1, -1))


def _hyena_out_kernel(y_ref, vg_ref, x0_ref, skip_ref, w_ref, h_ref, g_ref, b_ref, o_ref, obf_ref):
    y = y_ref[...].astype(f32) + vg_ref[...].astype(f32) * skip_ref[...]
    x = (y * x0_ref[...].astype(f32)).astype(bf16)
    _residual_ln(jnp.dot(x, w_ref[...], preferred_element_type=f32), h_ref, g_ref, b_ref, o_ref, obf_ref)


def hyena_out(yc, vg, x0, skip, w, h, g, b, cfg):
    R, N = h.shape
    tm, tpb = cfg.TL, cfg.Lp // cfg.TL
    row = lambda i: (i, 0)
    padded_row = lambda i: (i // tpb, i % tpb, 0)
    fixed = lambda i: (0, 0)
    return pl.pallas_call(
        _hyena_out_kernel,
        grid=(R // tm,),
        in_specs=[pl.BlockSpec((None, tm, N), padded_row), pl.BlockSpec((None, tm, N), padded_row), pl.BlockSpec((tm, N), row),
                  pl.BlockSpec((1, N), fixed), pl.BlockSpec((N, N), fixed), pl.BlockSpec((tm, N), row),
                  pl.BlockSpec((1, N), fixed), pl.BlockSpec((1, N), fixed)],
        out_specs=[pl.BlockSpec((tm, N), row), pl.BlockSpec((tm, N), row)],
        out_shape=[jax.ShapeDtypeStruct((R, N), f32), jax.ShapeDtypeStruct((R, N), bf16)],
        compiler_params=_cparams(("arbitrary",), VMEM_LIMIT),
        name="hyena_out",
    )(yc, vg, x0, skip.reshape(1, N), w, h, g.reshape(1, N), b.reshape(1, N))


def _ffn_in_gate_kernel(x_ref, xp_ref, xn_ref, wg_ref, wa_ref, cw_ref, cb_ref, o_ref, x_scr, g_scr, *, L, tm, tpb):
    i, c = pl.program_id(0), pl.program_id(1)

    @pl.when(c == 0)
    def _():
        _fill_halo_tile(x_scr, x_ref, xp_ref, xn_ref, tm)

    g_scr[...] = jnp.dot(x_scr[...], wg_ref[...], preferred_element_type=f32)
    a = jnp.dot(x_ref[...], wa_ref[...], preferred_element_type=f32)
    _zero_rows_outside_sequence(g_scr, i % tpb, L, tm)
    g = _conv3_centre(g_scr, cw_ref[...], cb_ref[...], tm)
    gelu = 0.5 * g * (1.0 + lax.erf(g * (2.0 ** -0.5)))
    o_ref[...] = (gelu * a).astype(o_ref.dtype)


def ffn_in_gate(x, w_in, conv_w, conv_b, cfg, nchunk=1):
    R, K = x.shape
    tm = cfg.TL
    tc = D_FF // nchunk
    tpb = cfg.Lp // tm
    assert D_FF % nchunk == 0 and tc % LANES == 0 and tm % HALO == 0
    wmode = dict(pipeline_mode=pl.Buffered(1)) if nchunk == 1 else {}
    return pl.pallas_call(
        functools.partial(_ffn_in_gate_kernel, L=cfg.L, tm=tm, tpb=tpb),
        grid=(R // tm, nchunk),
        in_specs=_halo_specs(tm, K, R, lambda i, c: i) + [
                  pl.BlockSpec((K, tc), lambda i, c: (0, c), **wmode),
                  pl.BlockSpec((K, tc), lambda i, c: (0, nchunk + c), **wmode),
                  pl.BlockSpec((3, tc), lambda i, c: (0, c)), pl.BlockSpec((1, tc), lambda i, c: (0, c))],
        out_specs=pl.BlockSpec((tm, tc), lambda i, c: (i, c)),
        out_shape=jax.ShapeDtypeStruct((R, D_FF), bf16),
        scratch_shapes=[pltpu.VMEM((tm + 2 * HALO, K), bf16), pltpu.VMEM((tm + 2 * HALO, tc), f32)],
        compiler_params=_cparams(("arbitrary", "arbitrary"), VMEM_LIMIT),
        name="ffn_in_gate",
    )(x, x, x, w_in, w_in, conv_w, conv_b.reshape(1, -1))


def _dft_tables(cfg):
    N1, NIN, N2, N = cfg.N1, cfg.NIN, FFT_N2, cfg.N
    assert N1 % 2 == 0
    k1 = jnp.arange(cfg.N1H, dtype=jnp.int32)
    wk = jnp.where((k1 == 0) | (k1 == N1 // 2), 1.0, jnp.where(k1 < N1 // 2, 2.0, 0.0)).astype(f32)
    n1 = jnp.arange(NIN, dtype=jnp.int32)
    ang = (2.0 * math.pi / N1) * ((k1[:, None] * n1[None, :]) % N1).astype(f32)
    ca, sa = jnp.cos(ang), jnp.sin(ang)
    fa = jnp.concatenate([ca, -sa], axis=0)
    fa_inv = jnp.concatenate([ca.T * wk, -sa.T * wk], axis=1) * (1.0 / N)
    k2 = jnp.arange(N2, dtype=jnp.int32)[:, None]
    n2 = jnp.arange(N2, dtype=jnp.int32)[None, :]
    t1 = (2.0 * math.pi / N2) * ((n2 * k2) % N2).astype(f32)
    t2 = (2.0 * math.pi / N) * ((n2 * k1[:, None]) % N).astype(f32)
    c1, s1 = jnp.cos(t1)[None], jnp.sin(t1)[None]
    c2, s2 = jnp.cos(t2)[:, None, :], jnp.sin(t2)[:, None, :]
    ar, ai = c1 * c2 - s1 * s2, -(s1 * c2 + c1 * s2)
    g_fwd = jnp.concatenate([jnp.concatenate([ar, -ai], axis=2), jnp.concatenate([ai, ar], axis=2)], axis=1)
    art, ait = jnp.swapaxes(ar, 1, 2), -jnp.swapaxes(ai, 1, 2)
    g_inv = jnp.concatenate([jnp.concatenate([art, -ait], axis=2), jnp.concatenate([ait, art], axis=2)], axis=1)
    return fa.astype(bf16), fa_inv.astype(bf16), g_fwd.astype(bf16), g_inv.astype(bf16)


FFT_KB = 4


def _fft_b_fused_kernel(z_ref, h_ref, gf_ref, gi_ref, o_ref):
    for kb in range(FFT_KB):
        zc = jnp.concatenate([z_ref[0, kb], z_ref[1, kb]], axis=0)
        y = jnp.dot(gf_ref[kb], zc, preferred_element_type=f32)
        yr, yi = y[:FFT_N2], y[FFT_N2:]
        hr, hi = h_ref[0, kb], h_ref[1, kb]
        pc = jnp.concatenate([yr * hr - yi * hi, yr * hi + yi * hr], axis=0).astype(bf16)
        q = jnp.dot(gi_ref[kb], pc, preferred_element_type=f32)
        o_ref[0, kb] = q[:FFT_N2].astype(o_ref.dtype)
        o_ref[1, kb] = q[FFT_N2:].astype(o_ref.dtype)


def fft_b_fused(z5, hspec, g_fwd, g_inv):
    B, _, N1, N2, D = z5.shape
    assert N1 % FFT_KB == 0
    zspec = pl.BlockSpec((None, 2, FFT_KB, N2, D), lambda k, b: (b, 0, k, 0, 0))
    gspec = pl.BlockSpec((FFT_KB, 2 * N2, 2 * N2), lambda k, b: (k, 0, 0))
    return pl.pallas_call(
        _fft_b_fused_kernel,
        grid=(N1 // FFT_KB, B),
        in_specs=[zspec, pl.BlockSpec((2, FFT_KB, N2, D), lambda k, b: (0, k, 0, 0)), gspec, gspec],
        out_specs=zspec,
        out_shape=jax.ShapeDtypeStruct(z5.shape, bf16),
        compiler_params=_cparams(("arbitrary", "arbitrary"), VMEM_LIMIT),
        name="fft_b_fused",
    )(z5, hspec, g_fwd, g_inv)


def _fft_b_filter_kernel(z_ref, csum_ref, gf_ref, o_ref):
    D = D_MODEL
    zc = z_ref[...].reshape(2 * FFT_N2, 2 * D)
    y = jnp.dot(gf_ref[...], zc, preferred_element_type=f32)
    scale = 1.0 / (csum_ref[...] + FILTER_NORM_EPS)
    y = y * scale
    yr, yi = y[:FFT_N2], y[FFT_N2:]
    o_ref[0] = yr[:, :D] + yr[:, D:]
    o_ref[1] = yi[:, :D] - yi[:, D:]


def fft_b_filter(z5, csum, g_fwd):
    _, N1, N2, D2 = z5.shape
    return pl.pallas_call(
        _fft_b_filter_kernel,
        grid=(N1,),
        in_specs=[pl.BlockSpec((2, None, N2, D2), lambda k: (0, k, 0, 0)), pl.BlockSpec((1, D2), lambda k: (0, 0)),
                  pl.BlockSpec((None, 2 * N2, 2 * N2), lambda k: (k, 0, 0))],
        out_specs=pl.BlockSpec((2, None, N2, D2 // 2), lambda k: (0, k, 0, 0)),
        out_shape=jax.ShapeDtypeStruct((2, N1, N2, D2 // 2), f32),
        compiler_params=_cparams(("arbitrary",), VMEM_LIMIT),
        name="fft_b_filter",
    )(z5, csum, g_fwd)


def _filter_mlp_kernel(z_ref, w1_ref, b1_ref, w2_ref, b2_ref, w3_ref, b3_ref, fr_ref, dl_ref, o_ref, cs_ref, *, L, tf):
    i = pl.program_id(0)
    hp = lax.Precision.HIGHEST
    z = z_ref[...]
    fr = fr_ref[...]
    h = jnp.sin(fr * (jnp.dot(z, w1_ref[...], preferred_element_type=f32, precision=hp) + b1_ref[...]))
    h = jnp.sin(fr * (jnp.dot(h, w2_ref[...], preferred_element_type=f32, precision=hp) + b2_ref[...]))
    h = jnp.dot(h.astype(bf16), w3_ref[...].astype(bf16), preferred_element_type=f32) + b3_ref[...]
    win = jnp.exp(-z[:, 0:1] * dl_ref[...]) + DECAY_SHIFT
    h = h * jnp.concatenate([win, win], axis=1)
    rows = i * tf + lax.broadcasted_iota(jnp.int32, (tf, 1), 0)
    h = jnp.where(rows < L, h, 0.0)

    @pl.when(i == 0)
    def _():
        cs_ref[...] = jnp.zeros_like(cs_ref)

    cs_ref[...] += jnp.sum(jnp.abs(h), axis=0, keepdims=True)
    lanes = lax.broadcasted_iota(jnp.int32, (1, h.shape[1]), 1)
    o_ref[...] = jnp.where((rows == 0) & (lanes >= D_MODEL), 0.0, h).astype(o_ref.dtype)


def filter_mlp(z, w1, b1, w2, b2, w3, b3, freq, deltas, cfg, tf=512):
    NR = cfg.NR
    D2 = 2 * D_MODEL
    fixed = lambda i: (0, 0)
    full = lambda a: pl.BlockSpec(a.shape, fixed)
    args = (w1, b1.reshape(1, -1), w2, b2.reshape(1, -1), w3, b3.reshape(1, -1), freq.reshape(1, -1), deltas.reshape(1, -1))
    return pl.pallas_call(
        functools.partial(_filter_mlp_kernel, L=cfg.L, tf=tf),
        grid=(NR // tf,),
        in_specs=[pl.BlockSpec((tf, FILTER_EMB_PAD), lambda i: (i, 0))] + [full(a) for a in args],
        out_specs=[pl.BlockSpec((tf, D2), lambda i: (i, 0)), pl.BlockSpec((1, D2), fixed)],
        out_shape=[jax.ShapeDtypeStruct((NR, D2), bf16), jax.ShapeDtypeStruct((1, D2), f32)],
        compiler_params=_cparams(("arbitrary",), VMEM_LIMIT),
        name="filter_mlp",
    )(z, *args)


def _filter_features(cfg):
    L = cfg.L
    pos = jnp.arange(cfg.NR, dtype=f32)
    t = (pos / (L - 1))[:, None]
    w = (2.0 * math.pi / L) * pos[:, None]
    bands = jnp.linspace(1e-4, FILTER_BANDS - 1, FILTER_BANDS, dtype=f32)
    z = jnp.concatenate([t, jnp.cos(w * bands), -jnp.sin(w * bands)], axis=-1)
    return jnp.pad(z, ((0, 0), (0, FILTER_EMB_PAD - FILTER_EMB_DIM)))


def hyena_filter_spectrum(cfg, tables, w1, b1, w2, b2, w3, b3, freq):
    fa, _, g_fwd, _ = tables
    z = _filter_features(cfg)
    w1p = jnp.pad(w1, ((0, FILTER_EMB_PAD - FILTER_EMB_DIM), (0, 0)))
    max_decay = math.log(1.0 / DECAY_TARGET) / FAST_DECAY_PCT
    min_decay = math.log(1.0 / DECAY_TARGET) / SLOW_DECAY_PCT
    deltas = jnp.linspace(min_decay, max_decay, D_MODEL, dtype=f32)
    xf, csum = filter_mlp(z, w1p, b1, w2, b2, w3, b3, freq, deltas, cfg)
    D2 = 2 * D_MODEL
    z2 = left_matmul(fa, xf.reshape(1, cfg.NIN, FFT_N2 * D2), bf16)
    z5 = z2.reshape(2, cfg.N1H, FFT_N2, D2)
    return fft_b_filter(z5, csum, g_fwd)


def long_conv(vg, hspec, tables, cfg):
    fa, fa_inv, g_fwd, g_inv = tables
    B = vg.shape[0]
    D = D_MODEL
    z2 = left_matmul(fa, vg.reshape(B, cfg.NIN, FFT_N2 * D), bf16)
    q5 = fft_b_fused(z2.reshape(B, 2, cfg.N1H, FFT_N2, D), hspec, g_fwd, g_inv)
    y2 = left_matmul(fa_inv, q5.reshape(B, 2 * cfg.N1H, FFT_N2 * D), bf16)
    return y2.reshape(B, cfg.NR, D)


def _qkv_proj_kernel(x_ref, w_ref, cc_ref, ss_ref, qg_ref, kg_ref, q_ref, k_ref, v_ref, *, L, tm, tpb, sub):
    nq, nk = N_HEADS * HEAD_DIM, N_KV_HEADS * HEAD_DIM
    for r in range(0, tm, sub):
        rows = slice(r, r + sub)
        t = (pl.program_id(0) % tpb) * tm + r + lax.broadcasted_iota(jnp.int32, (sub, 1), 0)
        valid = t < L
        cc, ss = cc_ref[rows, :], ss_ref[rows, :]
        qkv = jnp.dot(x_ref[rows, :], w_ref[...], preferred_element_type=f32)

        def norm_rope(x, gain, scale):
            xn = x * lax.rsqrt(jnp.mean(x * x, axis=-1, keepdims=True) + RMS_EPS) * gain
            y = xn * cc + pltpu.roll(xn, HEAD_DIM // 2, axis=1) * ss
            return jnp.where(valid, y * scale, 0.0).astype(bf16)

        for hh in range(N_HEADS):
            sl = slice(hh * HEAD_DIM, (hh + 1) * HEAD_DIM)
            q_ref[rows, sl] = norm_rope(qkv[:, sl], qg_ref[...], LOG2E * HEAD_DIM ** -0.5)
        for hh in range(N_KV_HEADS):
            sl = slice(hh * HEAD_DIM, (hh + 1) * HEAD_DIM)
            k_ref[rows, sl] = norm_rope(qkv[:, nq + hh * HEAD_DIM:nq + (hh + 1) * HEAD_DIM], kg_ref[...], 1.0)
        v_ref[rows, :] = jnp.where(valid, qkv[:, nq + nk:], 0.0).astype(bf16)


def qkv_proj(x, w, cc, ss, q_gain, k_gain, cfg):
    R, K = x.shape
    tm, tpb = cfg.TL, cfg.Lp // cfg.TL
    nq, nk = N_HEADS * HEAD_DIM, N_KV_HEADS * HEAD_DIM
    sub = _row_subtile(tm)
    row = lambda i: (i, 0)
    tab = lambda i: (i % tpb, 0)
    fixed = lambda i: (0, 0)
    return pl.pallas_call(
        functools.partial(_qkv_proj_kernel, L=cfg.L, tm=tm, tpb=tpb, sub=sub),
        grid=(R // tm,),
        in_specs=[pl.BlockSpec((tm, K), row), pl.BlockSpec((K, nq + 2 * nk), fixed), pl.BlockSpec((tm, HEAD_DIM), tab),
                  pl.BlockSpec((tm, HEAD_DIM), tab), pl.BlockSpec((1, HEAD_DIM), fixed), pl.BlockSpec((1, HEAD_DIM), fixed)],
        out_specs=[pl.BlockSpec((tm, nq), row), pl.BlockSpec((tm, nk), row), pl.BlockSpec((tm, nk), row)],
        out_shape=[jax.ShapeDtypeStruct((R, nq), bf16), jax.ShapeDtypeStruct((R, nk), bf16),
                   jax.ShapeDtypeStruct((R, nk), bf16)],
        compiler_params=_cparams(("arbitrary",), VMEM_LIMIT),
        name="qkv_proj",
    )(x, w, cc, ss, q_gain.reshape(1, -1), k_gain.reshape(1, -1))


def _flash_kernel(q_ref, k_ref, v_ref, o_ref, s0_scr, s1_scr, p0_scr, p1_scr, *, L):
    s_bufs, p_bufs = (s0_scr, s1_scr), (p0_scr, p1_scr)
    lp = k_ref.shape[0]
    c0 = (L // LANES) * LANES
    for g in range(GROUP):
        s_scr, p_scr = s_bufs[g % 2], p_bufs[g % 2]
        q = q_ref[:, g * HEAD_DIM:(g + 1) * HEAD_DIM]
        s = lax.dot_general(q, k_ref[...], (((1,), (1,)), ((), ())), preferred_element_type=f32)
        if c0 < lp:
            cols = c0 + lax.broadcasted_iota(jnp.int32, (1, lp - c0), 1)
            s_scr[:, :c0] = s[:, :c0]
            s_scr[:, c0:] = jnp.where(cols < L, s[:, c0:], NEG_BIG)
        else:
            s_scr[...] = s
        s = s_scr[...]
        p = jnp.exp2(s - jnp.max(s, axis=-1, keepdims=True))
        l = jnp.sum(p, axis=-1, keepdims=True)
        p_scr[...] = p.astype(bf16)
        o = jnp.dot(p_scr[...], v_ref[...], preferred_element_type=f32)
        o_ref[:, g * HEAD_DIM:(g + 1) * HEAD_DIM] = (o / l).astype(o_ref.dtype)


def flash_attention(q, k, v, cfg):
    B = q.shape[0]
    tq, lp = cfg.TQ, cfg.Lp
    gw = GROUP * HEAD_DIM
    return pl.pallas_call(
        functools.partial(_flash_kernel, L=cfg.L),
        grid=(B, N_KV_HEADS, lp // tq),
        in_specs=[pl.BlockSpec((None, tq, gw), lambda b, h, i: (b, i, h)),
                  pl.BlockSpec((None, lp, HEAD_DIM), lambda b, h, i: (b, 0, h)),
                  pl.BlockSpec((None, lp, HEAD_DIM), lambda b, h, i: (b, 0, h))],
        out_specs=pl.BlockSpec((None, tq, gw), lambda b, h, i: (b, i, h)),
        out_shape=jax.ShapeDtypeStruct(q.shape, bf16),
        scratch_shapes=[pltpu.VMEM((tq, lp), f32), pltpu.VMEM((tq, lp), f32),
                        pltpu.VMEM((tq, lp), bf16), pltpu.VMEM((tq, lp), bf16)],
        compiler_params=_cparams(("arbitrary", "arbitrary", "arbitrary"), VMEM_LIMIT),
        name="flash",
    )(q, k, v)


def _rope_tables(cfg):
    n = cfg.L - N_META
    rows = n // GRID_W
    row = jnp.concatenate([jnp.full((N_META,), -1.0, f32), jnp.repeat(jnp.arange(rows, dtype=f32), GRID_W)])
    col = jnp.concatenate([jnp.arange(N_META, dtype=f32), jnp.tile(jnp.arange(GRID_W, dtype=f32), rows)])
    axis_rot = HEAD_DIM // 2
    inv_freq = ROPE_THETA ** (-jnp.arange(0, axis_rot, 2, dtype=f32) / axis_rot)
    ang = jnp.concatenate([row[:, None] * inv_freq, col[:, None] * inv_freq], axis=-1)
    ang = jnp.pad(ang, ((0, cfg.Lp - cfg.L), (0, 0)))
    c, s = jnp.cos(ang), jnp.sin(ang)
    return jnp.concatenate([c, c], axis=-1), jnp.concatenate([-s, s], axis=-1)


def _run_trunk(x, cfg, meta_tokens, hy, at, ln, ffn):
    B, n, D = x.shape
    Lp = cfg.Lp
    R = B * Lp
    meta = jnp.broadcast_to(meta_tokens[None], (B, N_META, D))
    h = jnp.concatenate([meta, x, jnp.zeros((B, Lp - cfg.L, D), x.dtype)], axis=1).reshape(R, D)
    hb = h.astype(bf16)
    cc, ss = _rope_tables(cfg)
    tables = _dft_tables(cfg)
    for i in range(DEPTH):
        j = i // 2
        if i % 2 == 0:
            hspec = hyena_filter_spectrum(cfg, tables, hy["f_w1"][j], hy["f_b1"][j], hy["f_w2"][j], hy["f_b2"][j],
                                          hy["f_w3"][j], hy["f_b3"][j], hy["f_freq"][j])
            x0c, vg = hyena_in(hb, hy["w_in"][j], hy["conv_w"][j], hy["conv_b"][j], cfg)
            yc = long_conv(vg, hspec, tables, cfg)
            h, hb = hyena_out(yc, vg, x0c, hy["skip"][j], hy["w_out"][j], h,
                              ln["g1"][i], ln["b1"][i], cfg)
        else:
            q, k, v = qkv_proj(hb, at["w_qkv"][j], cc, ss, at["q_gain"][j], at["k_gain"][j], cfg)
            o = flash_attention(q.reshape(B, Lp, -1), k.reshape(B, Lp, -1), v.reshape(B, Lp, -1), cfg)
            h, hb = matmul_residual_ln(o.reshape(R, D), at["w_out"][j], h, ln["g1"][i], ln["b1"][i], cfg.TL)
        gated = ffn_in_gate(hb, ffn["w_in"][i], ffn["conv_w"][i], ffn["conv_b"][i], cfg)
        h, hb = matmul_residual_ln(gated, ffn["w_out"][i], h, ln["g2"][i], ln["b2"][i], cfg.TL)
    return h.reshape(B, Lp, D)[:, N_META:cfg.L]


def kernel(x_prompt, x_sample, meta_tokens, hy_w_in, hy_conv_w, hy_conv_b, hy_filt_w1, hy_filt_b1, hy_filt_w2, hy_filt_b2, hy_filt_w3, hy_filt_b3, hy_filt_freq, hy_skip, hy_w_out, at_w_qkv, at_q_gain, at_k_gain, at_w_out, ln1_g, ln1_b, ln2_g, ln2_b, ffn_w_in, ffn_conv_w, ffn_conv_b, ffn_w_out):
    hy = dict(w_in=hy_w_in.astype(bf16), conv_w=hy_conv_w, conv_b=hy_conv_b, f_w1=hy_filt_w1, f_b1=hy_filt_b1,
              f_w2=hy_filt_w2, f_b2=hy_filt_b2, f_w3=hy_filt_w3, f_b3=hy_filt_b3, f_freq=hy_filt_freq, skip=hy_skip,
              w_out=hy_w_out.astype(bf16))
    at = dict(w_qkv=at_w_qkv.astype(bf16), q_gain=at_q_gain, k_gain=at_k_gain, w_out=at_w_out.astype(bf16))
    ln = dict(g1=ln1_g, b1=ln1_b, g2=ln2_g, b2=ln2_b)
    ffn = dict(w_in=ffn_w_in.astype(bf16), conv_w=ffn_conv_w, conv_b=ffn_conv_b, w_out=ffn_w_out.astype(bf16))
    y_prompt = _run_trunk(x_prompt, Cfg(x_prompt.shape[1]), meta_tokens, hy, at, ln, ffn)
    y_sample = _run_trunk(x_sample, Cfg(x_sample.shape[1]), meta_tokens, hy, at, ln, ffn)
    return (y_prompt, y_sample)
```

```python
import functools
import math

import jax
import jax.numpy as jnp
from jax import lax
from jax.experimental import pallas as pl
from jax.experimental.pallas import tpu as pltpu

f32 = jnp.float32
bf16 = jnp.bfloat16

D_MODEL = 1024
DEPTH = 4
N_META = 16
GRID_W = 64
HEAD_DIM = 128
N_HEADS = 8
N_KV_HEADS = 2
GROUP = N_HEADS // N_KV_HEADS
ROPE_THETA = 10000.0
FILTER_EMB_DIM = 33
FILTER_EMB_PAD = 40
FILTER_BANDS = 16
DECAY_TARGET = 1e-2
FAST_DECAY_PCT = 0.3
SLOW_DECAY_PCT = 1.5
DECAY_SHIFT = 0.05
FILTER_NORM_EPS = 1e-6
D_FF = 2816
DEEPNORM_ALPHA = (2 * DEPTH) ** 0.25
LN_EPS = 1e-5
RMS_EPS = 1e-6
NEG_BIG = -1e30
LOG2E = 1.4426950408889634

LANES = 128
SUBLANES = 8
FFT_N2 = 128
V7X_VMEM_BYTES = 64 * 1024 * 1024
VMEM_LIMIT = 52 * 1024 * 1024
assert VMEM_LIMIT < V7X_VMEM_BYTES


class Cfg:
    def __init__(self, n):
        self.L = n + N_META
        if n == 8192:
            self.Lp, self.TL, self.TQ, self.N1, self.NIN = 8448, 768, 256, 136, 80
        elif n == 2048:
            self.Lp, self.TL, self.TQ, self.N1, self.NIN = 2304, 768, 256, 40, 32
        else:
            up = lambda a, m: -(-a // m) * m
            self.TL = self.TQ = 128
            self.Lp = up(self.L, LANES)
            self.N1 = up(-(-(2 * self.L - 1) // FFT_N2), 8)
            self.NIN = up(-(-self.Lp // FFT_N2), 16)
        self.TF = 384 if self.Lp % 384 == 0 else self.TL
        self.NR = self.NIN * FFT_N2
        self.N1H = -(-(self.N1 // 2 + 1) // SUBLANES) * SUBLANES
        self.N = self.N1 * FFT_N2
        assert self.N >= 2 * self.L - 1 and self.NR >= self.Lp
        assert self.Lp % self.TL == 0 and self.Lp % self.TQ == 0


def _cparams(sem, vmem=None):
    return pltpu.CompilerParams(dimension_semantics=sem, vmem_limit_bytes=vmem)


def _residual_ln(m, h_ref, g_ref, b_ref, o_ref, obf_ref):
    y = DEEPNORM_ALPHA * h_ref[...] + m
    mu = jnp.mean(y, axis=-1, keepdims=True)
    yc = y - mu
    var = jnp.mean(yc * yc, axis=-1, keepdims=True)
    out = yc * lax.rsqrt(var + LN_EPS) * g_ref[...] + b_ref[...]
    o_ref[...] = out
    obf_ref[...] = out.astype(bf16)


def _row_subtile(tm):
    sub = 256 if tm % 256 == 0 else tm // 2
    assert tm % sub == 0 and sub % 16 == 0
    return sub


def _mm_ln_kernel(x_ref, w_ref, h_ref, g_ref, b_ref, o_ref, obf_ref):
    sub = _row_subtile(x_ref.shape[0])
    for r in range(0, x_ref.shape[0], sub):
        rows = pl.ds(r, sub)
        _residual_ln(jnp.dot(x_ref[rows, :], w_ref[...], preferred_element_type=f32),
                     h_ref.at[rows, :], g_ref, b_ref, o_ref.at[rows, :], obf_ref.at[rows, :])


def matmul_residual_ln(x, w, h, g, b, tm):
    R, K = x.shape
    N = w.shape[1]
    row = lambda i: (i, 0)
    fixed = lambda i: (0, 0)
    return pl.pallas_call(
        _mm_ln_kernel,
        grid=(R // tm,),
        in_specs=[pl.BlockSpec((tm, K), row), pl.BlockSpec((K, N), fixed), pl.BlockSpec((tm, N), row),
                  pl.BlockSpec((1, N), fixed), pl.BlockSpec((1, N), fixed)],
        out_specs=[pl.BlockSpec((tm, N), row), pl.BlockSpec((tm, N), row)],
        out_shape=[jax.ShapeDtypeStruct((R, N), f32), jax.ShapeDtypeStruct((R, N), bf16)],
        compiler_params=_cparams(("arbitrary",), VMEM_LIMIT),
        name="mm_ln",
    )(x, w, h, g.reshape(1, N), b.reshape(1, N))


def _left_mm_kernel(a_ref, x_ref, o_ref, *, cw):
    a = a_ref[...]
    for c in range(0, x_ref.shape[1], cw):
        o_ref[:, c:c + cw] = jnp.dot(a, x_ref[:, c:c + cw], preferred_element_type=f32).astype(o_ref.dtype)


def left_matmul(a, x, out_dtype, tn=32768, cw=512):
    M, K = a.shape
    B, _, C = x.shape
    assert C % tn == 0 and tn % cw == 0
    return pl.pallas_call(
        functools.partial(_left_mm_kernel, cw=cw),
        grid=(B, C // tn),
        in_specs=[pl.BlockSpec((M, K), lambda b, j: (0, 0)), pl.BlockSpec((None, K, tn), lambda b, j: (b, 0, j))],
        out_specs=pl.BlockSpec((None, M, tn), lambda b, j: (b, 0, j)),
        out_shape=jax.ShapeDtypeStruct((B, M, C), out_dtype),
        compiler_params=_cparams(("arbitrary", "arbitrary"), VMEM_LIMIT),
        name="left_mm",
    )(a, x)


HALO = 16


def _halo_specs(tm, K, R, row_block):
    nh = tm // HALO
    last = R // HALO - 1
    return [pl.BlockSpec((tm, K), lambda *g: (row_block(*g), 0)),
            pl.BlockSpec((HALO, K), lambda *g: (jnp.maximum(row_block(*g) * nh - 1, 0), 0)),
            pl.BlockSpec((HALO, K), lambda *g: (jnp.minimum((row_block(*g) + 1) * nh, last), 0))]


def _fill_halo_tile(x_scr, x_ref, xp_ref, xn_ref, tm):
    x_scr[0:HALO, :] = xp_ref[...]
    x_scr[HALO:HALO + tm, :] = x_ref[...]
    x_scr[HALO + tm:, :] = xn_ref[...]


def _zero_rows_outside_sequence(u_scr, j, L, tm):
    assert L % tm != 0
    j_end, r_end = L // tm, HALO + L % tm
    g_end = (r_end // SUBLANES) * SUBLANES
    sub = lax.broadcasted_iota(jnp.int32, (SUBLANES, 1), 0)
    first = slice(HALO - SUBLANES, HALO)
    end = slice(g_end, g_end + SUBLANES)
    u_scr[first, :] = jnp.where((j == 0) & (sub == SUBLANES - 1), 0.0, u_scr[first, :])
    u_scr[end, :] = jnp.where((j == j_end) & (sub == r_end - g_end), 0.0, u_scr[end, :])


def _conv3_centre(u_scr, cw, cb, tm):
    return (u_scr[HALO - 1:HALO - 1 + tm, :] * cw[0:1, :] + u_scr[HALO:HALO + tm, :] * cw[1:2, :]
            + u_scr[HALO + 1:HALO + 1 + tm, :] * cw[2:3, :] + cb)


def _hyena_in_kernel(x_ref, xp_ref, xn_ref, w_ref, cw_ref, cb_ref, x0_ref, vg_ref, x_scr, u0_scr, u1_scr, uv_scr,
                     *, L, tm, tpb):
    j = pl.program_id(1)
    D = D_MODEL

    @pl.when(j < tpb)
    def _():
        _fill_halo_tile(x_scr, x_ref, xp_ref, xn_ref, tm)
        xs = x_scr[...]
        t = j * tm + lax.broadcasted_iota(jnp.int32, (tm, 1), 0)
        cw, cb = cw_ref[...], cb_ref[...]
        conv = []
        for s, scr in enumerate((u0_scr, u1_scr, uv_scr)):
            sl = slice(s * D, (s + 1) * D)
            scr[...] = jnp.dot(xs, w_ref[:, sl], preferred_element_type=f32)
            _zero_rows_outside_sequence(scr, j, L, tm)
            conv.append(_conv3_centre(scr, cw[:, sl], cb[:, sl], tm))
        x0_ref[...] = conv[0].astype(x0_ref.dtype)
        vg_ref[...] = jnp.where(t < L, conv[1] * conv[2], 0.0).astype(vg_ref.dtype)

    @pl.when(j >= tpb)
    def _():
        vg_ref[...] = jnp.zeros_like(vg_ref)


def hyena_in(x, w_in, conv_w, conv_b, cfg):
    R, K = x.shape
    D = D_MODEL
    tm, tpb, ntr = cfg.TL, cfg.Lp // cfg.TL, pl.cdiv(cfg.NR, cfg.TL)
    B = R // cfg.Lp
    row_block = lambda b, j: b * tpb + jnp.minimum(j, tpb - 1)
    fixed = lambda b, j: (0, 0)
    return pl.pallas_call(
        functools.partial(_hyena_in_kernel, L=cfg.L, tm=tm, tpb=tpb),
        grid=(B, ntr),
        in_specs=_halo_specs(tm, K, R, row_block) +
                 [pl.BlockSpec((K, 3 * D), fixed, pipeline_mode=pl.Buffered(1)),
                  pl.BlockSpec((3, 3 * D), fixed), pl.BlockSpec((1, 3 * D), fixed)],
        out_specs=[pl.BlockSpec((tm, D), lambda b, j: (row_block(b, j), 0)),
                   pl.BlockSpec((None, tm, D), lambda b, j: (b, j, 0))],
        out_shape=[jax.ShapeDtypeStruct((R, D), bf16), jax.ShapeDtypeStruct((B, cfg.NR, D), bf16)],
        scratch_shapes=[pltpu.VMEM((tm + 2 * HALO, K), bf16)] + [pltpu.VMEM((tm + 2 * HALO, D), f32)] * 3,
        compiler_params=_cparams(("arbitrary", "arbitrary"), VMEM_LIMIT),
        name="hyena_in",
    )(x, x, x, w_in, conv_w, conv_b.reshape(1, -1))


def _hyena_out_kernel(y_ref, vg_ref, x0_ref, skip_ref, w_ref, h_ref, g_ref, b_ref, o_ref, obf_ref):
    y = y_ref[...].astype(f32) + vg_ref[...].astype(f32) * skip_ref[...]
    x = (y * x0_ref[...].astype(f32)).astype(bf16)
    _residual_ln(jnp.dot(x, w_ref[...], preferred_element_type=f32), h_ref, g_ref, b_ref, o_ref, obf_ref)


def hyena_out(yc, vg, x0, skip, w, h, g, b, cfg):
    R, N = h.shape
    tm, tpb = cfg.TL, cfg.Lp // cfg.TL
    row = lambda i: (i, 0)
    padded_row = lambda i: (i // tpb, i % tpb, 0)
    fixed = lambda i: (0, 0)
    return pl.pallas_call(
        _hyena_out_kernel,
        grid=(R // tm,),
        in_specs=[pl.BlockSpec((None, tm, N), padded_row), pl.BlockSpec((None, tm, N), padded_row), pl.BlockSpec((tm, N), row),
                  pl.BlockSpec((1, N), fixed), pl.BlockSpec((N, N), fixed), pl.BlockSpec((tm, N), row),
                  pl.BlockSpec((1, N), fixed), pl.BlockSpec((1, N), fixed)],
        out_specs=[pl.BlockSpec((tm, N), row), pl.BlockSpec((tm, N), row)],
        out_shape=[jax.ShapeDtypeStruct((R, N), f32), jax.ShapeDtypeStruct((R, N), bf16)],
        compiler_params=_cparams(("arbitrary",), VMEM_LIMIT),
        name="hyena_out",
    )(yc, vg, x0, skip.reshape(1, N), w, h, g.reshape(1, N), b.reshape(1, N))


def _ffn_in_gate_kernel(x_ref, xp_ref, xn_ref, wg_ref, wa_ref, cw_ref, cb_ref, o_ref, x_scr, g_scr, *, L, tm, tpb):
    i, c = pl.program_id(0), pl.program_id(1)

    @pl.when(c == 0)
    def _():
        _fill_halo_tile(x_scr, x_ref, xp_ref, xn_ref, tm)

    g_scr[...] = jnp.dot(x_scr[...], wg_ref[...], preferred_element_type=f32)
    a = jnp.dot(x_ref[...], wa_ref[...], preferred_element_type=f32)
    _zero_rows_outside_sequence(g_scr, i % tpb, L, tm)
    g = _conv3_centre(g_scr, cw_ref[...], cb_ref[...], tm)
    gelu = 0.5 * g * (1.0 + lax.erf(g * (2.0 ** -0.5)))
    o_ref[...] = (gelu * a).astype(o_ref.dtype)


def ffn_in_gate(x, w_in, conv_w, conv_b, cfg, nchunk=1):
    R, K = x.shape
    tm = cfg.TL
    tc = D_FF // nchunk
    tpb = cfg.Lp // tm
    assert D_FF % nchunk == 0 and tc % LANES == 0 and tm % HALO == 0
    wmode = dict(pipeline_mode=pl.Buffered(1)) if nchunk == 1 else {}
    return pl.pallas_call(
        functools.partial(_ffn_in_gate_kernel, L=cfg.L, tm=tm, tpb=tpb),
        grid=(R // tm, nchunk),
        in_specs=_halo_specs(tm, K, R, lambda i, c: i) + [
                  pl.BlockSpec((K, tc), lambda i, c: (0, c), **wmode),
                  pl.BlockSpec((K, tc), lambda i, c: (0, nchunk + c), **wmode),
                  pl.BlockSpec((3, tc), lambda i, c: (0, c)), pl.BlockSpec((1, tc), lambda i, c: (0, c))],
        out_specs=pl.BlockSpec((tm, tc), lambda i, c: (i, c)),
        out_shape=jax.ShapeDtypeStruct((R, D_FF), bf16),
        scratch_shapes=[pltpu.VMEM((tm + 2 * HALO, K), bf16), pltpu.VMEM((tm + 2 * HALO, tc), f32)],
        compiler_params=_cparams(("arbitrary", "arbitrary"), VMEM_LIMIT),
        name="ffn_in_gate",
    )(x, x, x, w_in, w_in, conv_w, conv_b.reshape(1, -1))


def _ffn_full_kernel(x_ref, xp_ref, xn_ref, wg_ref, wa_ref, cw_ref, cb_ref, wo_ref, h_ref, lg_ref, lb_ref, o_ref, obf_ref,
                     x_scr, *g_scrs, L, tm, tpb, nch):
    i = pl.program_id(0)
    _fill_halo_tile(x_scr, x_ref, xp_ref, xn_ref, tm)
    tc = D_FF // nch
    m = None
    for c in range(nch):
        sl = slice(c * tc, (c + 1) * tc)
        g_scr = g_scrs[c]
        g_scr[...] = jnp.dot(x_scr[...], wg_ref[:, sl], preferred_element_type=f32)
        a = jnp.dot(x_ref[...], wa_ref[:, sl], preferred_element_type=f32)
        _zero_rows_outside_sequence(g_scr, i % tpb, L, tm)
        g = _conv3_centre(g_scr, cw_ref[:, sl], cb_ref[:, sl], tm)
        gated = (0.5 * g * (1.0 + lax.erf(g * (2.0 ** -0.5))) * a).astype(bf16)
        part = jnp.dot(gated, wo_ref[sl, :], preferred_element_type=f32)
        m = part if m is None else m + part
    _residual_ln(m, h_ref, lg_ref, lb_ref, o_ref, obf_ref)


def ffn_full(x, w_in, conv_w, conv_b, w_out, h, g, b, cfg, nch=1):
    R, K = x.shape
    N = w_out.shape[1]
    tm = cfg.TF
    tpb = cfg.Lp // tm
    tc = D_FF // nch
    assert D_FF % nch == 0 and tc % LANES == 0 and tm % HALO == 0 and cfg.Lp % tm == 0
    one = dict(pipeline_mode=pl.Buffered(1))
    row = lambda i: (i, 0)
    fixed = lambda i: (0, 0)
    return pl.pallas_call(
        functools.partial(_ffn_full_kernel, L=cfg.L, tm=tm, tpb=tpb, nch=nch),
        grid=(R // tm,),
        in_specs=_halo_specs(tm, K, R, lambda i: i) + [
                  pl.BlockSpec((K, D_FF), fixed, **one), pl.BlockSpec((K, D_FF), lambda i: (0, 1), **one),
                  pl.BlockSpec((3, D_FF), fixed), pl.BlockSpec((1, D_FF), fixed),
                  pl.BlockSpec((D_FF, N), fixed, **one), pl.BlockSpec((tm, N), row),
                  pl.BlockSpec((1, N), fixed), pl.BlockSpec((1, N), fixed)],
        out_specs=[pl.BlockSpec((tm, N), row), pl.BlockSpec((tm, N), row)],
        out_shape=[jax.ShapeDtypeStruct((R, N), f32), jax.ShapeDtypeStruct((R, N), bf16)],
        scratch_shapes=[pltpu.VMEM((tm + 2 * HALO, K), bf16)] + [pltpu.VMEM((tm + 2 * HALO, tc), f32)] * nch,
        compiler_params=_cparams(("arbitrary",), VMEM_LIMIT),
        name="ffn_full",
    )(x, x, x, w_in, w_in, conv_w, conv_b.reshape(1, -1), w_out, h, g.reshape(1, N), b.reshape(1, N))


def _dft_tables(cfg):
    N1, NIN, N2, N = cfg.N1, cfg.NIN, FFT_N2, cfg.N
    assert N1 % 2 == 0
    k1 = jnp.arange(cfg.N1H, dtype=jnp.int32)
    wk = jnp.where((k1 == 0) | (k1 == N1 // 2), 1.0, jnp.where(k1 < N1 // 2, 2.0, 0.0)).astype(f32)
    n1 = jnp.arange(NIN, dtype=jnp.int32)
    ang = (2.0 * math.pi / N1) * ((k1[:, None] * n1[None, :]) % N1).astype(f32)
    ca, sa = jnp.cos(ang), jnp.sin(ang)
    fa = jnp.concatenate([ca, -sa], axis=0)
    fa_inv = jnp.concatenate([ca.T * wk, -sa.T * wk], axis=1) * (1.0 / N)
    k2 = jnp.arange(N2, dtype=jnp.int32)[:, None]
    n2 = jnp.arange(N2, dtype=jnp.int32)[None, :]
    t1 = (2.0 * math.pi / N2) * ((n2 * k2) % N2).astype(f32)
    t2 = (2.0 * math.pi / N) * ((n2 * k1[:, None]) % N).astype(f32)
    c1, s1 = jnp.cos(t1)[None], jnp.sin(t1)[None]
    c2, s2 = jnp.cos(t2)[:, None, :], jnp.sin(t2)[:, None, :]
    ar, ai = c1 * c2 - s1 * s2, -(s1 * c2 + c1 * s2)
    g_fwd = jnp.concatenate([jnp.concatenate([ar, -ai], axis=2), jnp.concatenate([ai, ar], axis=2)], axis=1)
    art, ait = jnp.swapaxes(ar, 1, 2), -jnp.swapaxes(ai, 1, 2)
    g_inv = jnp.concatenate([jnp.concatenate([art, -ait], axis=2), jnp.concatenate([ait, art], axis=2)], axis=1)
    return fa.astype(bf16), fa_inv.astype(bf16), g_fwd.astype(bf16), g_inv.astype(bf16)


FFT_KB = 4


def _fft_b_fused_kernel(z_ref, h_ref, gf_ref, gi_ref, o_ref):
    for kb in range(FFT_KB):
        zc = jnp.concatenate([z_ref[0, kb], z_ref[1, kb]], axis=0)
        y = jnp.dot(gf_ref[kb], zc, preferred_element_type=f32)
        yr, yi = y[:FFT_N2], y[FFT_N2:]
        hr, hi = h_ref[0, kb], h_ref[1, kb]
        pc = jnp.concatenate([yr * hr - yi * hi, yr * hi + yi * hr], axis=0).astype(bf16)
        q = jnp.dot(gi_ref[kb], pc, preferred_element_type=f32)
        o_ref[0, kb] = q[:FFT_N2].astype(o_ref.dtype)
        o_ref[1, kb] = q[FFT_N2:].astype(o_ref.dtype)


def fft_b_fused(z5, hspec, g_fwd, g_inv):
    B, _, N1, N2, D = z5.shape
    assert N1 % FFT_KB == 0
    zspec = pl.BlockSpec((None, 2, FFT_KB, N2, D), lambda k, b: (b, 0, k, 0, 0))
    gspec = pl.BlockSpec((FFT_KB, 2 * N2, 2 * N2), lambda k, b: (k, 0, 0))
    return pl.pallas_call(
        _fft_b_fused_kernel,
        grid=(N1 // FFT_KB, B),
        in_specs=[zspec, pl.BlockSpec((2, FFT_KB, N2, D), lambda k, b: (0, k, 0, 0)), gspec, gspec],
        out_specs=zspec,
        out_shape=jax.ShapeDtypeStruct(z5.shape, bf16),
        compiler_params=_cparams(("arbitrary", "arbitrary"), VMEM_LIMIT),
        name="fft_b_fused",
    )(z5, hspec, g_fwd, g_inv)


def _fft_b_filter_kernel(z_ref, csum_ref, gf_ref, o_ref):
    D = D_MODEL
    zc = z_ref[...].reshape(2 * FFT_N2, 2 * D)
    y = jnp.dot(gf_ref[...], zc, preferred_element_type=f32)
    scale = 1.0 / (csum_ref[...] + FILTER_NORM_EPS)
    y = y * scale
    yr, yi = y[:FFT_N2], y[FFT_N2:]
    o_ref[0] = yr[:, :D] + yr[:, D:]
    o_ref[1] = yi[:, :D] - yi[:, D:]


def fft_b_filter(z5, csum, g_fwd):
    _, N1, N2, D2 = z5.shape
    return pl.pallas_call(
        _fft_b_filter_kernel,
        grid=(N1,),
        in_specs=[pl.BlockSpec((2, None, N2, D2), lambda k: (0, k, 0, 0)), pl.BlockSpec((1, D2), lambda k: (0, 0)),
                  pl.BlockSpec((None, 2 * N2, 2 * N2), lambda k: (k, 0, 0))],
        out_specs=pl.BlockSpec((2, None, N2, D2 // 2), lambda k: (0, k, 0, 0)),
        out_shape=jax.ShapeDtypeStruct((2, N1, N2, D2 // 2), f32),
        compiler_params=_cparams(("arbitrary",), VMEM_LIMIT),
        name="fft_b_filter",
    )(z5, csum, g_fwd)


def _filter_mlp_kernel(z_ref, w1_ref, b1_ref, w2_ref, b2_ref, w3_ref, b3_ref, fr_ref, dl_ref, o_ref, cs_ref, *, L, tf):
    i = pl.program_id(0)
    hp = lax.Precision.HIGHEST
    z = z_ref[...]
    fr = fr_ref[...]
    h = jnp.sin(fr * (jnp.dot(z, w1_ref[...], preferred_element_type=f32, precision=hp) + b1_ref[...]))
    h = jnp.sin(fr * (jnp.dot(h, w2_ref[...], preferred_element_type=f32, precision=hp) + b2_ref[...]))
    h = jnp.dot(h.astype(bf16), w3_ref[...].astype(bf16), preferred_element_type=f32) + b3_ref[...]
    win = jnp.exp(-z[:, 0:1] * dl_ref[...]) + DECAY_SHIFT
    h = h * jnp.concatenate([win, win], axis=1)
    rows = i * tf + lax.broadcasted_iota(jnp.int32, (tf, 1), 0)
    h = jnp.where(rows < L, h, 0.0)

    @pl.when(i == 0)
    def _():
        cs_ref[...] = jnp.zeros_like(cs_ref)

    cs_ref[...] += jnp.sum(jnp.abs(h), axis=0, keepdims=True)
    lanes = lax.broadcasted_iota(jnp.int32, (1, h.shape[1]), 1)
    o_ref[...] = jnp.where((rows == 0) & (lanes >= D_MODEL), 0.0, h).astype(o_ref.dtype)


def filter_mlp(z, w1, b1, w2, b2, w3, b3, freq, deltas, cfg, tf=512):
    NR = cfg.NR
    D2 = 2 * D_MODEL
    fixed = lambda i: (0, 0)
    full = lambda a: pl.BlockSpec(a.shape, fixed)
    args = (w1, b1.reshape(1, -1), w2, b2.reshape(1, -1), w3, b3.reshape(1, -1), freq.reshape(1, -1), deltas.reshape(1, -1))
    return pl.pallas_call(
        functools.partial(_filter_mlp_kernel, L=cfg.L, tf=tf),
        grid=(NR // tf,),
        in_specs=[pl.BlockSpec((tf, FILTER_EMB_PAD), lambda i: (i, 0))] + [full(a) for a in args],
        out_specs=[pl.BlockSpec((tf, D2), lambda i: (i, 0)), pl.BlockSpec((1, D2), fixed)],
        out_shape=[jax.ShapeDtypeStruct((NR, D2), bf16), jax.ShapeDtypeStruct((1, D2), f32)],
        compiler_params=_cparams(("arbitrary",), VMEM_LIMIT),
        name="filter_mlp",
    )(z, *args)


def _filter_features(cfg):
    L = cfg.L
    pos = jnp.arange(cfg.NR, dtype=f32)
    t = (pos / (L - 1))[:, None]
    w = (2.0 * math.pi / L) * pos[:, None]
    bands = jnp.linspace(1e-4, FILTER_BANDS - 1, FILTER_BANDS, dtype=f32)
    z = jnp.concatenate([t, jnp.cos(w * bands), -jnp.sin(w * bands)], axis=-1)
    return jnp.pad(z, ((0, 0), (0, FILTER_EMB_PAD - FILTER_EMB_DIM)))


def hyena_filter_spectrum(cfg, tables, w1, b1, w2, b2, w3, b3, freq):
    fa, _, g_fwd, _ = tables
    z = _filter_features(cfg)
    w1p = jnp.pad(w1, ((0, FILTER_EMB_PAD - FILTER_EMB_DIM), (0, 0)))
    max_decay = math.log(1.0 / DECAY_TARGET) / FAST_DECAY_PCT
    min_decay = math.log(1.0 / DECAY_TARGET) / SLOW_DECAY_PCT
    deltas = jnp.linspace(min_decay, max_decay, D_MODEL, dtype=f32)
    xf, csum = filter_mlp(z, w1p, b1, w2, b2, w3, b3, freq, deltas, cfg)
    D2 = 2 * D_MODEL
    z2 = left_matmul(fa, xf.reshape(1, cfg.NIN, FFT_N2 * D2), bf16)
    z5 = z2.reshape(2, cfg.N1H, FFT_N2, D2)
    return fft_b_filter(z5, csum, g_fwd)


def long_conv(vg, hspec, tables, cfg):
    fa, fa_inv, g_fwd, g_inv = tables
    B = vg.shape[0]
    D = D_MODEL
    z2 = left_matmul(fa, vg.reshape(B, cfg.NIN, FFT_N2 * D), bf16)
    q5 = fft_b_fused(z2.reshape(B, 2, cfg.N1H, FFT_N2, D), hspec, g_fwd, g_inv)
    y2 = left_matmul(fa_inv, q5.reshape(B, 2 * cfg.N1H, FFT_N2 * D), bf16)
    return y2.reshape(B, cfg.NR, D)


def _qkv_proj_kernel(x_ref, w_ref, cc_ref, ss_ref, qg_ref, kg_ref, q_ref, k_ref, v_ref, *, L, tm, tpb, sub):
    nq, nk = N_HEADS * HEAD_DIM, N_KV_HEADS * HEAD_DIM
    for r in range(0, tm, sub):
        rows = slice(r, r + sub)
        t = (pl.program_id(0) % tpb) * tm + r + lax.broadcasted_iota(jnp.int32, (sub, 1), 0)
        valid = t < L
        cc, ss = cc_ref[rows, :], ss_ref[rows, :]
        qkv = jnp.dot(x_ref[rows, :], w_ref[...], preferred_element_type=f32)

        def norm_rope(x, gain, scale):
            xn = x * lax.rsqrt(jnp.mean(x * x, axis=-1, keepdims=True) + RMS_EPS) * gain
            y = xn * cc + pltpu.roll(xn, HEAD_DIM // 2, axis=1) * ss
            return jnp.where(valid, y * scale, 0.0).astype(bf16)

        for hh in range(N_HEADS):
            sl = slice(hh * HEAD_DIM, (hh + 1) * HEAD_DIM)
            q_ref[rows, sl] = norm_rope(qkv[:, sl], qg_ref[...], LOG2E * HEAD_DIM ** -0.5)
        for hh in range(N_KV_HEADS):
            sl = slice(hh * HEAD_DIM, (hh + 1) * HEAD_DIM)
            k_ref[rows, sl] = norm_rope(qkv[:, nq + hh * HEAD_DIM:nq + (hh + 1) * HEAD_DIM], kg_ref[...], 1.0)
        v_ref[rows, :] = jnp.where(valid, qkv[:, nq + nk:], 0.0).astype(bf16)


def qkv_proj(x, w, cc, ss, q_gain, k_gain, cfg):
    R, K = x.shape
    tm, tpb = cfg.TL, cfg.Lp // cfg.TL
    nq, nk = N_HEADS * HEAD_DIM, N_KV_HEADS * HEAD_DIM
    sub = _row_subtile(tm)
    row = lambda i: (i, 0)
    tab = lambda i: (i % tpb, 0)
    fixed = lambda i: (0, 0)
    return pl.pallas_call(
        functools.partial(_qkv_proj_kernel, L=cfg.L, tm=tm, tpb=tpb, sub=sub),
        grid=(R // tm,),
        in_specs=[pl.BlockSpec((tm, K), row), pl.BlockSpec((K, nq + 2 * nk), fixed), pl.BlockSpec((tm, HEAD_DIM), tab),
                  pl.BlockSpec((tm, HEAD_DIM), tab), pl.BlockSpec((1, HEAD_DIM), fixed), pl.BlockSpec((1, HEAD_DIM), fixed)],
        out_specs=[pl.BlockSpec((tm, nq), row), pl.BlockSpec((tm, nk), row), pl.BlockSpec((tm, nk), row)],
        out_shape=[jax.ShapeDtypeStruct((R, nq), bf16), jax.ShapeDtypeStruct((R, nk), bf16),
                   jax.ShapeDtypeStruct((R, nk), bf16)],
        compiler_params=_cparams(("arbitrary",), VMEM_LIMIT),
        name="qkv_proj",
    )(x, w, cc, ss, q_gain.reshape(1, -1), k_gain.reshape(1, -1))


def _flash_kernel(q_ref, k_ref, v_ref, o_ref, s0_scr, s1_scr, p0_scr, p1_scr, *, L):
    s_bufs, p_bufs = (s0_scr, s1_scr), (p0_scr, p1_scr)
    lp = k_ref.shape[0]
    c0 = (L // LANES) * LANES
    for g in range(GROUP):
        s_scr, p_scr = s_bufs[g % 2], p_bufs[g % 2]
        q = q_ref[:, g * HEAD_DIM:(g + 1) * HEAD_DIM]
        s = lax.dot_general(q, k_ref[...], (((1,), (1,)), ((), ())), preferred_element_type=f32)
        if c0 < lp:
            cols = c0 + lax.broadcasted_iota(jnp.int32, (1, lp - c0), 1)
            s_scr[:, :c0] = s[:, :c0]
            s_scr[:, c0:] = jnp.where(cols < L, s[:, c0:], NEG_BIG)
        else:
            s_scr[...] = s
        s = s_scr[...]
        p = jnp.exp2(s - jnp.max(s, axis=-1, keepdims=True))
        l = jnp.sum(p, axis=-1, keepdims=True)
        p_scr[...] = p.astype(bf16)
        o = jnp.dot(p_scr[...], v_ref[...], preferred_element_type=f32)
        o_ref[:, g * HEAD_DIM:(g + 1) * HEAD_DIM] = (o / l).astype(o_ref.dtype)


def flash_attention(q, k, v, cfg):
    B = q.shape[0]
    tq, lp = cfg.TQ, cfg.Lp
    gw = GROUP * HEAD_DIM
    return pl.pallas_call(
        functools.partial(_flash_kernel, L=cfg.L),
        grid=(B, N_KV_HEADS, lp // tq),
        in_specs=[pl.BlockSpec((None, tq, gw), lambda b, h, i: (b, i, h)),
                  pl.BlockSpec((None, lp, HEAD_DIM), lambda b, h, i: (b, 0, h)),
                  pl.BlockSpec((None, lp, HEAD_DIM), lambda b, h, i: (b, 0, h))],
        out_specs=pl.BlockSpec((None, tq, gw), lambda b, h, i: (b, i, h)),
        out_shape=jax.ShapeDtypeStruct(q.shape, bf16),
        scratch_shapes=[pltpu.VMEM((tq, lp), f32), pltpu.VMEM((tq, lp), f32),
                        pltpu.VMEM((tq, lp), bf16), pltpu.VMEM((tq, lp), bf16)],
        compiler_params=_cparams(("arbitrary", "arbitrary", "arbitrary"), VMEM_LIMIT),
        name="flash",
    )(q, k, v)


def _rope_tables(cfg):
    n = cfg.L - N_META
    rows = n // GRID_W
    row = jnp.concatenate([jnp.full((N_META,), -1.0, f32), jnp.repeat(jnp.arange(rows, dtype=f32), GRID_W)])
    col = jnp.concatenate([jnp.arange(N_META, dtype=f32), jnp.tile(jnp.arange(GRID_W, dtype=f32), rows)])
    axis_rot = HEAD_DIM // 2
    inv_freq = ROPE_THETA ** (-jnp.arange(0, axis_rot, 2, dtype=f32) / axis_rot)
    ang = jnp.concatenate([row[:, None] * inv_freq, col[:, None] * inv_freq], axis=-1)
    ang = jnp.pad(ang, ((0, cfg.Lp - cfg.L), (0, 0)))
    c, s = jnp.cos(ang), jnp.sin(ang)
    return jnp.concatenate([c, c], axis=-1), jnp.concatenate([-s, s], axis=-1)


def _run_trunk(x, cfg, meta_tokens, hy, at, ln, ffn):
    B, n, D = x.shape
    Lp = cfg.Lp
    R = B * Lp
    meta = jnp.broadcast_to(meta_tokens[None], (B, N_META, D))
    h = jnp.concatenate([meta, x, jnp.zeros((B, Lp - cfg.L, D), x.dtype)], axis=1).reshape(R, D)
    hb = h.astype(bf16)
    cc, ss = _rope_tables(cfg)
    tables = _dft_tables(cfg)
    for i in range(DEPTH):
        j = i // 2
        if i % 2 == 0:
            hspec = hyena_filter_spectrum(cfg, tables, hy["f_w1"][j], hy["f_b1"][j], hy["f_w2"][j], hy["f_b2"][j],
                                          hy["f_w3"][j], hy["f_b3"][j], hy["f_freq"][j])
            x0c, vg = hyena_in(hb, hy["w_in"][j], hy["conv_w"][j], hy["conv_b"][j], cfg)
            yc = long_conv(vg, hspec, tables, cfg)
            h, hb = hyena_out(yc, vg, x0c, hy["skip"][j], hy["w_out"][j], h,
                              ln["g1"][i], ln["b1"][i], cfg)
        else:
            q, k, v = qkv_proj(hb, at["w_qkv"][j], cc, ss, at["q_gain"][j], at["k_gain"][j], cfg)
            o = flash_attention(q.reshape(B, Lp, -1), k.reshape(B, Lp, -1), v.reshape(B, Lp, -1), cfg)
            h, hb = matmul_residual_ln(o.reshape(R, D), at["w_out"][j], h, ln["g1"][i], ln["b1"][i], cfg.TL)
        h, hb = ffn_full(hb, ffn["w_in"][i], ffn["conv_w"][i], ffn["conv_b"][i], ffn["w_out"][i], h,
                         ln["g2"][i], ln["b2"][i], cfg)
    return h.reshape(B, Lp, D)[:, N_META:cfg.L]


def kernel(x_prompt, x_sample, meta_tokens, hy_w_in, hy_conv_w, hy_conv_b, hy_filt_w1, hy_filt_b1, hy_filt_w2, hy_filt_b2, hy_filt_w3, hy_filt_b3, hy_filt_freq, hy_skip, hy_w_out, at_w_qkv, at_q_gain, at_k_gain, at_w_out, ln1_g, ln1_b, ln2_g, ln2_b, ffn_w_in, ffn_conv_w, ffn_conv_b, ffn_w_out):
    hy = dict(w_in=hy_w_in.astype(bf16), conv_w=hy_conv_w, conv_b=hy_conv_b, f_w1=hy_filt_w1, f_b1=hy_filt_b1,
              f_w2=hy_filt_w2, f_b2=hy_filt_b2, f_w3=hy_filt_w3, f_b3=hy_filt_b3, f_freq=hy_filt_freq, skip=hy_skip,
              w_out=hy_w_out.astype(bf16))
    at = dict(w_qkv=at_w_qkv.astype(bf16), q_gain=at_q_gain, k_gain=at_k_gain, w_out=at_w_out.astype(bf16))
    ln = dict(g1=ln1_g, b1=ln1_b, g2=ln2_g, b2=ln2_b)
    ffn = dict(w_in=ffn_w_in.astype(bf16), conv_w=ffn_conv_w, conv_b=ffn_conv_b, w_out=ffn_w_out.astype(bf16))
    y_prompt = _run_trunk(x_prompt, Cfg(x_prompt.shape[1]), meta_tokens, hy, at, ln, ffn)
    y_sample = _run_trunk(x_sample, Cfg(x_sample.shape[1]), meta_tokens, hy, at, ln, ffn)
    return (y_prompt, y_sample)
```

```python
import functools
import math

import jax
import jax.numpy as jnp
from jax import lax
from jax.experimental import pallas as pl
from jax.experimental.pallas import tpu as pltpu

f32 = jnp.float32
bf16 = jnp.bfloat16

D_MODEL = 1024
DEPTH = 4
N_META = 16
GRID_W = 64
HEAD_DIM = 128
N_HEADS = 8
N_KV_HEADS = 2
GROUP = N_HEADS // N_KV_HEADS
ROPE_THETA = 10000.0
FILTER_EMB_DIM = 33
FILTER_EMB_PAD = 40
FILTER_BANDS = 16
DECAY_TARGET = 1e-2
FAST_DECAY_PCT = 0.3
SLOW_DECAY_PCT = 1.5
DECAY_SHIFT = 0.05
FILTER_NORM_EPS = 1e-6
D_FF = 2816
DEEPNORM_ALPHA = (2 * DEPTH) ** 0.25
LN_EPS = 1e-5
RMS_EPS = 1e-6
NEG_BIG = -1e30
LOG2E = 1.4426950408889634

LANES = 128
SUBLANES = 8
FFT_N2 = 128
V7X_VMEM_BYTES = 64 * 1024 * 1024
VMEM_LIMIT = 52 * 1024 * 1024
assert VMEM_LIMIT < V7X_VMEM_BYTES


class Cfg:
    def __init__(self, n):
        self.L = n + N_META
        if n == 8192:
            self.Lp, self.TL, self.TQ, self.N1, self.NIN = 8448, 768, 256, 136, 80
        elif n == 2048:
            self.Lp, self.TL, self.TQ, self.N1, self.NIN = 2304, 768, 768, 40, 32
        else:
            up = lambda a, m: -(-a // m) * m
            self.TL = self.TQ = 128
            self.Lp = up(self.L, LANES)
            self.N1 = up(-(-(2 * self.L - 1) // FFT_N2), 8)
            self.NIN = up(-(-self.Lp // FFT_N2), 16)
        self.NR = self.NIN * FFT_N2
        self.N1H = -(-(self.N1 // 2 + 1) // SUBLANES) * SUBLANES
        self.N = self.N1 * FFT_N2
        assert self.N >= 2 * self.L - 1 and self.NR >= self.Lp
        assert self.Lp % self.TL == 0 and self.Lp % self.TQ == 0


def _cparams(sem, vmem=None):
    return pltpu.CompilerParams(dimension_semantics=sem, vmem_limit_bytes=vmem)


def _residual_ln(m, h_ref, g_ref, b_ref, o_ref, obf_ref):
    y = DEEPNORM_ALPHA * h_ref[...] + m
    mu = jnp.mean(y, axis=-1, keepdims=True)
    yc = y - mu
    var = jnp.mean(yc * yc, axis=-1, keepdims=True)
    out = yc * lax.rsqrt(var + LN_EPS) * g_ref[...] + b_ref[...]
    o_ref[...] = out
    obf_ref[...] = out.astype(bf16)


def _row_subtile(tm):
    sub = 256 if tm % 256 == 0 else tm // 2
    assert tm % sub == 0 and sub % 16 == 0
    return sub


def _mm_ln_kernel(x_ref, w_ref, h_ref, g_ref, b_ref, o_ref, obf_ref):
    sub = _row_subtile(x_ref.shape[0])
    for r in range(0, x_ref.shape[0], sub):
        rows = pl.ds(r, sub)
        _residual_ln(jnp.dot(x_ref[rows, :], w_ref[...], preferred_element_type=f32),
                     h_ref.at[rows, :], g_ref, b_ref, o_ref.at[rows, :], obf_ref.at[rows, :])


def matmul_residual_ln(x, w, h, g, b, tm):
    R, K = x.shape
    N = w.shape[1]
    row = lambda i: (i, 0)
    fixed = lambda i: (0, 0)
    return pl.pallas_call(
        _mm_ln_kernel,
        grid=(R // tm,),
        in_specs=[pl.BlockSpec((tm, K), row), pl.BlockSpec((K, N), fixed), pl.BlockSpec((tm, N), row),
                  pl.BlockSpec((1, N), fixed), pl.BlockSpec((1, N), fixed)],
        out_specs=[pl.BlockSpec((tm, N), row), pl.BlockSpec((tm, N), row)],
        out_shape=[jax.ShapeDtypeStruct((R, N), f32), jax.ShapeDtypeStruct((R, N), bf16)],
        compiler_params=_cparams(("arbitrary",), VMEM_LIMIT),
        name="mm_ln",
    )(x, w, h, g.reshape(1, N), b.reshape(1, N))


def _left_mm_kernel(a_ref, x_ref, o_ref, *, cw):
    a = a_ref[...]
    for c in range(0, x_ref.shape[1], cw):
        o_ref[:, c:c + cw] = jnp.dot(a, x_ref[:, c:c + cw], preferred_element_type=f32).astype(o_ref.dtype)


def left_matmul(a, x, out_dtype, tn=32768, cw=512):
    M, K = a.shape
    B, _, C = x.shape
    assert C % tn == 0 and tn % cw == 0
    return pl.pallas_call(
        functools.partial(_left_mm_kernel, cw=cw),
        grid=(B, C // tn),
        in_specs=[pl.BlockSpec((M, K), lambda b, j: (0, 0)), pl.BlockSpec((None, K, tn), lambda b, j: (b, 0, j))],
        out_specs=pl.BlockSpec((None, M, tn), lambda b, j: (b, 0, j)),
        out_shape=jax.ShapeDtypeStruct((B, M, C), out_dtype),
        compiler_params=_cparams(("arbitrary", "arbitrary"), VMEM_LIMIT),
        name="left_mm",
    )(a, x)


HALO = 16


def _halo_specs(tm, K, R, row_block):
    nh = tm // HALO
    last = R // HALO - 1
    return [pl.BlockSpec((tm, K), lambda *g: (row_block(*g), 0)),
            pl.BlockSpec((HALO, K), lambda *g: (jnp.maximum(row_block(*g) * nh - 1, 0), 0)),
            pl.BlockSpec((HALO, K), lambda *g: (jnp.minimum((row_block(*g) + 1) * nh, last), 0))]


def _fill_halo_tile(x_scr, x_ref, xp_ref, xn_ref, tm):
    x_scr[0:HALO, :] = xp_ref[...]
    x_scr[HALO:HALO + tm, :] = x_ref[...]
    x_scr[HALO + tm:, :] = xn_ref[...]


def _zero_rows_outside_sequence(u_scr, j, L, tm):
    assert L % tm != 0
    j_end, r_end = L // tm, HALO + L % tm
    g_end = (r_end // SUBLANES) * SUBLANES
    sub = lax.broadcasted_iota(jnp.int32, (SUBLANES, 1), 0)
    first = slice(HALO - SUBLANES, HALO)
    end = slice(g_end, g_end + SUBLANES)
    u_scr[first, :] = jnp.where((j == 0) & (sub == SUBLANES - 1), 0.0, u_scr[first, :])
    u_scr[end, :] = jnp.where((j == j_end) & (sub == r_end - g_end), 0.0, u_scr[end, :])


def _conv3_centre(u_scr, cw, cb, tm):
    return (u_scr[HALO - 1:HALO - 1 + tm, :] * cw[0:1, :] + u_scr[HALO:HALO + tm, :] * cw[1:2, :]
            + u_scr[HALO + 1:HALO + 1 + tm, :] * cw[2:3, :] + cb)


def _hyena_in_kernel(x_ref, xp_ref, xn_ref, w_ref, cw_ref, cb_ref, x0_ref, vg_ref, x_scr, u0_scr, u1_scr, uv_scr,
                     *, L, tm, tpb):
    j = pl.program_id(1)
    D = D_MODEL

    @pl.when(j < tpb)
    def _():
        _fill_halo_tile(x_scr, x_ref, xp_ref, xn_ref, tm)
        xs = x_scr[...]
        t = j * tm + lax.broadcasted_iota(jnp.int32, (tm, 1), 0)
        cw, cb = cw_ref[...], cb_ref[...]
        conv = []
        for s, scr in enumerate((u0_scr, u1_scr, uv_scr)):
            sl = slice(s * D, (s + 1) * D)
            scr[...] = jnp.dot(xs, w_ref[:, sl], preferred_element_type=f32)
            _zero_rows_outside_sequence(scr, j, L, tm)
            conv.append(_conv3_centre(scr, cw[:, sl], cb[:, sl], tm))
        x0_ref[...] = conv[0].astype(x0_ref.dtype)
        vg_ref[...] = jnp.where(t < L, conv[1] * conv[2], 0.0).astype(vg_ref.dtype)

    @pl.when(j >= tpb)
    def _():
        vg_ref[...] = jnp.zeros_like(vg_ref)


def hyena_in(x, w_in, conv_w, conv_b, cfg):
    R, K = x.shape
    D = D_MODEL
    tm, tpb, ntr = cfg.TL, cfg.Lp // cfg.TL, pl.cdiv(cfg.NR, cfg.TL)
    B = R // cfg.Lp
    row_block = lambda b, j: b * tpb + jnp.minimum(j, tpb - 1)
    fixed = lambda b, j: (0, 0)
    return pl.pallas_call(
        functools.partial(_hyena_in_kernel, L=cfg.L, tm=tm, tpb=tpb),
        grid=(B, ntr),
        in_specs=_halo_specs(tm, K, R, row_block) +
                 [pl.BlockSpec((K, 3 * D), fixed, pipeline_mode=pl.Buffered(1)),
                  pl.BlockSpec((3, 3 * D), fixed), pl.BlockSpec((1, 3 * D), fixed)],
        out_specs=[pl.BlockSpec((tm, D), lambda b, j: (row_block(b, j), 0)),
                   pl.BlockSpec((None, tm, D), lambda b, j: (b, j, 0))],
        out_shape=[jax.ShapeDtypeStruct((R, D), bf16), jax.ShapeDtypeStruct((B, cfg.NR, D), bf16)],
        scratch_shapes=[pltpu.VMEM((tm + 2 * HALO, K), bf16)] + [pltpu.VMEM((tm + 2 * HALO, D), f32)] * 3,
        compiler_params=_cparams(("arbitrary", "arbitrary"), VMEM_LIMIT),
        name="hyena_in",
    )(x, x, x, w_in, conv_w, conv_b.reshape(1, -1))


def _hyena_out_kernel(y_ref, vg_ref, x0_ref, skip_ref, w_ref, h_ref, g_ref, b_ref, o_ref, obf_ref):
    y = y_ref[...].astype(f32) + vg_ref[...].astype(f32) * skip_ref[...]
    x = (y * x0_ref[...].astype(f32)).astype(bf16)
    _residual_ln(jnp.dot(x, w_ref[...], preferred_element_type=f32), h_ref, g_ref, b_ref, o_ref, obf_ref)


def hyena_out(yc, vg, x0, skip, w, h, g, b, cfg):
    R, N = h.shape
    tm, tpb = cfg.TL, cfg.Lp // cfg.TL
    row = lambda i: (i, 0)
    padded_row = lambda i: (i // tpb, i % tpb, 0)
    fixed = lambda i: (0, 0)
    return pl.pallas_call(
        _hyena_out_kernel,
        grid=(R // tm,),
        in_specs=[pl.BlockSpec((None, tm, N), padded_row), pl.BlockSpec((None, tm, N), padded_row), pl.BlockSpec((tm, N), row),
                  pl.BlockSpec((1, N), fixed), pl.BlockSpec((N, N), fixed), pl.BlockSpec((tm, N), row),
                  pl.BlockSpec((1, N), fixed), pl.BlockSpec((1, N), fixed)],
        out_specs=[pl.BlockSpec((tm, N), row), pl.BlockSpec((tm, N), row)],
        out_shape=[jax.ShapeDtypeStruct((R, N), f32), jax.ShapeDtypeStruct((R, N), bf16)],
        compiler_params=_cparams(("arbitrary",), VMEM_LIMIT),
        name="hyena_out",
    )(yc, vg, x0, skip.reshape(1, N), w, h, g.reshape(1, N), b.reshape(1, N))


def _ffn_in_gate_kernel(x_ref, xp_ref, xn_ref, wg_ref, wa_ref, cw_ref, cb_ref, o_ref, x_scr, g_scr, *, L, tm, tpb):
    i, c = pl.program_id(0), pl.program_id(1)

    @pl.when(c == 0)
    def _():
        _fill_halo_tile(x_scr, x_ref, xp_ref, xn_ref, tm)

    g_scr[...] = jnp.dot(x_scr[...], wg_ref[...], preferred_element_type=f32)
    a = jnp.dot(x_ref[...], wa_ref[...], preferred_element_type=f32)
    _zero_rows_outside_sequence(g_scr, i % tpb, L, tm)
    g = _conv3_centre(g_scr, cw_ref[...], cb_ref[...], tm)
    gelu = 0.5 * g * (1.0 + lax.erf(g * (2.0 ** -0.5)))
    o_ref[...] = (gelu * a).astype(o_ref.dtype)


def ffn_in_gate(x, w_in, conv_w, conv_b, cfg, nchunk=1):
    R, K = x.shape
    tm = cfg.TL
    tc = D_FF // nchunk
    tpb = cfg.Lp // tm
    assert D_FF % nchunk == 0 and tc % LANES == 0 and tm % HALO == 0
    wmode = dict(pipeline_mode=pl.Buffered(1)) if nchunk == 1 else {}
    return pl.pallas_call(
        functools.partial(_ffn_in_gate_kernel, L=cfg.L, tm=tm, tpb=tpb),
        grid=(R // tm, nchunk),
        in_specs=_halo_specs(tm, K, R, lambda i, c: i) + [
                  pl.BlockSpec((K, tc), lambda i, c: (0, c), **wmode),
                  pl.BlockSpec((K, tc), lambda i, c: (0, nchunk + c), **wmode),
                  pl.BlockSpec((3, tc), lambda i, c: (0, c)), pl.BlockSpec((1, tc), lambda i, c: (0, c))],
        out_specs=pl.BlockSpec((tm, tc), lambda i, c: (i, c)),
        out_shape=jax.ShapeDtypeStruct((R, D_FF), bf16),
        scratch_shapes=[pltpu.VMEM((tm + 2 * HALO, K), bf16), pltpu.VMEM((tm + 2 * HALO, tc), f32)],
        compiler_params=_cparams(("arbitrary", "arbitrary"), VMEM_LIMIT),
        name="ffn_in_gate",
    )(x, x, x, w_in, w_in, conv_w, conv_b.reshape(1, -1))


def _dft_tables(cfg):
    N1, NIN, N2, N = cfg.N1, cfg.NIN, FFT_N2, cfg.N
    assert N1 % 2 == 0
    k1 = jnp.arange(cfg.N1H, dtype=jnp.int32)
    wk = jnp.where((k1 == 0) | (k1 == N1 // 2), 1.0, jnp.where(k1 < N1 // 2, 2.0, 0.0)).astype(f32)
    n1 = jnp.arange(NIN, dtype=jnp.int32)
    ang = (2.0 * math.pi / N1) * ((k1[:, None] * n1[None, :]) % N1).astype(f32)
    ca, sa = jnp.cos(ang), jnp.sin(ang)
    fa = jnp.concatenate([ca, -sa], axis=0)
    fa_inv = jnp.concatenate([ca.T * wk, -sa.T * wk], axis=1) * (1.0 / N)
    k2 = jnp.arange(N2, dtype=jnp.int32)[:, None]
    n2 = jnp.arange(N2, dtype=jnp.int32)[None, :]
    t1 = (2.0 * math.pi / N2) * ((n2 * k2) % N2).astype(f32)
    t2 = (2.0 * math.pi / N) * ((n2 * k1[:, None]) % N).astype(f32)
    c1, s1 = jnp.cos(t1)[None], jnp.sin(t1)[None]
    c2, s2 = jnp.cos(t2)[:, None, :], jnp.sin(t2)[:, None, :]
    ar, ai = c1 * c2 - s1 * s2, -(s1 * c2 + c1 * s2)
    g_fwd = jnp.concatenate([jnp.concatenate([ar, -ai], axis=2), jnp.concatenate([ai, ar], axis=2)], axis=1)
    art, ait = jnp.swapaxes(ar, 1, 2), -jnp.swapaxes(ai, 1, 2)
    g_inv = jnp.concatenate([jnp.concatenate([art, -ait], axis=2), jnp.concatenate([ait, art], axis=2)], axis=1)
    return fa.astype(bf16), fa_inv.astype(bf16), g_fwd.astype(bf16), g_inv.astype(bf16)


FFT_KB = 4


def _fft_b_fused_kernel(z_ref, h_ref, gf_ref, gi_ref, o_ref):
    for kb in range(FFT_KB):
        zc = jnp.concatenate([z_ref[0, kb], z_ref[1, kb]], axis=0)
        y = jnp.dot(gf_ref[kb], zc, preferred_element_type=f32)
        yr, yi = y[:FFT_N2], y[FFT_N2:]
        hr, hi = h_ref[0, kb], h_ref[1, kb]
        pc = jnp.concatenate([yr * hr - yi * hi, yr * hi + yi * hr], axis=0).astype(bf16)
        q = jnp.dot(gi_ref[kb], pc, preferred_element_type=f32)
        o_ref[0, kb] = q[:FFT_N2].astype(o_ref.dtype)
        o_ref[1, kb] = q[FFT_N2:].astype(o_ref.dtype)


def fft_b_fused(z5, hspec, g_fwd, g_inv):
    B, _, N1, N2, D = z5.shape
    assert N1 % FFT_KB == 0
    zspec = pl.BlockSpec((None, 2, FFT_KB, N2, D), lambda k, b: (b, 0, k, 0, 0))
    gspec = pl.BlockSpec((FFT_KB, 2 * N2, 2 * N2), lambda k, b: (k, 0, 0))
    return pl.pallas_call(
        _fft_b_fused_kernel,
        grid=(N1 // FFT_KB, B),
        in_specs=[zspec, pl.BlockSpec((2, FFT_KB, N2, D), lambda k, b: (0, k, 0, 0)), gspec, gspec],
        out_specs=zspec,
        out_shape=jax.ShapeDtypeStruct(z5.shape, bf16),
        compiler_params=_cparams(("arbitrary", "arbitrary"), VMEM_LIMIT),
        name="fft_b_fused",
    )(z5, hspec, g_fwd, g_inv)


def _fft_b_filter_kernel(z_ref, csum_ref, gf_ref, o_ref):
    D = D_MODEL
    zc = z_ref[...].reshape(2 * FFT_N2, 2 * D)
    y = jnp.dot(gf_ref[...], zc, preferred_element_type=f32)
    scale = 1.0 / (csum_ref[...] + FILTER_NORM_EPS)
    y = y * scale
    yr, yi = y[:FFT_N2], y[FFT_N2:]
    o_ref[0] = yr[:, :D] + yr[:, D:]
    o_ref[1] = yi[:, :D] - yi[:, D:]


def fft_b_filter(z5, csum, g_fwd):
    _, N1, N2, D2 = z5.shape
    return pl.pallas_call(
        _fft_b_filter_kernel,
        grid=(N1,),
        in_specs=[pl.BlockSpec((2, None, N2, D2), lambda k: (0, k, 0, 0)), pl.BlockSpec((1, D2), lambda k: (0, 0)),
                  pl.BlockSpec((None, 2 * N2, 2 * N2), lambda k: (k, 0, 0))],
        out_specs=pl.BlockSpec((2, None, N2, D2 // 2), lambda k: (0, k, 0, 0)),
        out_shape=jax.ShapeDtypeStruct((2, N1, N2, D2 // 2), f32),
        compiler_params=_cparams(("arbitrary",), VMEM_LIMIT),
        name="fft_b_filter",
    )(z5, csum, g_fwd)


def _filter_mlp_kernel(z_ref, w1_ref, b1_ref, w2_ref, b2_ref, w3_ref, b3_ref, fr_ref, dl_ref, o_ref, cs_ref, *, L, tf):
    i = pl.program_id(0)
    hp = lax.Precision.HIGHEST
    z = z_ref[...]
    fr = fr_ref[...]
    h = jnp.sin(fr * (jnp.dot(z, w1_ref[...], preferred_element_type=f32, precision=hp) + b1_ref[...]))
    h = jnp.sin(fr * (jnp.dot(h, w2_ref[...], preferred_element_type=f32, precision=hp) + b2_ref[...]))
    h = jnp.dot(h.astype(bf16), w3_ref[...].astype(bf16), preferred_element_type=f32) + b3_ref[...]
    win = jnp.exp(-z[:, 0:1] * dl_ref[...]) + DECAY_SHIFT
    h = h * jnp.concatenate([win, win], axis=1)
    rows = i * tf + lax.broadcasted_iota(jnp.int32, (tf, 1), 0)
    h = jnp.where(rows < L, h, 0.0)

    @pl.when(i == 0)
    def _():
        cs_ref[...] = jnp.zeros_like(cs_ref)

    cs_ref[...] += jnp.sum(jnp.abs(h), axis=0, keepdims=True)
    lanes = lax.broadcasted_iota(jnp.int32, (1, h.shape[1]), 1)
    o_ref[...] = jnp.where((rows == 0) & (lanes >= D_MODEL), 0.0, h).astype(o_ref.dtype)


def filter_mlp(z, w1, b1, w2, b2, w3, b3, freq, deltas, cfg, tf=512):
    NR = cfg.NR
    D2 = 2 * D_MODEL
    fixed = lambda i: (0, 0)
    full = lambda a: pl.BlockSpec(a.shape, fixed)
    args = (w1, b1.reshape(1, -1), w2, b2.reshape(1, -1), w3, b3.reshape(1, -1), freq.reshape(1, -1), deltas.reshape(1, -1))
    return pl.pallas_call(
        functools.partial(_filter_mlp_kernel, L=cfg.L, tf=tf),
        grid=(NR // tf,),
        in_specs=[pl.BlockSpec((tf, FILTER_EMB_PAD), lambda i: (i, 0))] + [full(a) for a in args],
        out_specs=[pl.BlockSpec((tf, D2), lambda i: (i, 0)), pl.BlockSpec((1, D2), fixed)],
        out_shape=[jax.ShapeDtypeStruct((NR, D2), bf16), jax.ShapeDtypeStruct((1, D2), f32)],
        compiler_params=_cparams(("arbitrary",), VMEM_LIMIT),
        name="filter_mlp",
    )(z, *args)


def _filter_features(cfg):
    L = cfg.L
    pos = jnp.arange(cfg.NR, dtype=f32)
    t = (pos / (L - 1))[:, None]
    w = (2.0 * math.pi / L) * pos[:, None]
    bands = jnp.linspace(1e-4, FILTER_BANDS - 1, FILTER_BANDS, dtype=f32)
    z = jnp.concatenate([t, jnp.cos(w * bands), -jnp.sin(w * bands)], axis=-1)
    return jnp.pad(z, ((0, 0), (0, FILTER_EMB_PAD - FILTER_EMB_DIM)))


def hyena_filter_spectrum(cfg, tables, w1, b1, w2, b2, w3, b3, freq):
    fa, _, g_fwd, _ = tables
    z = _filter_features(cfg)
    w1p = jnp.pad(w1, ((0, FILTER_EMB_PAD - FILTER_EMB_DIM), (0, 0)))
    max_decay = math.log(1.0 / DECAY_TARGET) / FAST_DECAY_PCT
    min_decay = math.log(1.0 / DECAY_TARGET) / SLOW_DECAY_PCT
    deltas = jnp.linspace(min_decay, max_decay, D_MODEL, dtype=f32)
    xf, csum = filter_mlp(z, w1p, b1, w2, b2, w3, b3, freq, deltas, cfg)
    D2 = 2 * D_MODEL
    z2 = left_matmul(fa, xf.reshape(1, cfg.NIN, FFT_N2 * D2), bf16)
    z5 = z2.reshape(2, cfg.N1H, FFT_N2, D2)
    return fft_b_filter(z5, csum, g_fwd)


def long_conv(vg, hspec, tables, cfg):
    fa, fa_inv, g_fwd, g_inv = tables
    B = vg.shape[0]
    D = D_MODEL
    z2 = left_matmul(fa, vg.reshape(B, cfg.NIN, FFT_N2 * D), bf16)
    q5 = fft_b_fused(z2.reshape(B, 2, cfg.N1H, FFT_N2, D), hspec, g_fwd, g_inv)
    y2 = left_matmul(fa_inv, q5.reshape(B, 2 * cfg.N1H, FFT_N2 * D), bf16)
    return y2.reshape(B, cfg.NR, D)


def _qkv_proj_kernel(x_ref, w_ref, cc_ref, ss_ref, qg_ref, kg_ref, q_ref, k_ref, v_ref, *, L, tm, tpb, sub):
    nq, nk = N_HEADS * HEAD_DIM, N_KV_HEADS * HEAD_DIM
    for r in range(0, tm, sub):
        rows = slice(r, r + sub)
        t = (pl.program_id(0) % tpb) * tm + r + lax.broadcasted_iota(jnp.int32, (sub, 1), 0)
        valid = t < L
        cc, ss = cc_ref[rows, :], ss_ref[rows, :]
        qkv = jnp.dot(x_ref[rows, :], w_ref[...], preferred_element_type=f32)

        def norm_rope(x, gain, scale):
            xn = x * lax.rsqrt(jnp.mean(x * x, axis=-1, keepdims=True) + RMS_EPS) * gain
            y = xn * cc + pltpu.roll(xn, HEAD_DIM // 2, axis=1) * ss
            return jnp.where(valid, y * scale, 0.0).astype(bf16)

        for hh in range(N_HEADS):
            sl = slice(hh * HEAD_DIM, (hh + 1) * HEAD_DIM)
            q_ref[rows, sl] = norm_rope(qkv[:, sl], qg_ref[...], LOG2E * HEAD_DIM ** -0.5)
        for hh in range(N_KV_HEADS):
            sl = slice(hh * HEAD_DIM, (hh + 1) * HEAD_DIM)
            k_ref[rows, sl] = norm_rope(qkv[:, nq + hh * HEAD_DIM:nq + (hh + 1) * HEAD_DIM], kg_ref[...], 1.0)
        v_ref[rows, :] = jnp.where(valid, qkv[:, nq + nk:], 0.0).astype(bf16)


def qkv_proj(x, w, cc, ss, q_gain, k_gain, cfg):
    R, K = x.shape
    tm, tpb = cfg.TL, cfg.Lp // cfg.TL
    nq, nk = N_HEADS * HEAD_DIM, N_KV_HEADS * HEAD_DIM
    sub = _row_subtile(tm)
    row = lambda i: (i, 0)
    tab = lambda i: (i % tpb, 0)
    fixed = lambda i: (0, 0)
    return pl.pallas_call(
        functools.partial(_qkv_proj_kernel, L=cfg.L, tm=tm, tpb=tpb, sub=sub),
        grid=(R // tm,),
        in_specs=[pl.BlockSpec((tm, K), row), pl.BlockSpec((K, nq + 2 * nk), fixed), pl.BlockSpec((tm, HEAD_DIM), tab),
                  pl.BlockSpec((tm, HEAD_DIM), tab), pl.BlockSpec((1, HEAD_DIM), fixed), pl.BlockSpec((1, HEAD_DIM), fixed)],
        out_specs=[pl.BlockSpec((tm, nq), row), pl.BlockSpec((tm, nk), row), pl.BlockSpec((tm, nk), row)],
        out_shape=[jax.ShapeDtypeStruct((R, nq), bf16), jax.ShapeDtypeStruct((R, nk), bf16),
                   jax.ShapeDtypeStruct((R, nk), bf16)],
        compiler_params=_cparams(("arbitrary",), VMEM_LIMIT),
        name="qkv_proj",
    )(x, w, cc, ss, q_gain.reshape(1, -1), k_gain.reshape(1, -1))


def _flash_kernel(q_ref, k_ref, v_ref, o_ref, s0_scr, s1_scr, p0_scr, p1_scr, *, L):
    s_bufs, p_bufs = (s0_scr, s1_scr), (p0_scr, p1_scr)
    lp = k_ref.shape[0]
    c0 = (L // LANES) * LANES
    for g in range(GROUP):
        s_scr, p_scr = s_bufs[g % 2], p_bufs[g % 2]
        q = q_ref[:, g * HEAD_DIM:(g + 1) * HEAD_DIM]
        s = lax.dot_general(q, k_ref[...], (((1,), (1,)), ((), ())), preferred_element_type=f32)
        if c0 < lp:
            cols = c0 + lax.broadcasted_iota(jnp.int32, (1, lp - c0), 1)
            s_scr[:, :c0] = s[:, :c0]
            s_scr[:, c0:] = jnp.where(cols < L, s[:, c0:], NEG_BIG)
        else:
            s_scr[...] = s
        s = s_scr[...]
        p = jnp.exp2(s - jnp.max(s, axis=-1, keepdims=True))
        l = jnp.sum(p, axis=-1, keepdims=True)
        p_scr[...] = p.astype(bf16)
        o = jnp.dot(p_scr[...], v_ref[...], preferred_element_type=f32)
        o_ref[:, g * HEAD_DIM:(g + 1) * HEAD_DIM] = (o / l).astype(o_ref.dtype)


def flash_attention(q, k, v, cfg):
    B = q.shape[0]
    tq = cfg.TQ
    lk = -(-cfg.L // LANES) * LANES
    gw = GROUP * HEAD_DIM
    return pl.pallas_call(
        functools.partial(_flash_kernel, L=cfg.L),
        grid=(B, N_KV_HEADS, cfg.Lp // tq),
        in_specs=[pl.BlockSpec((None, tq, gw), lambda b, h, i: (b, i, h)),
                  pl.BlockSpec((None, lk, HEAD_DIM), lambda b, h, i: (b, 0, h)),
                  pl.BlockSpec((None, lk, HEAD_DIM), lambda b, h, i: (b, 0, h))],
        out_specs=pl.BlockSpec((None, tq, gw), lambda b, h, i: (b, i, h)),
        out_shape=jax.ShapeDtypeStruct(q.shape, bf16),
        scratch_shapes=[pltpu.VMEM((tq, lk), f32), pltpu.VMEM((tq, lk), f32),
                        pltpu.VMEM((tq, lk), bf16), pltpu.VMEM((tq, lk), bf16)],
        compiler_params=_cparams(("arbitrary", "arbitrary", "arbitrary"), VMEM_LIMIT),
        name="flash",
    )(q, k, v)


def _rope_tables(cfg):
    n = cfg.L - N_META
    rows = n // GRID_W
    row = jnp.concatenate([jnp.full((N_META,), -1.0, f32), jnp.repeat(jnp.arange(rows, dtype=f32), GRID_W)])
    col = jnp.concatenate([jnp.arange(N_META, dtype=f32), jnp.tile(jnp.arange(GRID_W, dtype=f32), rows)])
    axis_rot = HEAD_DIM // 2
    inv_freq = ROPE_THETA ** (-jnp.arange(0, axis_rot, 2, dtype=f32) / axis_rot)
    ang = jnp.concatenate([row[:, None] * inv_freq, col[:, None] * inv_freq], axis=-1)
    ang = jnp.pad(ang, ((0, cfg.Lp - cfg.L), (0, 0)))
    c, s = jnp.cos(ang), jnp.sin(ang)
    return jnp.concatenate([c, c], axis=-1), jnp.concatenate([-s, s], axis=-1)


def _run_trunk(x, cfg, meta_tokens, hy, at, ln, ffn):
    B, n, D = x.shape
    Lp = cfg.Lp
    R = B * Lp
    meta = jnp.broadcast_to(meta_tokens[None], (B, N_META, D))
    h = jnp.concatenate([meta, x, jnp.zeros((B, Lp - cfg.L, D), x.dtype)], axis=1).reshape(R, D)
    hb = h.astype(bf16)
    cc, ss = _rope_tables(cfg)
    tables = _dft_tables(cfg)
    for i in range(DEPTH):
        j = i // 2
        if i % 2 == 0:
            hspec = hyena_filter_spectrum(cfg, tables, hy["f_w1"][j], hy["f_b1"][j], hy["f_w2"][j], hy["f_b2"][j],
                                          hy["f_w3"][j], hy["f_b3"][j], hy["f_freq"][j])
            x0c, vg = hyena_in(hb, hy["w_in"][j], hy["conv_w"][j], hy["conv_b"][j], cfg)
            yc = long_conv(vg, hspec, tables, cfg)
            h, hb = hyena_out(yc, vg, x0c, hy["skip"][j], hy["w_out"][j], h,
                              ln["g1"][i], ln["b1"][i], cfg)
        else:
            q, k, v = qkv_proj(hb, at["w_qkv"][j], cc, ss, at["q_gain"][j], at["k_gain"][j], cfg)
            o = flash_attention(q.reshape(B, Lp, -1), k.reshape(B, Lp, -1), v.reshape(B, Lp, -1), cfg)
            h, hb = matmul_residual_ln(o.reshape(R, D), at["w_out"][j], h, ln["g1"][i], ln["b1"][i], cfg.TL)
        gated = ffn_in_gate(hb, ffn["w_in"][i], ffn["conv_w"][i], ffn["conv_b"][i], cfg)
        h, hb = matmul_residual_ln(gated, ffn["w_out"][i], h, ln["g2"][i], ln["b2"][i], cfg.TL)
    return h.reshape(B, Lp, D)[:, N_META:cfg.L]


def kernel(x_prompt, x_sample, meta_tokens, hy_w_in, hy_conv_w, hy_conv_b, hy_filt_w1, hy_filt_b1, hy_filt_w2, hy_filt_b2, hy_filt_w3, hy_filt_b3, hy_filt_freq, hy_skip, hy_w_out, at_w_qkv, at_q_gain, at_k_gain, at_w_out, ln1_g, ln1_b, ln2_g, ln2_b, ffn_w_in, ffn_conv_w, ffn_conv_b, ffn_w_out):
    hy = dict(w_in=hy_w_in.astype(bf16), conv_w=hy_conv_w, conv_b=hy_conv_b, f_w1=hy_filt_w1, f_b1=hy_filt_b1,
              f_w2=hy_filt_w2, f_b2=hy_filt_b2, f_w3=hy_filt_w3, f_b3=hy_filt_b3, f_freq=hy_filt_freq, skip=hy_skip,
              w_out=hy_w_out.astype(bf16))
    at = dict(w_qkv=at_w_qkv.astype(bf16), q_gain=at_q_gain, k_gain=at_k_gain, w_out=at_w_out.astype(bf16))
    ln = dict(g1=ln1_g, b1=ln1_b, g2=ln2_g, b2=ln2_b)
    ffn = dict(w_in=ffn_w_in.astype(bf16), conv_w=ffn_conv_w, conv_b=ffn_conv_b, w_out=ffn_w_out.astype(bf16))
    y_prompt = _run_trunk(x_prompt, Cfg(x_prompt.shape[1]), meta_tokens, hy, at, ln, ffn)
    y_sample = _run_trunk(x_sample, Cfg(x_sample.shape[1]), meta_tokens, hy, at, ln, ffn)
    return (y_prompt, y_sample)
```

```python
import functools
import math

import jax
import jax.numpy as jnp
from jax import lax
from jax.experimental import pallas as pl
from jax.experimental.pallas import tpu as pltpu

f32 = jnp.float32
bf16 = jnp.bfloat16

D_MODEL = 1024
DEPTH = 4
N_META = 16
GRID_W = 64
HEAD_DIM = 128
N_HEADS = 8
N_KV_HEADS = 2
GROUP = N_HEADS // N_KV_HEADS
ROPE_THETA = 10000.0
FILTER_EMB_DIM = 33
FILTER_EMB_PAD = 40
FILTER_BANDS = 16
DECAY_TARGET = 1e-2
FAST_DECAY_PCT = 0.3
SLOW_DECAY_PCT = 1.5
DECAY_SHIFT = 0.05
FILTER_NORM_EPS = 1e-6
D_FF = 2816
DEEPNORM_ALPHA = (2 * DEPTH) ** 0.25
LN_EPS = 1e-5
RMS_EPS = 1e-6
NEG_BIG = -1e30
LOG2E = 1.4426950408889634

LANES = 128
SUBLANES = 8
FFT_N2 = 128
V7X_VMEM_BYTES = 64 * 1024 * 1024
VMEM_LIMIT = 52 * 1024 * 1024
assert VMEM_LIMIT < V7X_VMEM_BYTES


class Cfg:
    def __init__(self, n):
        self.L = n + N_META
        if n == 8192:
            self.Lp, self.TL, self.TQ, self.N1, self.NIN = 8448, 768, 256, 136, 80
        elif n == 2048:
            self.Lp, self.TL, self.TQ, self.N1, self.NIN = 2304, 768, 768, 40, 32
        else:
            up = lambda a, m: -(-a // m) * m
            self.TL = self.TQ = 128
            self.Lp = up(self.L, LANES)
            self.N1 = up(-(-(2 * self.L - 1) // FFT_N2), 8)
            self.NIN = up(-(-self.Lp // FFT_N2), 16)
        self.NR = self.NIN * FFT_N2
        self.N1H = -(-(self.N1 // 2 + 1) // SUBLANES) * SUBLANES
        self.N = self.N1 * FFT_N2
        assert self.N >= 2 * self.L - 1 and self.NR >= self.Lp
        assert self.Lp % self.TL == 0 and self.Lp % self.TQ == 0


def _cparams(sem, vmem=None):
    return pltpu.CompilerParams(dimension_semantics=sem, vmem_limit_bytes=vmem)


def _residual_ln(m, h_ref, g_ref, b_ref, o_ref):
    y = DEEPNORM_ALPHA * h_ref[...] + m
    mu = jnp.mean(y, axis=-1, keepdims=True)
    yc = y - mu
    var = jnp.mean(yc * yc, axis=-1, keepdims=True)
    out = yc * lax.rsqrt(var + LN_EPS) * g_ref[...] + b_ref[...]
    o_ref[...] = out


def _row_subtile(tm):
    sub = 256 if tm % 256 == 0 else tm // 2
    assert tm % sub == 0 and sub % 16 == 0
    return sub


def _mm_ln_kernel(x_ref, w_ref, h_ref, g_ref, b_ref, o_ref):
    sub = _row_subtile(x_ref.shape[0])
    for r in range(0, x_ref.shape[0], sub):
        rows = pl.ds(r, sub)
        _residual_ln(jnp.dot(x_ref[rows, :], w_ref[...], preferred_element_type=f32),
                     h_ref.at[rows, :], g_ref, b_ref, o_ref.at[rows, :])


def matmul_residual_ln(x, w, h, g, b, tm):
    R, K = x.shape
    N = w.shape[1]
    row = lambda i: (i, 0)
    fixed = lambda i: (0, 0)
    return pl.pallas_call(
        _mm_ln_kernel,
        grid=(R // tm,),
        in_specs=[pl.BlockSpec((tm, K), row), pl.BlockSpec((K, N), fixed), pl.BlockSpec((tm, N), row),
                  pl.BlockSpec((1, N), fixed), pl.BlockSpec((1, N), fixed)],
        out_specs=pl.BlockSpec((tm, N), row),
        out_shape=jax.ShapeDtypeStruct((R, N), f32),
        compiler_params=_cparams(("arbitrary",), VMEM_LIMIT),
        name="mm_ln",
    )(x, w, h, g.reshape(1, N), b.reshape(1, N))


def _left_mm_kernel(a_ref, x_ref, o_ref, *, cw):
    a = a_ref[...]
    for c in range(0, x_ref.shape[1], cw):
        o_ref[:, c:c + cw] = jnp.dot(a, x_ref[:, c:c + cw], preferred_element_type=f32).astype(o_ref.dtype)


def left_matmul(a, x, out_dtype, tn=32768, cw=512):
    M, K = a.shape
    B, _, C = x.shape
    assert C % tn == 0 and tn % cw == 0
    return pl.pallas_call(
        functools.partial(_left_mm_kernel, cw=cw),
        grid=(B, C // tn),
        in_specs=[pl.BlockSpec((M, K), lambda b, j: (0, 0)), pl.BlockSpec((None, K, tn), lambda b, j: (b, 0, j))],
        out_specs=pl.BlockSpec((None, M, tn), lambda b, j: (b, 0, j)),
        out_shape=jax.ShapeDtypeStruct((B, M, C), out_dtype),
        compiler_params=_cparams(("arbitrary", "arbitrary"), VMEM_LIMIT),
        name="left_mm",
    )(a, x)


HALO = 16


def _halo_specs(tm, K, R, row_block):
    nh = tm // HALO
    last = R // HALO - 1
    return [pl.BlockSpec((tm, K), lambda *g: (row_block(*g), 0)),
            pl.BlockSpec((HALO, K), lambda *g: (jnp.maximum(row_block(*g) * nh - 1, 0), 0)),
            pl.BlockSpec((HALO, K), lambda *g: (jnp.minimum((row_block(*g) + 1) * nh, last), 0))]


def _fill_halo_tile(x_scr, x_ref, xp_ref, xn_ref, tm):
    x_scr[0:HALO, :] = xp_ref[...].astype(bf16)
    x_scr[HALO:HALO + tm, :] = x_ref[...].astype(bf16)
    x_scr[HALO + tm:, :] = xn_ref[...].astype(bf16)


def _zero_rows_outside_sequence(u_scr, j, L, tm):
    assert L % tm != 0
    j_end, r_end = L // tm, HALO + L % tm
    g_end = (r_end // SUBLANES) * SUBLANES
    sub = lax.broadcasted_iota(jnp.int32, (SUBLANES, 1), 0)
    first = slice(HALO - SUBLANES, HALO)
    end = slice(g_end, g_end + SUBLANES)
    u_scr[first, :] = jnp.where((j == 0) & (sub == SUBLANES - 1), 0.0, u_scr[first, :])
    u_scr[end, :] = jnp.where((j == j_end) & (sub == r_end - g_end), 0.0, u_scr[end, :])


def _conv3_centre(u_scr, cw, cb, tm):
    return (u_scr[HALO - 1:HALO - 1 + tm, :] * cw[0:1, :] + u_scr[HALO:HALO + tm, :] * cw[1:2, :]
            + u_scr[HALO + 1:HALO + 1 + tm, :] * cw[2:3, :] + cb)


def _hyena_in_kernel(x_ref, xp_ref, xn_ref, w_ref, cw_ref, cb_ref, x0_ref, vg_ref, x_scr, u0_scr, u1_scr, uv_scr,
                     *, L, tm, tpb):
    j = pl.program_id(1)
    D = D_MODEL

    @pl.when(j < tpb)
    def _():
        _fill_halo_tile(x_scr, x_ref, xp_ref, xn_ref, tm)
        xs = x_scr[...]
        t = j * tm + lax.broadcasted_iota(jnp.int32, (tm, 1), 0)
        cw, cb = cw_ref[...], cb_ref[...]
        conv = []
        for s, scr in enumerate((u0_scr, u1_scr, uv_scr)):
            sl = slice(s * D, (s + 1) * D)
            scr[...] = jnp.dot(xs, w_ref[:, sl], preferred_element_type=f32)
            _zero_rows_outside_sequence(scr, j, L, tm)
            conv.append(_conv3_centre(scr, cw[:, sl], cb[:, sl], tm))
        x0_ref[...] = conv[0].astype(x0_ref.dtype)
        vg_ref[...] = jnp.where(t < L, conv[1] * conv[2], 0.0).astype(vg_ref.dtype)

    @pl.when(j >= tpb)
    def _():
        vg_ref[...] = jnp.zeros_like(vg_ref)


def hyena_in(x, w_in, conv_w, conv_b, cfg):
    R, K = x.shape
    D = D_MODEL
    tm, tpb, ntr = cfg.TL, cfg.Lp // cfg.TL, pl.cdiv(cfg.NR, cfg.TL)
    B = R // cfg.Lp
    row_block = lambda b, j: b * tpb + jnp.minimum(j, tpb - 1)
    fixed = lambda b, j: (0, 0)
    return pl.pallas_call(
        functools.partial(_hyena_in_kernel, L=cfg.L, tm=tm, tpb=tpb),
        grid=(B, ntr),
        in_specs=_halo_specs(tm, K, R, row_block) +
                 [pl.BlockSpec((K, 3 * D), fixed, pipeline_mode=pl.Buffered(1)),
                  pl.BlockSpec((3, 3 * D), fixed), pl.BlockSpec((1, 3 * D), fixed)],
        out_specs=[pl.BlockSpec((tm, D), lambda b, j: (row_block(b, j), 0)),
                   pl.BlockSpec((None, tm, D), lambda b, j: (b, j, 0))],
        out_shape=[jax.ShapeDtypeStruct((R, D), bf16), jax.ShapeDtypeStruct((B, cfg.NR, D), bf16)],
        scratch_shapes=[pltpu.VMEM((tm + 2 * HALO, K), bf16)] + [pltpu.VMEM((tm + 2 * HALO, D), f32)] * 3,
        compiler_params=_cparams(("arbitrary", "arbitrary"), VMEM_LIMIT),
        name="hyena_in",
    )(x, x, x, w_in, conv_w, conv_b.reshape(1, -1))


def _hyena_out_kernel(y_ref, vg_ref, x0_ref, skip_ref, w_ref, h_ref, g_ref, b_ref, o_ref):
    y = y_ref[...].astype(f32) + vg_ref[...].astype(f32) * skip_ref[...]
    x = (y * x0_ref[...].astype(f32)).astype(bf16)
    _residual_ln(jnp.dot(x, w_ref[...], preferred_element_type=f32), h_ref, g_ref, b_ref, o_ref)


def hyena_out(yc, vg, x0, skip, w, h, g, b, cfg):
    R, N = h.shape
    tm, tpb = cfg.TL, cfg.Lp // cfg.TL
    row = lambda i: (i, 0)
    padded_row = lambda i: (i // tpb, i % tpb, 0)
    fixed = lambda i: (0, 0)
    return pl.pallas_call(
        _hyena_out_kernel,
        grid=(R // tm,),
        in_specs=[pl.BlockSpec((None, tm, N), padded_row), pl.BlockSpec((None, tm, N), padded_row), pl.BlockSpec((tm, N), row),
                  pl.BlockSpec((1, N), fixed), pl.BlockSpec((N, N), fixed), pl.BlockSpec((tm, N), row),
                  pl.BlockSpec((1, N), fixed), pl.BlockSpec((1, N), fixed)],
        out_specs=pl.BlockSpec((tm, N), row),
        out_shape=jax.ShapeDtypeStruct((R, N), f32),
        compiler_params=_cparams(("arbitrary",), VMEM_LIMIT),
        name="hyena_out",
    )(yc, vg, x0, skip.reshape(1, N), w, h, g.reshape(1, N), b.reshape(1, N))


def _ffn_in_gate_kernel(x_ref, xp_ref, xn_ref, wg_ref, wa_ref, cw_ref, cb_ref, o_ref, x_scr, g_scr, *, L, tm, tpb):
    i, c = pl.program_id(0), pl.program_id(1)

    @pl.when(c == 0)
    def _():
        _fill_halo_tile(x_scr, x_ref, xp_ref, xn_ref, tm)

    g_scr[...] = jnp.dot(x_scr[...], wg_ref[...], preferred_element_type=f32)
    a = jnp.dot(x_scr[HALO:HALO + tm, :], wa_ref[...], preferred_element_type=f32)
    _zero_rows_outside_sequence(g_scr, i % tpb, L, tm)
    g = _conv3_centre(g_scr, cw_ref[...], cb_ref[...], tm)
    gelu = 0.5 * g * (1.0 + lax.erf(g * (2.0 ** -0.5)))
    o_ref[...] = (gelu * a).astype(o_ref.dtype)


def ffn_in_gate(x, w_in, conv_w, conv_b, cfg, nchunk=1):
    R, K = x.shape
    tm = cfg.TL
    tc = D_FF // nchunk
    tpb = cfg.Lp // tm
    assert D_FF % nchunk == 0 and tc % LANES == 0 and tm % HALO == 0
    wmode = dict(pipeline_mode=pl.Buffered(1)) if nchunk == 1 else {}
    return pl.pallas_call(
        functools.partial(_ffn_in_gate_kernel, L=cfg.L, tm=tm, tpb=tpb),
        grid=(R // tm, nchunk),
        in_specs=_halo_specs(tm, K, R, lambda i, c: i) + [
                  pl.BlockSpec((K, tc), lambda i, c: (0, c), **wmode),
                  pl.BlockSpec((K, tc), lambda i, c: (0, nchunk + c), **wmode),
                  pl.BlockSpec((3, tc), lambda i, c: (0, c)), pl.BlockSpec((1, tc), lambda i, c: (0, c))],
        out_specs=pl.BlockSpec((tm, tc), lambda i, c: (i, c)),
        out_shape=jax.ShapeDtypeStruct((R, D_FF), bf16),
        scratch_shapes=[pltpu.VMEM((tm + 2 * HALO, K), bf16), pltpu.VMEM((tm + 2 * HALO, tc), f32)],
        compiler_params=_cparams(("arbitrary", "arbitrary"), VMEM_LIMIT),
        name="ffn_in_gate",
    )(x, x, x, w_in, w_in, conv_w, conv_b.reshape(1, -1))


def _dft_tables(cfg):
    N1, NIN, N2, N = cfg.N1, cfg.NIN, FFT_N2, cfg.N
    assert N1 % 2 == 0
    k1 = jnp.arange(cfg.N1H, dtype=jnp.int32)
    wk = jnp.where((k1 == 0) | (k1 == N1 // 2), 1.0, jnp.where(k1 < N1 // 2, 2.0, 0.0)).astype(f32)
    n1 = jnp.arange(NIN, dtype=jnp.int32)
    ang = (2.0 * math.pi / N1) * ((k1[:, None] * n1[None, :]) % N1).astype(f32)
    ca, sa = jnp.cos(ang), jnp.sin(ang)
    fa = jnp.concatenate([ca, -sa], axis=0)
    fa_inv = jnp.concatenate([ca.T * wk, -sa.T * wk], axis=1) * (1.0 / N)
    k2 = jnp.arange(N2, dtype=jnp.int32)[:, None]
    n2 = jnp.arange(N2, dtype=jnp.int32)[None, :]
    t1 = (2.0 * math.pi / N2) * ((n2 * k2) % N2).astype(f32)
    t2 = (2.0 * math.pi / N) * ((n2 * k1[:, None]) % N).astype(f32)
    c1, s1 = jnp.cos(t1)[None], jnp.sin(t1)[None]
    c2, s2 = jnp.cos(t2)[:, None, :], jnp.sin(t2)[:, None, :]
    ar, ai = c1 * c2 - s1 * s2, -(s1 * c2 + c1 * s2)
    g_fwd = jnp.concatenate([jnp.concatenate([ar, -ai], axis=2), jnp.concatenate([ai, ar], axis=2)], axis=1)
    art, ait = jnp.swapaxes(ar, 1, 2), -jnp.swapaxes(ai, 1, 2)
    g_inv = jnp.concatenate([jnp.concatenate([art, -ait], axis=2), jnp.concatenate([ait, art], axis=2)], axis=1)
    return fa.astype(bf16), fa_inv.astype(bf16), g_fwd.astype(bf16), g_inv.astype(bf16)


FFT_KB = 4


def _fft_b_fused_kernel(z_ref, h_ref, gf_ref, gi_ref, o_ref):
    for kb in range(FFT_KB):
        zc = jnp.concatenate([z_ref[0, kb], z_ref[1, kb]], axis=0)
        y = jnp.dot(gf_ref[kb], zc, preferred_element_type=f32)
        yr, yi = y[:FFT_N2], y[FFT_N2:]
        hr, hi = h_ref[0, kb], h_ref[1, kb]
        pc = jnp.concatenate([yr * hr - yi * hi, yr * hi + yi * hr], axis=0).astype(bf16)
        q = jnp.dot(gi_ref[kb], pc, preferred_element_type=f32)
        o_ref[0, kb] = q[:FFT_N2].astype(o_ref.dtype)
        o_ref[1, kb] = q[FFT_N2:].astype(o_ref.dtype)


def fft_b_fused(z5, hspec, g_fwd, g_inv):
    B, _, N1, N2, D = z5.shape
    assert N1 % FFT_KB == 0
    zspec = pl.BlockSpec((None, 2, FFT_KB, N2, D), lambda k, b: (b, 0, k, 0, 0))
    gspec = pl.BlockSpec((FFT_KB, 2 * N2, 2 * N2), lambda k, b: (k, 0, 0))
    return pl.pallas_call(
        _fft_b_fused_kernel,
        grid=(N1 // FFT_KB, B),
        in_specs=[zspec, pl.BlockSpec((2, FFT_KB, N2, D), lambda k, b: (0, k, 0, 0)), gspec, gspec],
        out_specs=zspec,
        out_shape=jax.ShapeDtypeStruct(z5.shape, bf16),
        compiler_params=_cparams(("arbitrary", "arbitrary"), VMEM_LIMIT),
        name="fft_b_fused",
    )(z5, hspec, g_fwd, g_inv)


def _fft_b_filter_kernel(z_ref, csum_ref, gf_ref, o_ref):
    D = D_MODEL
    zc = z_ref[...].reshape(2 * FFT_N2, 2 * D)
    y = jnp.dot(gf_ref[...], zc, preferred_element_type=f32)
    scale = 1.0 / (csum_ref[...] + FILTER_NORM_EPS)
    y = y * scale
    yr, yi = y[:FFT_N2], y[FFT_N2:]
    o_ref[0] = yr[:, :D] + yr[:, D:]
    o_ref[1] = yi[:, :D] - yi[:, D:]


def fft_b_filter(z5, csum, g_fwd):
    _, N1, N2, D2 = z5.shape
    return pl.pallas_call(
        _fft_b_filter_kernel,
        grid=(N1,),
        in_specs=[pl.BlockSpec((2, None, N2, D2), lambda k: (0, k, 0, 0)), pl.BlockSpec((1, D2), lambda k: (0, 0)),
                  pl.BlockSpec((None, 2 * N2, 2 * N2), lambda k: (k, 0, 0))],
        out_specs=pl.BlockSpec((2, None, N2, D2 // 2), lambda k: (0, k, 0, 0)),
        out_shape=jax.ShapeDtypeStruct((2, N1, N2, D2 // 2), f32),
        compiler_params=_cparams(("arbitrary",), VMEM_LIMIT),
        name="fft_b_filter",
    )(z5, csum, g_fwd)


def _filter_mlp_kernel(z_ref, w1_ref, b1_ref, w2_ref, b2_ref, w3_ref, b3_ref, fr_ref, dl_ref, o_ref, cs_ref, *, L, tf):
    i = pl.program_id(0)
    hp = lax.Precision.HIGHEST
    z = z_ref[...]
    fr = fr_ref[...]
    h = jnp.sin(fr * (jnp.dot(z, w1_ref[...], preferred_element_type=f32, precision=hp) + b1_ref[...]))
    h = jnp.sin(fr * (jnp.dot(h, w2_ref[...], preferred_element_type=f32, precision=hp) + b2_ref[...]))
    h = jnp.dot(h.astype(bf16), w3_ref[...].astype(bf16), preferred_element_type=f32) + b3_ref[...]
    win = jnp.exp(-z[:, 0:1] * dl_ref[...]) + DECAY_SHIFT
    h = h * jnp.concatenate([win, win], axis=1)
    rows = i * tf + lax.broadcasted_iota(jnp.int32, (tf, 1), 0)
    h = jnp.where(rows < L, h, 0.0)

    @pl.when(i == 0)
    def _():
        cs_ref[...] = jnp.zeros_like(cs_ref)

    cs_ref[...] += jnp.sum(jnp.abs(h), axis=0, keepdims=True)
    lanes = lax.broadcasted_iota(jnp.int32, (1, h.shape[1]), 1)
    o_ref[...] = jnp.where((rows == 0) & (lanes >= D_MODEL), 0.0, h).astype(o_ref.dtype)


def filter_mlp(z, w1, b1, w2, b2, w3, b3, freq, deltas, cfg, tf=512):
    NR = cfg.NR
    D2 = 2 * D_MODEL
    fixed = lambda i: (0, 0)
    full = lambda a: pl.BlockSpec(a.shape, fixed)
    args = (w1, b1.reshape(1, -1), w2, b2.reshape(1, -1), w3, b3.reshape(1, -1), freq.reshape(1, -1), deltas.reshape(1, -1))
    return pl.pallas_call(
        functools.partial(_filter_mlp_kernel, L=cfg.L, tf=tf),
        grid=(NR // tf,),
        in_specs=[pl.BlockSpec((tf, FILTER_EMB_PAD), lambda i: (i, 0))] + [full(a) for a in args],
        out_specs=[pl.BlockSpec((tf, D2), lambda i: (i, 0)), pl.BlockSpec((1, D2), fixed)],
        out_shape=[jax.ShapeDtypeStruct((NR, D2), bf16), jax.ShapeDtypeStruct((1, D2), f32)],
        compiler_params=_cparams(("arbitrary",), VMEM_LIMIT),
        name="filter_mlp",
    )(z, *args)


def _filter_features(cfg):
    L = cfg.L
    pos = jnp.arange(cfg.NR, dtype=f32)
    t = (pos / (L - 1))[:, None]
    w = (2.0 * math.pi / L) * pos[:, None]
    bands = jnp.linspace(1e-4, FILTER_BANDS - 1, FILTER_BANDS, dtype=f32)
    z = jnp.concatenate([t, jnp.cos(w * bands), -jnp.sin(w * bands)], axis=-1)
    return jnp.pad(z, ((0, 0), (0, FILTER_EMB_PAD - FILTER_EMB_DIM)))


def hyena_filter_spectrum(cfg, tables, w1, b1, w2, b2, w3, b3, freq):
    fa, _, g_fwd, _ = tables
    z = _filter_features(cfg)
    w1p = jnp.pad(w1, ((0, FILTER_EMB_PAD - FILTER_EMB_DIM), (0, 0)))
    max_decay = math.log(1.0 / DECAY_TARGET) / FAST_DECAY_PCT
    min_decay = math.log(1.0 / DECAY_TARGET) / SLOW_DECAY_PCT
    deltas = jnp.linspace(min_decay, max_decay, D_MODEL, dtype=f32)
    xf, csum = filter_mlp(z, w1p, b1, w2, b2, w3, b3, freq, deltas, cfg)
    D2 = 2 * D_MODEL
    z2 = left_matmul(fa, xf.reshape(1, cfg.NIN, FFT_N2 * D2), bf16)
    z5 = z2.reshape(2, cfg.N1H, FFT_N2, D2)
    return fft_b_filter(z5, csum, g_fwd)


def long_conv(vg, hspec, tables, cfg):
    fa, fa_inv, g_fwd, g_inv = tables
    B = vg.shape[0]
    D = D_MODEL
    z2 = left_matmul(fa, vg.reshape(B, cfg.NIN, FFT_N2 * D), bf16)
    q5 = fft_b_fused(z2.reshape(B, 2, cfg.N1H, FFT_N2, D), hspec, g_fwd, g_inv)
    y2 = left_matmul(fa_inv, q5.reshape(B, 2 * cfg.N1H, FFT_N2 * D), bf16)
    return y2.reshape(B, cfg.NR, D)


def _qkv_proj_kernel(x_ref, w_ref, cc_ref, ss_ref, qg_ref, kg_ref, q_ref, k_ref, v_ref, *, L, tm, tpb, sub):
    nq, nk = N_HEADS * HEAD_DIM, N_KV_HEADS * HEAD_DIM
    for r in range(0, tm, sub):
        rows = slice(r, r + sub)
        t = (pl.program_id(0) % tpb) * tm + r + lax.broadcasted_iota(jnp.int32, (sub, 1), 0)
        valid = t < L
        cc, ss = cc_ref[rows, :], ss_ref[rows, :]
        qkv = jnp.dot(x_ref[rows, :].astype(bf16), w_ref[...], preferred_element_type=f32)

        def norm_rope(x, gain, scale):
            xn = x * lax.rsqrt(jnp.mean(x * x, axis=-1, keepdims=True) + RMS_EPS) * gain
            y = xn * cc + pltpu.roll(xn, HEAD_DIM // 2, axis=1) * ss
            return jnp.where(valid, y * scale, 0.0).astype(bf16)

        for hh in range(N_HEADS):
            sl = slice(hh * HEAD_DIM, (hh + 1) * HEAD_DIM)
            q_ref[rows, sl] = norm_rope(qkv[:, sl], qg_ref[...], LOG2E * HEAD_DIM ** -0.5)
        for hh in range(N_KV_HEADS):
            sl = slice(hh * HEAD_DIM, (hh + 1) * HEAD_DIM)
            k_ref[rows, sl] = norm_rope(qkv[:, nq + hh * HEAD_DIM:nq + (hh + 1) * HEAD_DIM], kg_ref[...], 1.0)
        v_ref[rows, :] = jnp.where(valid, qkv[:, nq + nk:], 0.0).astype(bf16)


def qkv_proj(x, w, cc, ss, q_gain, k_gain, cfg):
    R, K = x.shape
    tm, tpb = cfg.TL, cfg.Lp // cfg.TL
    nq, nk = N_HEADS * HEAD_DIM, N_KV_HEADS * HEAD_DIM
    sub = _row_subtile(tm)
    row = lambda i: (i, 0)
    tab = lambda i: (i % tpb, 0)
    fixed = lambda i: (0, 0)
    return pl.pallas_call(
        functools.partial(_qkv_proj_kernel, L=cfg.L, tm=tm, tpb=tpb, sub=sub),
        grid=(R // tm,),
        in_specs=[pl.BlockSpec((tm, K), row), pl.BlockSpec((K, nq + 2 * nk), fixed), pl.BlockSpec((tm, HEAD_DIM), tab),
                  pl.BlockSpec((tm, HEAD_DIM), tab), pl.BlockSpec((1, HEAD_DIM), fixed), pl.BlockSpec((1, HEAD_DIM), fixed)],
        out_specs=[pl.BlockSpec((tm, nq), row), pl.BlockSpec((tm, nk), row), pl.BlockSpec((tm, nk), row)],
        out_shape=[jax.ShapeDtypeStruct((R, nq), bf16), jax.ShapeDtypeStruct((R, nk), bf16),
                   jax.ShapeDtypeStruct((R, nk), bf16)],
        compiler_params=_cparams(("arbitrary",), VMEM_LIMIT),
        name="qkv_proj",
    )(x, w, cc, ss, q_gain.reshape(1, -1), k_gain.reshape(1, -1))


def _flash_kernel(q_ref, k_ref, v_ref, o_ref, s0_scr, s1_scr, p0_scr, p1_scr, *, L):
    s_bufs, p_bufs = (s0_scr, s1_scr), (p0_scr, p1_scr)
    lp = k_ref.shape[0]
    c0 = (L // LANES) * LANES
    for g in range(GROUP):
        s_scr, p_scr = s_bufs[g % 2], p_bufs[g % 2]
        q = q_ref[:, g * HEAD_DIM:(g + 1) * HEAD_DIM]
        s = lax.dot_general(q, k_ref[...], (((1,), (1,)), ((), ())), preferred_element_type=f32)
        if c0 < lp:
            cols = c0 + lax.broadcasted_iota(jnp.int32, (1, lp - c0), 1)
            s_scr[:, :c0] = s[:, :c0]
            s_scr[:, c0:] = jnp.where(cols < L, s[:, c0:], NEG_BIG)
        else:
            s_scr[...] = s
        s = s_scr[...]
        p = jnp.exp2(s - jnp.max(s, axis=-1, keepdims=True))
        l = jnp.sum(p, axis=-1, keepdims=True)
        p_scr[...] = p.astype(bf16)
        o = jnp.dot(p_scr[...], v_ref[...], preferred_element_type=f32)
        o_ref[:, g * HEAD_DIM:(g + 1) * HEAD_DIM] = (o / l).astype(o_ref.dtype)


def flash_attention(q, k, v, cfg):
    B = q.shape[0]
    tq = cfg.TQ
    lk = -(-cfg.L // LANES) * LANES
    gw = GROUP * HEAD_DIM
    return pl.pallas_call(
        functools.partial(_flash_kernel, L=cfg.L),
        grid=(B, N_KV_HEADS, cfg.Lp // tq),
        in_specs=[pl.BlockSpec((None, tq, gw), lambda b, h, i: (b, i, h)),
                  pl.BlockSpec((None, lk, HEAD_DIM), lambda b, h, i: (b, 0, h)),
                  pl.BlockSpec((None, lk, HEAD_DIM), lambda b, h, i: (b, 0, h))],
        out_specs=pl.BlockSpec((None, tq, gw), lambda b, h, i: (b, i, h)),
        out_shape=jax.ShapeDtypeStruct(q.shape, bf16),
        scratch_shapes=[pltpu.VMEM((tq, lk), f32), pltpu.VMEM((tq, lk), f32),
                        pltpu.VMEM((tq, lk), bf16), pltpu.VMEM((tq, lk), bf16)],
        compiler_params=_cparams(("arbitrary", "arbitrary", "arbitrary"), VMEM_LIMIT),
        name="flash",
    )(q, k, v)


def _rope_tables(cfg):
    n = cfg.L - N_META
    rows = n // GRID_W
    row = jnp.concatenate([jnp.full((N_META,), -1.0, f32), jnp.repeat(jnp.arange(rows, dtype=f32), GRID_W)])
    col = jnp.concatenate([jnp.arange(N_META, dtype=f32), jnp.tile(jnp.arange(GRID_W, dtype=f32), rows)])
    axis_rot = HEAD_DIM // 2
    inv_freq = ROPE_THETA ** (-jnp.arange(0, axis_rot, 2, dtype=f32) / axis_rot)
    ang = jnp.concatenate([row[:, None] * inv_freq, col[:, None] * inv_freq], axis=-1)
    ang = jnp.pad(ang, ((0, cfg.Lp - cfg.L), (0, 0)))
    c, s = jnp.cos(ang), jnp.sin(ang)
    return jnp.concatenate([c, c], axis=-1), jnp.concatenate([-s, s], axis=-1)


def _run_trunk(x, cfg, meta_tokens, hy, at, ln, ffn):
    B, n, D = x.shape
    Lp = cfg.Lp
    R = B * Lp
    meta = jnp.broadcast_to(meta_tokens[None], (B, N_META, D))
    h = jnp.concatenate([meta, x, jnp.zeros((B, Lp - cfg.L, D), x.dtype)], axis=1).reshape(R, D)
    cc, ss = _rope_tables(cfg)
    tables = _dft_tables(cfg)
    for i in range(DEPTH):
        j = i // 2
        if i % 2 == 0:
            hspec = hyena_filter_spectrum(cfg, tables, hy["f_w1"][j], hy["f_b1"][j], hy["f_w2"][j], hy["f_b2"][j],
                                          hy["f_w3"][j], hy["f_b3"][j], hy["f_freq"][j])
            x0c, vg = hyena_in(h,hy["w_in"][j], hy["conv_w"][j], hy["conv_b"][j], cfg)
            yc = long_conv(vg, hspec, tables, cfg)
            h = hyena_out(yc, vg, x0c, hy["skip"][j], hy["w_out"][j], h,
                              ln["g1"][i], ln["b1"][i], cfg)
        else:
            q, k, v = qkv_proj(h,at["w_qkv"][j], cc, ss, at["q_gain"][j], at["k_gain"][j], cfg)
            o = flash_attention(q.reshape(B, Lp, -1), k.reshape(B, Lp, -1), v.reshape(B, Lp, -1), cfg)
            h = matmul_residual_ln(o.reshape(R, D), at["w_out"][j], h, ln["g1"][i], ln["b1"][i], cfg.TL)
        gated = ffn_in_gate(h,ffn["w_in"][i], ffn["conv_w"][i], ffn["conv_b"][i], cfg)
        h = matmul_residual_ln(gated, ffn["w_out"][i], h, ln["g2"][i], ln["b2"][i], cfg.TL)
    return h.reshape(B, Lp, D)[:, N_META:cfg.L]


def kernel(x_prompt, x_sample, meta_tokens, hy_w_in, hy_conv_w, hy_conv_b, hy_filt_w1, hy_filt_b1, hy_filt_w2, hy_filt_b2, hy_filt_w3, hy_filt_b3, hy_filt_freq, hy_skip, hy_w_out, at_w_qkv, at_q_gain, at_k_gain, at_w_out, ln1_g, ln1_b, ln2_g, ln2_b, ffn_w_in, ffn_conv_w, ffn_conv_b, ffn_w_out):
    hy = dict(w_in=hy_w_in.astype(bf16), conv_w=hy_conv_w, conv_b=hy_conv_b, f_w1=hy_filt_w1, f_b1=hy_filt_b1,
              f_w2=hy_filt_w2, f_b2=hy_filt_b2, f_w3=hy_filt_w3, f_b3=hy_filt_b3, f_freq=hy_filt_freq, skip=hy_skip,
              w_out=hy_w_out.astype(bf16))
    at = dict(w_qkv=at_w_qkv.astype(bf16), q_gain=at_q_gain, k_gain=at_k_gain, w_out=at_w_out.astype(bf16))
    ln = dict(g1=ln1_g, b1=ln1_b, g2=ln2_g, b2=ln2_b)
    ffn = dict(w_in=ffn_w_in.astype(bf16), conv_w=ffn_conv_w, conv_b=ffn_conv_b, w_out=ffn_w_out.astype(bf16))
    y_prompt = _run_trunk(x_prompt, Cfg(x_prompt.shape[1]), meta_tokens, hy, at, ln, ffn)
    y_sample = _run_trunk(x_sample, Cfg(x_sample.shape[1]), meta_tokens, hy, at, ln, ffn)
    return (y_prompt, y_sample)
```
